```python
import math
import jax
import jax.numpy as jnp
from jax import lax
import numpy as np

D_MODEL = 2048
BATCH = 2
SEQ = 4096
DEPTH = 4
DEC_BATCH = 8
DEC_SEQ = 8
PAST_LEN = 16384
PAGE_SIZE = 128

ATT_HEADS = 8
ATT_HEAD_DIM = 128
ATT_WIDTH = ATT_HEADS * ATT_HEAD_DIM
DILATED_PATTERNS = ((128, 1), (512, 4), (2048, 16))
ATT_WINDOW_MAX = 2048
ATT_BLOCK = 128
ATT_SCALE = ATT_HEAD_DIM ** -0.5
N_REL_BUCKETS = 32
REL_MAX_DIST = 2048
SSM_WIDTH = 512
SSM_GROUP = 16
SSM_GROUPS = SSM_WIDTH // SSM_GROUP
SSM_STATE = 64
SGU_WIDTH = 512
SGU_HEADS = 4
SGU_HEAD_DIM = SGU_WIDTH // SGU_HEADS
SGU_CHUNK = 128
MIX_WIDTH = ATT_WIDTH + SSM_WIDTH + SGU_WIDTH
IN_COLS = 3 * ATT_WIDTH + SSM_WIDTH + 2 * SGU_WIDTH
FFN_HIDDEN = -((-8 * D_MODEL) // (3 * 256)) * 256
EPS = 1e-6
NEG_INF = -1e30

kernel_name = 'hybrid_dilated_s5_gmlp_decoder_step'


def rmsnorm(x, g):
    xf = x.astype(jnp.float32)
    y = xf * lax.rsqrt(jnp.mean(xf * xf, axis=-1, keepdims=True) + EPS)
    return (y * g.astype(jnp.float32)).astype(x.dtype)


def layernorm_gain(x, g):
    xf = x.astype(jnp.float32)
    xc = xf - jnp.mean(xf, axis=-1, keepdims=True)
    y = xc * lax.rsqrt(jnp.mean(xc * xc, axis=-1, keepdims=True) + EPS)
    return (y * g.astype(jnp.float32)).astype(x.dtype)


def t5_bucket(dist):
    max_exact = N_REL_BUCKETS // 2
    df = jnp.maximum(dist, 1).astype(jnp.float32)
    large = max_exact + (jnp.log(df / max_exact) / math.log(REL_MAX_DIST / max_exact)
                         * (N_REL_BUCKETS - max_exact)).astype(jnp.int32)
    large = jnp.minimum(large, N_REL_BUCKETS - 1)
    return jnp.where(dist < max_exact, dist, large)


def strided_bias(rel_bias, dilation, n_steps):
    dist = jnp.arange(n_steps + 1, dtype=jnp.int32) * dilation
    return rel_bias[t5_bucket(dist)].astype(jnp.float32)


def mix_by_denominators(outs, lses):
    wts = jax.nn.softmax(jnp.stack(lses, axis=0), axis=0)
    return sum(wts[i][..., None] * outs[i].astype(jnp.float32) for i in range(len(outs)))


def dilated_attn_prompt(q, k, v, rel_bias):
    N, L, H, E = q.shape
    QB = ATT_BLOCK
    outs, lses = [], []
    for window, d in DILATED_PATTERNS:
        nk = window // d
        Ls = L // d
        Lp = -(-Ls // QB) * QB
        nb = Lp // QB

        def sub(t):
            t = t.reshape(N, Ls, d, H, E)
            return jnp.pad(t, ((0, 0), (0, Lp - Ls), (0, 0), (0, 0), (0, 0)))

        def band(t):
            t = jnp.pad(t, ((0, 0), (QB, 0), (0, 0), (0, 0), (0, 0))).reshape(N, nb + 1, QB, d, H, E)
            return jnp.concatenate([t[:, :-1], t[:, 1:]], axis=2)

        qb = sub(q).reshape(N, nb, QB, d, H, E)
        kb, vb = band(sub(k)), band(sub(v))
        s = jnp.einsum('nbqrhe,nbkrhe->nbrhqk', qb, kb).astype(jnp.float32) * ATT_SCALE
        qi = jnp.arange(QB)[:, None]
        kj = jnp.arange(2 * QB)[None, :]
        delta = QB + qi - kj
        band_ok = (delta >= 0) & (delta <= nk)
        blk = jnp.arange(nb)[:, None, None]
        valid = band_ok[None] & ((blk > 0) | (kj[None] >= QB))
        bias = strided_bias(rel_bias, d, nk)[jnp.clip(delta, 0, nk)]
        s = s + jnp.transpose(bias, (2, 0, 1))[None, None, None]
        s = jnp.where(valid[None, :, None, None], s, NEG_INF)
        lse = jax.nn.logsumexp(s, axis=-1)
        p = jnp.exp(s - lse[..., None])
        o = jnp.einsum('nbrhqk,nbkrhe->nbqrhe', p.astype(vb.dtype), vb)
        o = o.reshape(N, Lp, d, H, E)[:, :Ls].reshape(N, L, H, E)
        lse = jnp.transpose(lse, (0, 1, 4, 2, 3)).reshape(N, Lp, d, H)[:, :Ls].reshape(N, L, H)
        outs.append(o)
        lses.append(lse)
    return mix_by_denominators(outs, lses)


def dilated_attn_sample(q, k_all, v_all, rel_bias):
    N, T, H, E = q.shape
    W = k_all.shape[1] - T
    outs, lses = [], []
    for window, d in DILATED_PATTERNS:
        nk = window // d
        j = jnp.arange(nk + 1)
        idx = W + jnp.arange(T)[:, None] - j[None, :] * d
        valid = idx >= 0
        idxc = jnp.maximum(idx, 0)
        kg = k_all[:, idxc]
        vg = v_all[:, idxc]
        s = jnp.einsum('nthe,ntjhe->nthj', q, kg).astype(jnp.float32) * ATT_SCALE
        s = s + strided_bias(rel_bias, d, nk).T[None, None]
        s = jnp.where(valid[None, :, None, :], s, NEG_INF)
        lse = jax.nn.logsumexp(s, axis=-1)
        p = jnp.exp(s - lse[..., None])
        outs.append(jnp.einsum('nthj,ntjhe->nthe', p.astype(vg.dtype), vg))
        lses.append(lse)
    return mix_by_denominators(outs, lses)


def s5_scan(u, h0_re, h0_im, lam_re, lam_im, log_step, b_re, b_im, c_re, c_im, d_skip):
    f32 = jnp.float32
    u = u.astype(f32)
    lam_re = lam_re.astype(f32)
    lam_im = lam_im.astype(f32)
    step = jnp.exp(log_step.astype(f32))[:, None]
    mag = jnp.exp(lam_re * step)
    a_re = mag * jnp.cos(lam_im * step)
    a_im = mag * jnp.sin(lam_im * step)
    den = lam_re * lam_re + lam_im * lam_im
    coef_re = ((a_re - 1.0) * lam_re + a_im * lam_im) / den
    coef_im = (a_im * lam_re - (a_re - 1.0) * lam_im) / den
    b_re = b_re.astype(f32)
    b_im = b_im.astype(f32)
    bb_re = coef_re[..., None] * b_re - coef_im[..., None] * b_im
    bb_im = coef_re[..., None] * b_im + coef_im[..., None] * b_re
    x_re = jnp.einsum('nlgc,gpc->nlgp', u, bb_re)
    x_im = jnp.einsum('nlgc,gpc->nlgp', u, bb_im)
    h0_re = h0_re.astype(f32)
    h0_im = h0_im.astype(f32)
    x_re = x_re.at[:, 0].add(a_re * h0_re - a_im * h0_im)
    x_im = x_im.at[:, 0].add(a_re * h0_im + a_im * h0_re)
    ar = jnp.broadcast_to(a_re, x_re.shape)
    ai = jnp.broadcast_to(a_im, x_im.shape)

    def combine(e1, e2):
        a1r, a1i, b1r, b1i = e1
        a2r, a2i, b2r, b2i = e2
        return (a1r * a2r - a1i * a2i, a1r * a2i + a1i * a2r,
                a2r * b1r - a2i * b1i + b2r, a2r * b1i + a2i * b1r + b2i)

    _, _, h_re, h_im = lax.associative_scan(combine, (ar, ai, x_re, x_im), axis=1)
    y = (jnp.einsum('gcp,nlgp->nlgc', c_re.astype(f32), h_re)
         - jnp.einsum('gcp,nlgp->nlgc', c_im.astype(f32), h_im))
    y = y + d_skip.astype(f32).reshape(SSM_GROUPS, SSM_GROUP) * u
    return y, h_re[:, -1], h_im[:, -1]


def chunk_sgu(u, v, w_sp, b_sp):
    N, L = u.shape[:2]
    Lp = -(-L // SGU_CHUNK) * SGU_CHUNK
    vp = jnp.pad(v, ((0, 0), (0, Lp - L), (0, 0), (0, 0)))
    vc = vp.reshape(N, Lp // SGU_CHUNK, SGU_CHUNK, SGU_HEADS, SGU_HEAD_DIM)
    causal = jnp.tril(jnp.ones((SGU_CHUNK, SGU_CHUNK), w_sp.dtype))
    mixed = jnp.einsum('hts,ncshe->ncthe', w_sp * causal, vc) + b_sp.T[None, None, :, :, None]
    mixed = mixed.reshape(N, Lp, SGU_HEADS, SGU_HEAD_DIM)[:, :L]
    return u * mixed


def trunk_layer(x, p, rel_bias, cache_k=None, cache_v=None, h_re=None, h_im=None):
    N, L, _ = x.shape
    h = rmsnorm(x, p['g_pre_mix'])
    z = h @ p['w_in']
    A = ATT_WIDTH
    q = z[..., :A].reshape(N, L, ATT_HEADS, ATT_HEAD_DIM)
    k = z[..., A:2 * A].reshape(N, L, ATT_HEADS, ATT_HEAD_DIM)
    v = z[..., 2 * A:3 * A].reshape(N, L, ATT_HEADS, ATT_HEAD_DIM)
    o0 = 3 * A
    u_ssm = z[..., o0:o0 + SSM_WIDTH]
    u_sgu = z[..., o0 + SSM_WIDTH:o0 + SSM_WIDTH + SGU_WIDTH]
    v_sgu = z[..., o0 + SSM_WIDTH + SGU_WIDTH:]
    if cache_k is None:
        o_att = dilated_attn_prompt(q, k, v, rel_bias)
        n_keep = min(ATT_WINDOW_MAX, L)
        new_k, new_v = k[:, L - n_keep:], v[:, L - n_keep:]
        h_re = jnp.zeros((N, SSM_GROUPS, SSM_STATE), jnp.float32)
        h_im = jnp.zeros((N, SSM_GROUPS, SSM_STATE), jnp.float32)
    else:
        k_all = jnp.concatenate([cache_k.astype(k.dtype), k], axis=1)
        v_all = jnp.concatenate([cache_v.astype(v.dtype), v], axis=1)
        o_att = dilated_attn_sample(q, k_all, v_all, rel_bias)
        new_k, new_v = k, v
    o_att = o_att.reshape(N, L, ATT_WIDTH).astype(x.dtype)
    y_ssm, hr, hi = s5_scan(u_ssm.reshape(N, L, SSM_GROUPS, SSM_GROUP), h_re, h_im,
                            p['ssm_lam_re'], p['ssm_lam_im'], p['ssm_log_step'],
                            p['ssm_b_re'], p['ssm_b_im'], p['ssm_c_re'], p['ssm_c_im'], p['ssm_d'])
    y_ssm = y_ssm.reshape(N, L, SSM_WIDTH)
    o_ssm = y_ssm * jax.nn.sigmoid(y_ssm @ p['ssm_w_glu'].astype(jnp.float32)
                                   + p['ssm_b_glu'].astype(jnp.float32))
    o_ssm = o_ssm.astype(x.dtype)
    gu = jax.nn.gelu(u_sgu)
    gv = layernorm_gain(jax.nn.gelu(v_sgu), p['sgu_g'])
    o_sgu = chunk_sgu(gu.reshape(N, L, SGU_HEADS, SGU_HEAD_DIM),
                      gv.reshape(N, L, SGU_HEADS, SGU_HEAD_DIM),
                      p['sgu_w'], p['sgu_b']).reshape(N, L, SGU_WIDTH)
    g = p['g_mix_out']
    mixed = jnp.concatenate([rmsnorm(o_att, g[:A]),
                             rmsnorm(o_ssm, g[A:A + SSM_WIDTH]),
                             rmsnorm(o_sgu, g[A + SSM_WIDTH:])], axis=-1)
    x = x + rmsnorm(mixed @ p['w_out'], p['g_post_mix'])
    h = rmsnorm(x, p['g_pre_ffn'])
    f = (jax.nn.silu(h @ p['w_gate']) * (h @ p['w_up'])) @ p['w_down']
    x = x + rmsnorm(f, p['g_post_ffn'])
    return x, (new_k, new_v, hr, hi, gv)


def setup_inputs(seed: int = 0) -> dict:
    key = jax.random.key(seed)
    k = jax.random.split(key, 30)
    f32 = jnp.float32

    def nrm(kk, shape, scale):
        return jax.random.normal(kk, shape, f32) * scale

    def gain(kk, shape):
        return 1.0 + 0.02 * jax.random.normal(kk, shape, f32)

    w_buf = min(ATT_WINDOW_MAX, PAST_LEN)
    G, P, CG = SSM_GROUPS, SSM_STATE, SSM_GROUP
    n_idx = jnp.arange(P, dtype=f32)[None, None, :]
    return {
        'x_prompt': nrm(k[0], (BATCH, SEQ, D_MODEL), 1.0),
        'x_sample': nrm(k[1], (DEC_BATCH, DEC_SEQ, D_MODEL), 1.0),
        'cache_attn_k': nrm(k[2], (DEPTH, DEC_BATCH, w_buf, ATT_HEADS, ATT_HEAD_DIM), 1.0),
        'cache_attn_v': nrm(k[3], (DEPTH, DEC_BATCH, w_buf, ATT_HEADS, ATT_HEAD_DIM), 1.0),
        'state_ssm_re': nrm(k[4], (DEPTH, DEC_BATCH, G, P), 0.5),
        'state_ssm_im': nrm(k[5], (DEPTH, DEC_BATCH, G, P), 0.5),
        'rel_bias': nrm(k[6], (N_REL_BUCKETS, ATT_HEADS), 0.5),
        'w_in': nrm(k[7], (DEPTH, D_MODEL, IN_COLS), D_MODEL ** -0.5),
        'w_out': nrm(k[8], (DEPTH, MIX_WIDTH, D_MODEL), MIX_WIDTH ** -0.5),
        'g_pre_mix': gain(k[9], (DEPTH, D_MODEL)),
        'g_post_mix': gain(k[10], (DEPTH, D_MODEL)),
        'g_mix_out': gain(k[11], (DEPTH, MIX_WIDTH)),
        'ssm_lam_re': -0.5 + nrm(k[12], (DEPTH, G, P), 0.01),
        'ssm_lam_im': math.pi * n_idx + nrm(k[13], (DEPTH, G, P), 0.01),
        'ssm_log_step': jax.random.uniform(k[14], (DEPTH, G), f32, math.log(1e-3), math.log(1e-1)),
        'ssm_b_re': nrm(k[15], (DEPTH, G, P, CG), (2 * CG) ** -0.5),
        'ssm_b_im': nrm(k[16], (DEPTH, G, P, CG), (2 * CG) ** -0.5),
        'ssm_c_re': nrm(k[17], (DEPTH, G, CG, P), P ** -0.5),
        'ssm_c_im': nrm(k[18], (DEPTH, G, CG, P), P ** -0.5),
        'ssm_d': nrm(k[19], (DEPTH, SSM_WIDTH), 1.0),
        'ssm_w_glu': nrm(k[20], (DEPTH, SSM_WIDTH, SSM_WIDTH), SSM_WIDTH ** -0.5),
        'ssm_b_glu': nrm(k[21], (DEPTH, SSM_WIDTH), 0.02),
        'sgu_g': gain(k[22], (DEPTH, SGU_WIDTH)),
        'sgu_w': nrm(k[23], (DEPTH, SGU_HEADS, SGU_CHUNK, SGU_CHUNK), SGU_CHUNK ** -0.5),
        'sgu_b': gain(k[24], (DEPTH, SGU_HEADS, SGU_CHUNK)),
        'g_pre_ffn': gain(k[25], (DEPTH, D_MODEL)),
        'g_post_ffn': gain(k[26], (DEPTH, D_MODEL)),
        'w_gate': nrm(k[27], (DEPTH, D_MODEL, FFN_HIDDEN), D_MODEL ** -0.5),
        'w_up': nrm(k[28], (DEPTH, D_MODEL, FFN_HIDDEN), D_MODEL ** -0.5),
        'w_down': nrm(k[29], (DEPTH, FFN_HIDDEN, D_MODEL), FFN_HIDDEN ** -0.5),
    }


def reference(x_prompt, x_sample, cache_attn_k, cache_attn_v, state_ssm_re, state_ssm_im,
              rel_bias, w_in, w_out, g_pre_mix, g_post_mix, g_mix_out,
              ssm_lam_re, ssm_lam_im, ssm_log_step, ssm_b_re, ssm_b_im, ssm_c_re, ssm_c_im,
              ssm_d, ssm_w_glu, ssm_b_glu, sgu_g, sgu_w, sgu_b,
              g_pre_ffn, g_post_ffn, w_gate, w_up, w_down):
    xp, xs = x_prompt, x_sample
    kp_l, vp_l, rp_l, ip_l = [], [], [], []
    ks_l, vs_l, rs_l, is_l, us_l = [], [], [], [], []
    for l in range(DEPTH):
        p = {'w_in': w_in[l], 'w_out': w_out[l], 'g_pre_mix': g_pre_mix[l],
             'g_post_mix': g_post_mix[l], 'g_mix_out': g_mix_out[l],
             'ssm_lam_re': ssm_lam_re[l], 'ssm_lam_im': ssm_lam_im[l],
             'ssm_log_step': ssm_log_step[l], 'ssm_b_re': ssm_b_re[l], 'ssm_b_im': ssm_b_im[l],
             'ssm_c_re': ssm_c_re[l], 'ssm_c_im': ssm_c_im[l], 'ssm_d': ssm_d[l],
             'ssm_w_glu': ssm_w_glu[l], 'ssm_b_glu': ssm_b_glu[l],
             'sgu_g': sgu_g[l], 'sgu_w': sgu_w[l], 'sgu_b': sgu_b[l],
             'g_pre_ffn': g_pre_ffn[l], 'g_post_ffn': g_post_ffn[l],
             'w_gate': w_gate[l], 'w_up': w_up[l], 'w_down': w_down[l]}
        xp, (nk, nv, hr, hi, _) = trunk_layer(xp, p, rel_bias)
        kp_l.append(nk)
        vp_l.append(nv)
        rp_l.append(hr)
        ip_l.append(hi)
        xs, (nk, nv, hr, hi, sv) = trunk_layer(xs, p, rel_bias, cache_attn_k[l], cache_attn_v[l],
                                               state_ssm_re[l], state_ssm_im[l])
        ks_l.append(nk)
        vs_l.append(nv)
        rs_l.append(hr)
        is_l.append(hi)
        us_l.append(sv)
    return (xp, xs, jnp.stack(kp_l), jnp.stack(vp_l), jnp.stack(rp_l), jnp.stack(ip_l),
            jnp.stack(ks_l), jnp.stack(vs_l), jnp.stack(rs_l), jnp.stack(is_l), jnp.stack(us_l))
```

```python
import functools
import math

import jax
import jax.numpy as jnp
from jax import lax
from jax.experimental import pallas as pl
from jax.experimental.pallas import tpu as pltpu

F32 = jnp.float32
BF16 = jnp.bfloat16

D_MODEL = 2048
DEPTH = 4
ATT_HEADS = 8
ATT_HEAD_DIM = 128
ATT_WIDTH = ATT_HEADS * ATT_HEAD_DIM
DILATED_PATTERNS = ((128, 1), (512, 4), (2048, 16))
ATT_BLOCK = 128
ATT_SCALE = ATT_HEAD_DIM ** -0.5
N_REL_BUCKETS = 32
REL_MAX_DIST = 2048
SSM_WIDTH = 512
SSM_GROUP = 16
SSM_GROUPS = SSM_WIDTH // SSM_GROUP
SSM_STATE = 64
SSM_QUARTERS = 4
SSM_QW = SSM_GROUPS * SSM_STATE // SSM_QUARTERS
SSM_QC = SSM_WIDTH // SSM_QUARTERS
SGU_WIDTH = 512
SGU_HEADS = 4
SGU_HEAD_DIM = SGU_WIDTH // SGU_HEADS
SGU_CHUNK = 128
IN_COLS = 3 * ATT_WIDTH + SSM_WIDTH + 2 * SGU_WIDTH
FFN_HIDDEN = 5632
EPS = 1e-6
NEG_INF = -1e30
LANES = 128

VMEM_LIMIT_BYTES = 56 * 1024 * 1024


def _params(*sem):
    return pltpu.CompilerParams(dimension_semantics=sem, vmem_limit_bytes=VMEM_LIMIT_BYTES)


def _rms(x, g):
    return x * lax.rsqrt(jnp.mean(x * x, axis=-1, keepdims=True) + EPS) * g


def _in_proj_kernel(x_ref, g_ref, w_ref, z_ref, h_ref):
    @pl.when(pl.program_id(1) == 0)
    def _():
        h_ref[...] = _rms(x_ref[...], g_ref[...]).astype(BF16)

    z_ref[...] = jnp.dot(h_ref[...], w_ref[...], preferred_element_type=F32)


def in_proj(x, g, w, tm, tn):
    m, k = x.shape
    n = w.shape[1]
    return pl.pallas_call(
        _in_proj_kernel,
        grid=(m // tm, n // tn),
        in_specs=[pl.BlockSpec((tm, k), lambda i, j: (i, 0)),
                  pl.BlockSpec((1, k), lambda i, j: (0, 0)),
                  pl.BlockSpec((k, tn), lambda i, j: (0, j))],
        out_specs=pl.BlockSpec((tm, tn), lambda i, j: (i, j)),
        out_shape=jax.ShapeDtypeStruct((m, n), F32),
        scratch_shapes=[pltpu.VMEM((tm, k), BF16)],
        compiler_params=_params("parallel", "arbitrary"),
        name="in_proj",
    )(x, g, w)


def _attn_prompt_kernel(q_ref, k_ref, v_ref, b_ref, o_ref, m_ref, l_ref, *, seq):
    qb = ATT_BLOCK

    for p, (_, d) in enumerate(DILATED_PATTERNS):
        shift = int(math.log2(d))
        n_blocks = seq // qb

        def block(i, carry, p=p, d=d, shift=shift):
            r = i & (d - 1)
            b = i >> shift
            first = b == 0
            q_start = r + b * (qb * d)
            k_start = r + jnp.maximum(b - 1, 0) * (qb * d)
            if d == 1:
                q_rows = pl.ds(pl.multiple_of(q_start, qb), qb)
                k_rows = pl.ds(pl.multiple_of(k_start, qb), 2 * qb)
            else:
                q_rows = pl.ds(q_start, qb, stride=d)
                k_rows = pl.ds(k_start, 2 * qb, stride=d)
            q = q_ref[0, q_rows, :].astype(BF16)
            k = k_ref[0, k_rows, :].astype(BF16)
            v = v_ref[0, k_rows, :].astype(BF16)
            s = lax.dot_general(q, k, (((1,), (1,)), ((), ())), preferred_element_type=F32)
            s = s * ATT_SCALE + b_ref[p, first.astype(jnp.int32), 0]
            m_blk = jnp.broadcast_to(jnp.max(s, axis=1, keepdims=True), (qb, ATT_HEAD_DIM))
            if p == 0:
                m_new = m_blk
            else:
                m_old = m_ref[q_rows, :]
                m_new = jnp.maximum(m_old, m_blk)
            pe = jnp.exp(s - jnp.concatenate([m_new, m_new], axis=1))
            l_blk = jnp.broadcast_to(jnp.sum(pe, axis=1, keepdims=True), (qb, ATT_HEAD_DIM))
            acc = jnp.dot(pe.astype(BF16), v, preferred_element_type=F32)
            if p != 0:
                alpha = jnp.exp(m_old - m_new)
                l_blk = l_blk + alpha * l_ref[q_rows, :]
                acc = acc + alpha * o_ref[0, q_rows, :]
            m_ref[q_rows, :] = m_new
            l_ref[q_rows, :] = l_blk
            o_ref[0, q_rows, :] = acc
            return carry

        lax.fori_loop(0, n_blocks, block, 0)

    o_ref[0] = o_ref[0] / l_ref[...]


def attn_prompt(z3, bias):
    n, seq, _ = z3.shape
    h = ATT_HEADS
    kern = functools.partial(_attn_prompt_kernel, seq=seq)
    return pl.pallas_call(
        kern,
        grid=(n, h),
        in_specs=[pl.BlockSpec((1, seq, ATT_HEAD_DIM), lambda i, j: (i, 0, j)),
                  pl.BlockSpec((1, seq, ATT_HEAD_DIM), lambda i, j: (i, 0, h + j)),
                  pl.BlockSpec((1, seq, ATT_HEAD_DIM), lambda i, j: (i, 0, 2 * h + j)),
                  pl.BlockSpec((3, 2, 1, ATT_BLOCK, 2 * ATT_BLOCK), lambda i, j: (0, 0, j, 0, 0))],
        out_specs=pl.BlockSpec((1, seq, ATT_HEAD_DIM), lambda i, j: (i, 0, j)),
        out_shape=jax.ShapeDtypeStruct((n, seq, ATT_WIDTH), F32),
        scratch_shapes=[pltpu.VMEM((seq, ATT_HEAD_DIM), F32),
                        pltpu.VMEM((seq, ATT_HEAD_DIM), F32)],
        compiler_params=_params("parallel", "parallel"),
        name="attn_prompt",
    )(z3, z3, z3, bias)


def _attn_sample_kernel(q_ref, kn_ref, vn_ref, kc_ref, vc_ref, bc_ref, bn_ref, o_ref):
    t = q_ref.shape[1]
    pad = jnp.zeros((ATT_BLOCK - t, ATT_HEAD_DIM), F32)
    q = q_ref[0].astype(BF16)
    kn = jnp.concatenate([kn_ref[0], pad], axis=0).astype(BF16)
    vn = jnp.concatenate([vn_ref[0], pad], axis=0).astype(BF16)
    kc = kc_ref[0, 0].astype(BF16)
    vc = vc_ref[0, 0].astype(BF16)
    nt = (((1,), (1,)), ((), ()))
    sc = lax.dot_general(q, kc, nt, preferred_element_type=F32) * ATT_SCALE
    sn = lax.dot_general(q, kn, nt, preferred_element_type=F32) * ATT_SCALE
    n_pat = len(DILATED_PATTERNS)
    scs = [sc + bc_ref[p, 0] for p in range(n_pat)]
    sns = [sn + bn_ref[p, 0] for p in range(n_pat)]
    m = None
    for s in scs + sns:
        mx = jnp.max(s, axis=1, keepdims=True)
        m = mx if m is None else jnp.maximum(m, mx)
    pc = sum(jnp.exp(s - m) for s in scs)
    pn = sum(jnp.exp(s - m) for s in sns)
    l = jnp.sum(pc, axis=1, keepdims=True) + jnp.sum(pn, axis=1, keepdims=True)
    acc = (jnp.dot(pc.astype(BF16), vc, preferred_element_type=F32)
           + jnp.dot(pn.astype(BF16), vn, preferred_element_type=F32))
    o_ref[0] = acc / l


def attn_sample(zs3, cache_k, cache_v, layer, bias_c, bias_n):
    n, t, _ = zs3.shape
    w = cache_k.shape[2]
    h = ATT_HEADS
    e = ATT_HEAD_DIM
    return pl.pallas_call(
        _attn_sample_kernel,
        grid=(n, h),
        in_specs=[pl.BlockSpec((1, t, e), lambda i, j: (i, 0, j)),
                  pl.BlockSpec((1, t, e), lambda i, j: (i, 0, h + j)),
                  pl.BlockSpec((1, t, e), lambda i, j: (i, 0, 2 * h + j)),
                  pl.BlockSpec((1, 1, w, e), lambda i, j: (layer, i, 0, j)),
                  pl.BlockSpec((1, 1, w, e), lambda i, j: (layer, i, 0, j)),
                  pl.BlockSpec((3, 1, t, w), lambda i, j: (0, j, 0, 0)),
                  pl.BlockSpec((3, 1, t, ATT_BLOCK), lambda i, j: (0, j, 0, 0))],
        out_specs=pl.BlockSpec((1, t, e), lambda i, j: (i, 0, j)),
        out_shape=jax.ShapeDtypeStruct((n, t, ATT_WIDTH), F32),
        compiler_params=_params("parallel", "parallel"),
        name="attn_sample",
    )(zs3, zs3, zs3, cache_k, cache_v, bias_c, bias_n)


def _ssm_prompt_kernel(u_ref, bq_ref, a_ref, cq_ref, d_ref, wg_ref, bg_ref,
                       o_ref, hfin_ref, x_ref, h_ref, *, tb, pitch):
    step = pl.program_id(0)
    nq = SSM_QUARTERS
    qw = SSM_QW
    qc = SSM_QC

    @pl.when(step == 0)
    def _():
        h_ref[...] = jnp.zeros_like(h_ref)

    nl = 2 * qw // LANES
    hl = nl // 2

    for b in range(2):
        for k in range(nq):
            u = u_ref[b, :, k * qc:(k + 1) * qc].astype(BF16)
            x = jnp.dot(u, bq_ref[k], preferred_element_type=F32)
            for c in range(nl):
                x_ref[c, pl.ds((b * nq + k) * pitch, tb), :] = x[:, c * LANES:(c + 1) * LANES]

    a = [a_ref[:, c * LANES:(c + 1) * LANES] for c in range(nl)]

    def scan(t, h):
        rows = pl.ds(t, 2 * nq, stride=pitch)
        new = [None] * nl
        for c in range(hl):
            a_re, a_im, h_re, h_im = a[c], a[hl + c], h[c], h[hl + c]
            new[c] = a_re * h_re - a_im * h_im + x_ref[c, rows, :]
            new[hl + c] = a_re * h_im + a_im * h_re + x_ref[hl + c, rows, :]
        for c in range(nl):
            x_ref[c, rows, :] = new[c]
        return tuple(new)

    h0 = tuple(h_ref[:, c * LANES:(c + 1) * LANES] for c in range(nl))
    h_new = jnp.concatenate(lax.fori_loop(0, tb, scan, h0, unroll=8), axis=1)
    h_ref[...] = h_new

    for b in range(2):
        ys = []
        for k in range(nq):
            rows = pl.ds((b * nq + k) * pitch, tb)
            hs = jnp.concatenate([x_ref[c, rows, :] for c in range(nl)], axis=1).astype(BF16)
            ys.append(jnp.dot(hs, cq_ref[k], preferred_element_type=F32))
        y = jnp.concatenate(ys, axis=1) + d_ref[...] * u_ref[b]
        gate = jnp.dot(y.astype(BF16), wg_ref[...], preferred_element_type=F32) + bg_ref[...]
        o_ref[b] = y * jax.nn.sigmoid(gate)

    @pl.when(step == pl.num_programs(0) - 1)
    def _():
        hfin_ref[...] = h_new


def ssm_prompt(z3, bq, a8, cq, d_skip, w_glu, b_glu, tb=512):
    n, seq, _ = z3.shape
    assert n == 2
    pitch = tb + 8
    ucol = 3 * ATT_WIDTH // SSM_WIDTH
    kern = functools.partial(_ssm_prompt_kernel, tb=tb, pitch=pitch)
    full = lambda *shape: pl.BlockSpec(shape, lambda s: (0,) * len(shape))
    return pl.pallas_call(
        kern,
        grid=(seq // tb,),
        in_specs=[pl.BlockSpec((n, tb, SSM_WIDTH), lambda s: (0, s, ucol)),
                  full(SSM_QUARTERS, SSM_QC, 2 * SSM_QW),
                  full(2 * SSM_QUARTERS, 2 * SSM_QW),
                  full(SSM_QUARTERS, 2 * SSM_QW, SSM_QC),
                  full(1, SSM_WIDTH),
                  full(SSM_WIDTH, SSM_WIDTH),
                  full(1, SSM_WIDTH)],
        out_specs=[pl.BlockSpec((n, tb, SSM_WIDTH), lambda s: (0, s, 0)),
                   full(2 * SSM_QUARTERS, 2 * SSM_QW)],
        out_shape=[jax.ShapeDtypeStruct((n, seq, SSM_WIDTH), F32),
                   jax.ShapeDtypeStruct((2 * SSM_QUARTERS, 2 * SSM_QW), F32)],
        scratch_shapes=[pltpu.VMEM((2 * SSM_QW // LANES, 2 * SSM_QUARTERS * pitch, LANES), F32),
                        pltpu.VMEM((2 * SSM_QUARTERS, 2 * SSM_QW), F32)],
        compiler_params=_params("arbitrary"),
        name="ssm_prompt",
    )(z3, bq, a8, cq, d_skip, w_glu, b_glu)


def _ssm_sample_kernel(u_ref, hre_ref, him_ref, bq_ref, a_ref, cq_ref, d_ref, wg_ref, bg_ref,
                       o_ref, ore_ref, oim_ref, x_ref):
    n, t, _ = u_ref.shape
    nq = SSM_QUARTERS
    qw = SSM_QW
    qc = SSM_QC
    nl = 2 * qw // LANES
    hl = nl // 2
    u_all = u_ref[...].reshape(n * t, SSM_WIDTH)
    ys = []
    for k in range(nq):
        u = u_all[:, k * qc:(k + 1) * qc]
        x = jnp.dot(u, bq_ref[k], preferred_element_type=F32, precision=lax.Precision.HIGHEST)
        for c in range(nl):
            x_ref[c] = x[:, c * LANES:(c + 1) * LANES]
        h = ([hre_ref[:, k * qw + c * LANES:k * qw + (c + 1) * LANES] for c in range(hl)]
             + [him_ref[:, k * qw + c * LANES:k * qw + (c + 1) * LANES] for c in range(hl)])
        for s in range(t):
            rows = pl.ds(s, n, stride=t)
            new = [None] * nl
            for c in range(hl):
                a_re = a_ref[k:k + 1, c * LANES:(c + 1) * LANES]
                a_im = a_ref[k:k + 1, (hl + c) * LANES:(hl + c + 1) * LANES]
                new[c] = a_re * h[c] - a_im * h[hl + c] + x_ref[c, rows, :]
                new[hl + c] = a_re * h[hl + c] + a_im * h[c] + x_ref[hl + c, rows, :]
            for c in range(nl):
                x_ref[c, rows, :] = new[c]
            h = new
        for c in range(hl):
            ore_ref[:, k * qw + c * LANES:k * qw + (c + 1) * LANES] = h[c]
            oim_ref[:, k * qw + c * LANES:k * qw + (c + 1) * LANES] = h[hl + c]
        hs = jnp.concatenate([x_ref[c] for c in range(nl)], axis=1).astype(BF16)
        ys.append(jnp.dot(hs, cq_ref[k].astype(BF16), preferred_element_type=F32))
    y = jnp.concatenate(ys, axis=1) + d_ref[...] * u_all
    gate = jnp.dot(y.astype(BF16), wg_ref[...], preferred_element_type=F32) + bg_ref[...]
    o_ref[...] = (y * jax.nn.sigmoid(gate)).reshape(n, t, SSM_WIDTH)


def ssm_sample(zs3, h_re, h_im, bq32, a8, cq32, d_skip, w_glu, b_glu):
    n, t, _ = zs3.shape
    ucol = 3 * ATT_WIDTH // SSM_WIDTH
    ns = SSM_GROUPS * SSM_STATE
    full = lambda *shape: pl.BlockSpec(shape, lambda s: (0,) * len(shape))
    return pl.pallas_call(
        _ssm_sample_kernel,
        grid=(1,),
        in_specs=[pl.BlockSpec((n, t, SSM_WIDTH), lambda s: (0, 0, ucol)),
                  full(n, ns), full(n, ns),
                  full(SSM_QUARTERS, SSM_QC, 2 * SSM_QW),
                  full(2 * SSM_QUARTERS, 2 * SSM_QW),
                  full(SSM_QUARTERS, 2 * SSM_QW, SSM_QC),
                  full(1, SSM_WIDTH),
                  full(SSM_WIDTH, SSM_WIDTH),
                  full(1, SSM_WIDTH)],
        out_specs=[full(n, t, SSM_WIDTH), full(n, ns), full(n, ns)],
        out_shape=[jax.ShapeDtypeStruct((n, t, SSM_WIDTH), F32),
                   jax.ShapeDtypeStruct((n, ns), F32),
                   jax.ShapeDtypeStruct((n, ns), F32)],
        scratch_shapes=[pltpu.VMEM((2 * SSM_QW // LANES, n * t, LANES), F32)],
        compiler_params=_params("arbitrary"),
        name="ssm_sample",
    )(zs3, h_re, h_im, bq32, a8, cq32, d_skip, w_glu, b_glu)


def _sgu_norm_v(v, g):
    gv = jax.nn.gelu(v)
    vc = gv - jnp.mean(gv, axis=-1, keepdims=True)
    return vc * lax.rsqrt(jnp.mean(vc * vc, axis=-1, keepdims=True) + EPS) * g


def _sgu_causal(w_ref):
    c = SGU_CHUNK
    keep = lax.broadcasted_iota(jnp.int32, (c, c), 0) >= lax.broadcasted_iota(jnp.int32, (c, c), 1)
    return [jnp.where(keep, w_ref[h], 0.0).astype(BF16) for h in range(SGU_HEADS)]


def _sgu_prompt_kernel(u_ref, v_ref, g_ref, w_ref, b_ref, o_ref):
    c = SGU_CHUNK
    e = SGU_HEAD_DIM
    rows = u_ref.shape[1]
    w = _sgu_causal(w_ref)
    gu = jax.nn.gelu(u_ref[0])
    gv = _sgu_norm_v(v_ref[0], g_ref[...]).astype(BF16)
    for ci in range(rows // c):
        for h in range(SGU_HEADS):
            mixed = jnp.dot(w[h], gv[ci * c:(ci + 1) * c, h * e:(h + 1) * e],
                            preferred_element_type=F32) + b_ref[h]
            o_ref[0, ci * c:(ci + 1) * c, h * e:(h + 1) * e] = (
                gu[ci * c:(ci + 1) * c, h * e:(h + 1) * e] * mixed)


def sgu_prompt(z3, g, w_sp, b_rows, rows=256):
    n, seq, _ = z3.shape
    ucol = (3 * ATT_WIDTH + SSM_WIDTH) // SGU_WIDTH
    return pl.pallas_call(
        _sgu_prompt_kernel,
        grid=(n, seq // rows),
        in_specs=[pl.BlockSpec((1, rows, SGU_WIDTH), lambda i, j: (i, j, ucol)),
                  pl.BlockSpec((1, rows, SGU_WIDTH), lambda i, j: (i, j, ucol + 1)),
                  pl.BlockSpec((1, SGU_WIDTH), lambda i, j: (0, 0)),
                  pl.BlockSpec((SGU_HEADS, SGU_CHUNK, SGU_CHUNK), lambda i, j: (0, 0, 0)),
                  pl.BlockSpec((SGU_HEADS, SGU_CHUNK, SGU_HEAD_DIM), lambda i, j: (0, 0, 0))],
        out_specs=pl.BlockSpec((1, rows, SGU_WIDTH), lambda i, j: (i, j, 0)),
        out_shape=jax.ShapeDtypeStruct((n, seq, SGU_WIDTH), F32),
        compiler_params=_params("parallel", "parallel"),
        name="sgu_prompt",
    )(z3, z3, g, w_sp, b_rows)


def _sgu_sample_kernel(u_ref, v_ref, g_ref, w_ref, b_ref, o_ref, gv_ref):
    c = SGU_CHUNK
    e = SGU_HEAD_DIM
    t = u_ref.shape[1]
    w = _sgu_causal(w_ref)
    gu = jax.nn.gelu(u_ref[0])
    gv = _sgu_norm_v(v_ref[0], g_ref[...])
    gv_ref[0] = gv
    gvp = jnp.concatenate([gv, jnp.zeros((c - t, SGU_WIDTH), F32)], axis=0).astype(BF16)
    for h in range(SGU_HEADS):
        mixed = jnp.dot(w[h], gvp[:, h * e:(h + 1) * e], preferred_element_type=F32) + b_ref[h]
        o_ref[0, :, h * e:(h + 1) * e] = gu[:, h * e:(h + 1) * e] * mixed[:t]


def sgu_sample(zs3, g, w_sp, b_rows):
    n, t, _ = zs3.shape
    ucol = (3 * ATT_WIDTH + SSM_WIDTH) // SGU_WIDTH
    return pl.pallas_call(
        _sgu_sample_kernel,
        grid=(n,),
        in_specs=[pl.BlockSpec((1, t, SGU_WIDTH), lambda i: (i, 0, ucol)),
                  pl.BlockSpec((1, t, SGU_WIDTH), lambda i: (i, 0, ucol + 1)),
                  pl.BlockSpec((1, SGU_WIDTH), lambda i: (0, 0)),
                  pl.BlockSpec((SGU_HEADS, SGU_CHUNK, SGU_CHUNK), lambda i: (0, 0, 0)),
                  pl.BlockSpec((SGU_HEADS, SGU_CHUNK, SGU_HEAD_DIM), lambda i: (0, 0, 0))],
        out_specs=[pl.BlockSpec((1, t, SGU_WIDTH), lambda i: (i, 0, 0)),
                   pl.BlockSpec((1, t, SGU_WIDTH), lambda i: (i, 0, 0))],
        out_shape=[jax.ShapeDtypeStruct((n, t, SGU_WIDTH), F32),
                   jax.ShapeDtypeStruct((n, t, SGU_WIDTH), F32)],
        compiler_params=_params("parallel"),
        name="sgu_sample",
    )(zs3, zs3, g, w_sp, b_rows)


def _out_proj_kernel(att_ref, ssm_ref, sgu_ref, x_ref, w_ref, gm_ref, gp_ref, o_ref):
    a0 = ATT_WIDTH
    a1 = ATT_WIDTH + SSM_WIDTH
    a = _rms(att_ref[...], gm_ref[:, :a0]).astype(BF16)
    s = _rms(ssm_ref[...], gm_ref[:, a0:a1]).astype(BF16)
    c = _rms(sgu_ref[...], gm_ref[:, a1:]).astype(BF16)
    y = (jnp.dot(a, w_ref[:a0, :], preferred_element_type=F32)
         + jnp.dot(s, w_ref[a0:a1, :], preferred_element_type=F32)
         + jnp.dot(c, w_ref[a1:, :], preferred_element_type=F32))
    o_ref[...] = x_ref[...] + _rms(y, gp_ref[...])


def out_proj(att, ssm, sgu, x, w, gm, gp, tm):
    m, dm = x.shape
    row = lambda width: pl.BlockSpec((tm, width), lambda i: (i, 0))
    return pl.pallas_call(
        _out_proj_kernel,
        grid=(m // tm,),
        in_specs=[row(ATT_WIDTH), row(SSM_WIDTH), row(SGU_WIDTH), row(dm),
                  pl.BlockSpec(w.shape, lambda i: (0, 0)),
                  pl.BlockSpec((1, dm), lambda i: (0, 0)),
                  pl.BlockSpec((1, dm), lambda i: (0, 0))],
        out_specs=row(dm),
        out_shape=jax.ShapeDtypeStruct((m, dm), F32),
        compiler_params=_params("parallel"),
        name="out_proj",
    )(att, ssm, sgu, x, w, gm, gp)


def _ffn_kernel(x_ref, gpre_ref, wg_ref, wu_ref, wd_ref, gpost_ref, o_ref, h_ref):
    j = pl.program_id(1)

    @pl.when(j == 0)
    def _():
        h_ref[...] = _rms(x_ref[...], gpre_ref[...]).astype(BF16)

    h = h_ref[...]
    gate = jnp.dot(h, wg_ref[...], preferred_element_type=F32)
    up = jnp.dot(h, wu_ref[...], preferred_element_type=F32)
    act = (jax.nn.silu(gate) * up).astype(BF16)
    part = jnp.dot(act, wd_ref[...], preferred_element_type=F32)

    @pl.when(j == 0)
    def _():
        o_ref[...] = part

    @pl.when(j > 0)
    def _():
        o_ref[...] += part

    @pl.when(j == pl.num_programs(1) - 1)
    def _():
        o_ref[...] = x_ref[...] + _rms(o_ref[...], gpost_ref[...])


def ffn(x, gpre, wg, wu, wd, gpost, tm, tf):
    m, dm = x.shape
    f = wg.shape[1]
    return pl.pallas_call(
        _ffn_kernel,
        grid=(m // tm, f // tf),
        in_specs=[pl.BlockSpec((tm, dm), lambda i, j: (i, 0)),
                  pl.BlockSpec((1, dm), lambda i, j: (0, 0)),
                  pl.BlockSpec((dm, tf), lambda i, j: (0, j)),
                  pl.BlockSpec((dm, tf), lambda i, j: (0, j)),
                  pl.BlockSpec((tf, dm), lambda i, j: (j, 0)),
                  pl.BlockSpec((1, dm), lambda i, j: (0, 0))],
        out_specs=pl.BlockSpec((tm, dm), lambda i, j: (i, 0)),
        out_shape=jax.ShapeDtypeStruct((m, dm), F32),
        scratch_shapes=[pltpu.VMEM((tm, dm), BF16)],
        compiler_params=_params("parallel", "arbitrary"),
        name="ffn",
    )(x, gpre, wg, wu, wd, gpost)


def _t5_bucket(dist):
    max_exact = N_REL_BUCKETS // 2
    df = jnp.maximum(dist, 1).astype(F32)
    large = max_exact + (jnp.log(df / max_exact) / math.log(REL_MAX_DIST / max_exact)
                         * (N_REL_BUCKETS - max_exact)).astype(jnp.int32)
    large = jnp.minimum(large, N_REL_BUCKETS - 1)
    return jnp.where(dist < max_exact, dist, large)


def _strided_bias(rel_bias, dilation, n_steps):
    dist = jnp.arange(n_steps + 1, dtype=jnp.int32) * dilation
    return rel_bias[_t5_bucket(dist)].astype(F32)


def _prompt_bias_tables(rel_bias):
    qb = ATT_BLOCK
    qi = jnp.arange(qb)[:, None]
    kj = jnp.arange(2 * qb)[None, :]
    tables = []
    for window, d in DILATED_PATTERNS:
        nk = window // d
        sb = _strided_bias(rel_bias, d, nk)
        delta = qb + qi - kj
        ok = (delta >= 0) & (delta <= nk)
        normal = jnp.where(ok[None], jnp.transpose(sb[jnp.clip(delta, 0, nk)], (2, 0, 1)), NEG_INF)
        delta_f = qi - kj
        ok_f = (delta_f >= 0) & (delta_f <= nk)
        first = jnp.where(ok_f[None], jnp.transpose(sb[jnp.clip(delta_f, 0, nk)], (2, 0, 1)), NEG_INF)
        tables.append(jnp.stack([normal, first]))
    return jnp.stack(tables).astype(F32)


def _sample_bias_tables(rel_bias, w, t):
    tq = jnp.arange(t)[:, None]
    cache_delta = w + tq - jnp.arange(w)[None, :]
    new_delta = tq - jnp.arange(ATT_BLOCK)[None, :]
    out_c, out_n = [], []
    for window, d in DILATED_PATTERNS:
        nk = window // d
        sb = _strided_bias(rel_bias, d, nk)

        def table(delta, extra_ok):
            ok = (delta >= 0) & (delta % d == 0) & (delta <= nk * d) & extra_ok
            steps = jnp.clip(delta // d, 0, nk)
            return jnp.where(ok[None], jnp.transpose(sb[steps], (2, 0, 1)), NEG_INF)

        out_c.append(table(cache_delta, True))
        out_n.append(table(new_delta, jnp.arange(ATT_BLOCK)[None, :] < t))
    return jnp.stack(out_c).astype(F32), jnp.stack(out_n).astype(F32)


def _ssm_tables(lam_re, lam_im, log_step, b_re, b_im, c_re, c_im):
    g, p, cg = SSM_GROUPS, SSM_STATE, SSM_GROUP
    nq = SSM_QUARTERS
    gq = g // nq
    step = jnp.exp(log_step)[:, None]
    mag = jnp.exp(lam_re * step)
    a_re = mag * jnp.cos(lam_im * step)
    a_im = mag * jnp.sin(lam_im * step)
    den = lam_re * lam_re + lam_im * lam_im
    coef_re = ((a_re - 1.0) * lam_re + a_im * lam_im) / den
    coef_im = (a_im * lam_re - (a_re - 1.0) * lam_im) / den
    bb_re = coef_re[..., None] * b_re - coef_im[..., None] * b_im
    bb_im = coef_re[..., None] * b_im + coef_im[..., None] * b_re
    eye = jnp.eye(gq, dtype=F32)

    def in_mat(bb):
        bb = bb.reshape(nq, gq, p, cg)
        return jnp.einsum('kgpc,gh->kgchp', bb, eye).reshape(nq, gq * cg, gq * p)

    def out_mat(c):
        c = c.reshape(nq, gq, cg, p)
        return jnp.einsum('kgcp,gh->kgphc', c, eye).reshape(nq, gq * p, gq * cg)

    bq = jnp.concatenate([in_mat(bb_re), in_mat(bb_im)], axis=2)
    cq = jnp.concatenate([out_mat(c_re), -out_mat(c_im)], axis=1)
    aq = jnp.concatenate([a_re.reshape(nq, gq * p), a_im.reshape(nq, gq * p)], axis=1)
    a8 = jnp.concatenate([aq, aq], axis=0)
    return bq, cq, a8


def _row(v):
    return v.reshape(1, -1)


def kernel(x_prompt, x_sample, cache_attn_k, cache_attn_v, state_ssm_re, state_ssm_im, rel_bias, w_in, w_out, g_pre_mix, g_post_mix, g_mix_out, ssm_lam_re, ssm_lam_im, ssm_log_step, ssm_b_re, ssm_b_im, ssm_c_re, ssm_c_im, ssm_d, ssm_w_glu, ssm_b_glu, sgu_g, sgu_w, sgu_b, g_pre_ffn, g_post_ffn, w_gate, w_up, w_down):
    nb, seq, dm = x_prompt.shape
    ns, ts, _ = x_sample.shape
    wbuf = cache_attn_k.shape[2]
    n_keep = min(DILATED_PATTERNS[-1][0], seq)
    h, e = ATT_HEADS, ATT_HEAD_DIM

    bias_p = _prompt_bias_tables(rel_bias)
    bias_c, bias_n = _sample_bias_tables(rel_bias, wbuf, ts)
    cache_k = cache_attn_k.reshape(DEPTH, ns, wbuf, ATT_WIDTH)
    cache_v = cache_attn_v.reshape(DEPTH, ns, wbuf, ATT_WIDTH)
    st_re = state_ssm_re.reshape(DEPTH, ns, SSM_GROUPS * SSM_STATE)
    st_im = state_ssm_im.reshape(DEPTH, ns, SSM_GROUPS * SSM_STATE)

    xp = x_prompt.reshape(nb * seq, dm)
    xs = x_sample.reshape(ns * ts, dm)
    outs = {k: [] for k in ('kp', 'vp', 'rp', 'ip', 'ks', 'vs', 'rs', 'is', 'us')}

    for l in range(DEPTH):
        w_in_l = w_in[l].astype(BF16)
        w_out_l = w_out[l].astype(BF16)
        w_gate_l = w_gate[l].astype(BF16)
        w_up_l = w_up[l].astype(BF16)
        w_down_l = w_down[l].astype(BF16)
        w_glu_l = ssm_w_glu[l].astype(BF16)
        bq, cq, a8 = _ssm_tables(ssm_lam_re[l], ssm_lam_im[l], ssm_log_step[l],
                                 ssm_b_re[l], ssm_b_im[l], ssm_c_re[l], ssm_c_im[l])
        bq16, cq16 = bq.astype(BF16), cq.astype(BF16)
        sgu_b_rows = jnp.broadcast_to(sgu_b[l][:, :, None], (SGU_HEADS, SGU_CHUNK, SGU_HEAD_DIM))
        g_pre, g_post, g_mix = _row(g_pre_mix[l]), _row(g_post_mix[l]), _row(g_mix_out[l])
        d_skip, b_glu, g_sgu = _row(ssm_d[l]), _row(ssm_b_glu[l]), _row(sgu_g[l])

        z = in_proj(xp, g_pre, w_in_l, tm=1024, tn=512).reshape(nb, seq, IN_COLS)
        o_att = attn_prompt(z, bias_p)
        o_ssm, h_fin = ssm_prompt(z, bq16, a8, cq16, d_skip, w_glu_l, b_glu)
        o_sgu = sgu_prompt(z, g_sgu, sgu_w[l], sgu_b_rows)
        xp = out_proj(o_att.reshape(nb * seq, ATT_WIDTH), o_ssm.reshape(nb * seq, SSM_WIDTH),
                      o_sgu.reshape(nb * seq, SGU_WIDTH), xp, w_out_l, g_mix, g_post, tm=512)
        xp = ffn(xp, _row(g_pre_ffn[l]), w_gate_l, w_up_l, w_down_l, _row(g_post_ffn[l]),
                 tm=512, tf=512)
        outs['kp'].append(z[:, seq - n_keep:, ATT_WIDTH:2 * ATT_WIDTH].reshape(nb, n_keep, h, e))
        outs['vp'].append(z[:, seq - n_keep:, 2 * ATT_WIDTH:3 * ATT_WIDTH].reshape(nb, n_keep, h, e))
        hq = h_fin.reshape(nb, SSM_QUARTERS, 2, SSM_GROUPS // SSM_QUARTERS, SSM_STATE)
        outs['rp'].append(hq[:, :, 0].reshape(nb, SSM_GROUPS, SSM_STATE))
        outs['ip'].append(hq[:, :, 1].reshape(nb, SSM_GROUPS, SSM_STATE))

        zs = in_proj(xs, g_pre, w_in_l, tm=ns * ts, tn=512).reshape(ns, ts, IN_COLS)
        s_att = attn_sample(zs, cache_k, cache_v, l, bias_c, bias_n)
        s_ssm, s_re, s_im = ssm_sample(zs, st_re[l], st_im[l], bq, a8, cq, d_skip, w_glu_l, b_glu)
        s_sgu, s_gv = sgu_sample(zs, g_sgu, sgu_w[l], sgu_b_rows)
        xs = out_proj(s_att.reshape(ns * ts, ATT_WIDTH), s_ssm.reshape(ns * ts, SSM_WIDTH),
                      s_sgu.reshape(ns * ts, SGU_WIDTH), xs, w_out_l, g_mix, g_post, tm=ns * ts)
        xs = ffn(xs, _row(g_pre_ffn[l]), w_gate_l, w_up_l, w_down_l, _row(g_post_ffn[l]),
                 tm=ns * ts, tf=512)
        outs['ks'].append(zs[:, :, ATT_WIDTH:2 * ATT_WIDTH].reshape(ns, ts, h, e))
        outs['vs'].append(zs[:, :, 2 * ATT_WIDTH:3 * ATT_WIDTH].reshape(ns, ts, h, e))
        outs['rs'].append(s_re.reshape(ns, SSM_GROUPS, SSM_STATE))
        outs['is'].append(s_im.reshape(ns, SSM_GROUPS, SSM_STATE))
        outs['us'].append(s_gv)

    st = lambda key: jnp.stack(outs[key])
    return (xp.reshape(nb, seq, dm), xs.reshape(ns, ts, dm), st('kp'), st('vp'), st('rp'), st('ip'),
            st('ks'), st('vs'), st('rs'), st('is'), st('us'))
```

```python
import functools
import math

import jax
import jax.numpy as jnp
from jax import lax
from jax.experimental import pallas as pl
from jax.experimental.pallas import tpu as pltpu

F32 = jnp.float32
BF16 = jnp.bfloat16

D_MODEL = 2048
DEPTH = 4
ATT_HEADS = 8
ATT_HEAD_DIM = 128
ATT_WIDTH = ATT_HEADS * ATT_HEAD_DIM
DILATED_PATTERNS = ((128, 1), (512, 4), (2048, 16))
ATT_BLOCK = 128
ATT_GROUP = 4
ATT_SCALE = ATT_HEAD_DIM ** -0.5
N_REL_BUCKETS = 32
REL_MAX_DIST = 2048
SSM_WIDTH = 512
SSM_GROUP = 16
SSM_GROUPS = SSM_WIDTH // SSM_GROUP
SSM_STATE = 64
SSM_QUARTERS = 4
SSM_QW = SSM_GROUPS * SSM_STATE // SSM_QUARTERS
SSM_QC = SSM_WIDTH // SSM_QUARTERS
SGU_WIDTH = 512
SGU_HEADS = 4
SGU_HEAD_DIM = SGU_WIDTH // SGU_HEADS
SGU_CHUNK = 128
IN_COLS = 3 * ATT_WIDTH + SSM_WIDTH + 2 * SGU_WIDTH
FFN_HIDDEN = 5632
EPS = 1e-6
NEG_INF = -1e30
LANES = 128

VMEM_LIMIT_BYTES = 56 * 1024 * 1024


def _params(*sem):
    return pltpu.CompilerParams(dimension_semantics=sem, vmem_limit_bytes=VMEM_LIMIT_BYTES)


def _rms(x, g):
    return x * lax.rsqrt(jnp.mean(x * x, axis=-1, keepdims=True) + EPS) * g


def _in_proj_kernel(x_ref, g_ref, w_ref, *rest, keep_from, tiles_per_seq):
    z_ref, h_ref = rest[-4 if len(rest) > 2 else 0], rest[-1]
    i = pl.program_id(0)
    j = pl.program_id(1)

    @pl.when(j == 0)
    def _():
        h_ref[...] = _rms(x_ref[...], g_ref[...]).astype(BF16)

    res = jnp.dot(h_ref[...], w_ref[...], preferred_element_type=F32)
    z_ref[...] = res

    if len(rest) > 2:
        tm, tn = res.shape
        heads_per_tile = tn // ATT_HEAD_DIM
        keep = (i % tiles_per_seq) >= keep_from
        for buf_ref, col0 in ((rest[3], ATT_WIDTH), (rest[4], 2 * ATT_WIDTH)):
            for jj in range(ATT_WIDTH // tn):
                @pl.when(keep & (j == col0 // tn + jj))
                def _(buf_ref=buf_ref, jj=jj):
                    for hh in range(heads_per_tile):
                        rows = pl.ds(jj * heads_per_tile + hh, tm, stride=ATT_HEADS)
                        buf_ref[0, 0, rows, :] = res[:, hh * ATT_HEAD_DIM:(hh + 1) * ATT_HEAD_DIM]


def in_proj(x, g, w, tm, tn, kv=None):
    m, k = x.shape
    n = w.shape[1]
    in_specs = [pl.BlockSpec((tm, k), lambda i, j: (i, 0)),
                pl.BlockSpec((1, k), lambda i, j: (0, 0)),
                pl.BlockSpec((k, tn), lambda i, j: (0, j))]
    z_spec = pl.BlockSpec((tm, tn), lambda i, j: (i, j))
    z_shape = jax.ShapeDtypeStruct((m, n), F32)
    scratch = [pltpu.VMEM((tm, k), BF16)]
    if kv is None:
        kern = functools.partial(_in_proj_kernel, keep_from=0, tiles_per_seq=1)
        return pl.pallas_call(
            kern, grid=(m // tm, n // tn), in_specs=in_specs, out_specs=z_spec, out_shape=z_shape,
            scratch_shapes=scratch, compiler_params=_params("parallel", "arbitrary"),
            name="in_proj",
        )(x, g, w)
    layer, seq, n_keep, k_buf, v_buf = kv
    tiles_per_seq = seq // tm
    keep_from = (seq - n_keep) // tm
    assert seq % tm == 0 and (seq - n_keep) % tm == 0 and ATT_WIDTH % tn == 0

    def buf_map(i, j):
        return (layer, i // tiles_per_seq, jnp.maximum(i % tiles_per_seq - keep_from, 0), 0)

    buf_spec = pl.BlockSpec((1, 1, tm * ATT_HEADS, ATT_HEAD_DIM), buf_map)
    any_spec = pl.BlockSpec(memory_space=pl.ANY)
    kern = functools.partial(_in_proj_kernel, keep_from=keep_from, tiles_per_seq=tiles_per_seq)
    return pl.pallas_call(
        kern, grid=(m // tm, n // tn),
        in_specs=in_specs + [any_spec, any_spec],
        out_specs=[z_spec, buf_spec, buf_spec],
        out_shape=[z_shape, jax.ShapeDtypeStruct(k_buf.shape, F32),
                   jax.ShapeDtypeStruct(v_buf.shape, F32)],
        input_output_aliases={3: 1, 4: 2},
        scratch_shapes=scratch, compiler_params=_params("arbitrary", "arbitrary"),
        name="in_proj_kv",
    )(x, g, w, k_buf, v_buf)


def _attn_prompt_kernel(q_ref, k_ref, v_ref, b_ref, o_ref, acc_ref, m_ref, l_ref, *, seq):
    qb = ATT_BLOCK
    n_pat = len(DILATED_PATTERNS)

    for p, (_, d) in enumerate(DILATED_PATTERNS):
        shift = int(math.log2(d))
        whole = seq // d == 2 * qb
        units = ATT_GROUP // 2 if whole else ATT_GROUP

        def group(g, carry, p=p, d=d, shift=shift, whole=whole, units=units):
            rows, biases = [], []
            for u in range(units):
                i = g * units + u
                if whole:
                    rows.append((pl.ds(i, 2 * qb, stride=d), pl.ds(i, 2 * qb, stride=d)))
                    biases.append(None)
                    continue
                r = i & (d - 1)
                b = i >> shift
                q_start = r + b * (qb * d)
                k_start = r + jnp.maximum(b - 1, 0) * (qb * d)
                if d == 1:
                    rows.append((pl.ds(pl.multiple_of(q_start, qb), qb),
                                 pl.ds(pl.multiple_of(k_start, qb), 2 * qb)))
                else:
                    rows.append((pl.ds(q_start, qb, stride=d), pl.ds(k_start, 2 * qb, stride=d)))
                biases.append((b == 0).astype(jnp.int32))
            nt = (((1,), (1,)), ((), ()))
            scores = [lax.dot_general(q_ref[0, qr, :].astype(BF16), k_ref[0, kr, :].astype(BF16), nt,
                                      preferred_element_type=F32) for qr, kr in rows]
            probs = []
            for (qr, _), first, s in zip(rows, biases, scores):
                if whole:
                    bias = jnp.concatenate([b_ref[p, 1, 0], b_ref[p, 0, 0]], axis=0)
                else:
                    bias = b_ref[p, first, 0]
                s = s * ATT_SCALE + bias
                m_blk = jnp.max(s, axis=1, keepdims=True)
                pe = jnp.exp(s - m_blk)
                wide = (s.shape[0], ATT_HEAD_DIM)
                m_ref[p, qr, :] = jnp.broadcast_to(m_blk, wide)
                l_ref[p, qr, :] = jnp.broadcast_to(jnp.sum(pe, axis=1, keepdims=True), wide)
                probs.append(pe.astype(BF16))
            for (qr, kr), pe in zip(rows, probs):
                acc_ref[p, qr, :] = jnp.dot(pe, v_ref[0, kr, :].astype(BF16),
                                            preferred_element_type=F32)
            return carry

        lax.fori_loop(0, seq // qb // ATT_GROUP, group, 0)

    rows_per = 2 * qb

    def merge(c, carry):
        rows = pl.ds(pl.multiple_of(c * rows_per, rows_per), rows_per)
        ms = [m_ref[p, rows, :] for p in range(n_pat)]
        m = functools.reduce(jnp.maximum, ms)
        num = den = None
        for p in range(n_pat):
            w = jnp.exp(ms[p] - m)
            num = w * acc_ref[p, rows, :] if num is None else num + w * acc_ref[p, rows, :]
            den = w * l_ref[p, rows, :] if den is None else den + w * l_ref[p, rows, :]
        o_ref[0, rows, :] = num / den
        return carry

    lax.fori_loop(0, seq // rows_per, merge, 0)


def attn_prompt(z3, bias):
    n, seq, _ = z3.shape
    h = ATT_HEADS
    kern = functools.partial(_attn_prompt_kernel, seq=seq)
    return pl.pallas_call(
        kern,
        grid=(n, h),
        in_specs=[pl.BlockSpec((1, seq, ATT_HEAD_DIM), lambda i, j: (i, 0, j)),
                  pl.BlockSpec((1, seq, ATT_HEAD_DIM), lambda i, j: (i, 0, h + j)),
                  pl.BlockSpec((1, seq, ATT_HEAD_DIM), lambda i, j: (i, 0, 2 * h + j)),
                  pl.BlockSpec((3, 2, 1, ATT_BLOCK, 2 * ATT_BLOCK), lambda i, j: (0, 0, j, 0, 0))],
        out_specs=pl.BlockSpec((1, seq, ATT_HEAD_DIM), lambda i, j: (i, 0, j)),
        out_shape=jax.ShapeDtypeStruct((n, seq, ATT_WIDTH), F32),
        scratch_shapes=[pltpu.VMEM((len(DILATED_PATTERNS), seq, ATT_HEAD_DIM), F32)] * 3,
        compiler_params=_params("parallel", "parallel"),
        name="attn_prompt",
    )(z3, z3, z3, bias)


def _attn_sample_kernel(q_ref, kn_ref, vn_ref, kc_ref, vc_ref, bc_ref, bn_ref, o_ref):
    t = q_ref.shape[1]
    pad = jnp.zeros((ATT_BLOCK - t, ATT_HEAD_DIM), F32)
    q = q_ref[0].astype(BF16)
    kn = jnp.concatenate([kn_ref[0], pad], axis=0).astype(BF16)
    vn = jnp.concatenate([vn_ref[0], pad], axis=0).astype(BF16)
    head_rows = pl.ds(pl.program_id(1), kc_ref.shape[2] // ATT_HEADS, stride=ATT_HEADS)
    kc = kc_ref[0, 0, head_rows, :].astype(BF16)
    vc = vc_ref[0, 0, head_rows, :].astype(BF16)
    nt = (((1,), (1,)), ((), ()))
    sc = lax.dot_general(q, kc, nt, preferred_element_type=F32) * ATT_SCALE
    sn = lax.dot_general(q, kn, nt, preferred_element_type=F32) * ATT_SCALE
    n_pat = len(DILATED_PATTERNS)
    scs = [sc + bc_ref[p, 0] for p in range(n_pat)]
    sns = [sn + bn_ref[p, 0] for p in range(n_pat)]
    m = None
    for s in scs + sns:
        mx = jnp.max(s, axis=1, keepdims=True)
        m = mx if m is None else jnp.maximum(m, mx)
    pc = sum(jnp.exp(s - m) for s in scs)
    pn = sum(jnp.exp(s - m) for s in sns)
    l = jnp.sum(pc, axis=1, keepdims=True) + jnp.sum(pn, axis=1, keepdims=True)
    acc = (jnp.dot(pc.astype(BF16), vc, preferred_element_type=F32)
           + jnp.dot(pn.astype(BF16), vn, preferred_element_type=F32))
    o_ref[0] = acc / l


def attn_sample(zs3, cache_k, cache_v, layer, bias_c, bias_n):
    n, t, _ = zs3.shape
    h = ATT_HEADS
    e = ATT_HEAD_DIM
    w = cache_k.shape[2] // h
    return pl.pallas_call(
        _attn_sample_kernel,
        grid=(n, h),
        in_specs=[pl.BlockSpec((1, t, e), lambda i, j: (i, 0, j)),
                  pl.BlockSpec((1, t, e), lambda i, j: (i, 0, h + j)),
                  pl.BlockSpec((1, t, e), lambda i, j: (i, 0, 2 * h + j)),
                  pl.BlockSpec((1, 1, w * h, e), lambda i, j: (layer, i, 0, 0)),
                  pl.BlockSpec((1, 1, w * h, e), lambda i, j: (layer, i, 0, 0)),
                  pl.BlockSpec((3, 1, t, w), lambda i, j: (0, j, 0, 0)),
                  pl.BlockSpec((3, 1, t, ATT_BLOCK), lambda i, j: (0, j, 0, 0))],
        out_specs=pl.BlockSpec((1, t, e), lambda i, j: (i, 0, j)),
        out_shape=jax.ShapeDtypeStruct((n, t, ATT_WIDTH), F32),
        compiler_params=_params("parallel", "parallel"),
        name="attn_sample",
    )(zs3, zs3, zs3, cache_k, cache_v, bias_c, bias_n)


def _ssm_prompt_kernel(u_ref, bq_ref, a_ref, cq_ref, d_ref, wg_ref, bg_ref,
                       o_ref, hfin_ref, x_ref, h_ref, *, tb, pitch):
    step = pl.program_id(0)
    nq = SSM_QUARTERS
    qw = SSM_QW
    qc = SSM_QC

    @pl.when(step == 0)
    def _():
        h_ref[...] = jnp.zeros_like(h_ref)

    nl = 2 * qw // LANES
    hl = nl // 2

    for b in range(2):
        for k in range(nq):
            u = u_ref[b, :, k * qc:(k + 1) * qc].astype(BF16)
            x = jnp.dot(u, bq_ref[k], preferred_element_type=F32)
            for c in range(nl):
                x_ref[c, pl.ds((b * nq + k) * pitch, tb), :] = x[:, c * LANES:(c + 1) * LANES]

    a = [a_ref[:, c * LANES:(c + 1) * LANES] for c in range(nl)]

    def scan(t, h):
        rows = pl.ds(t, 2 * nq, stride=pitch)
        new = [None] * nl
        for c in range(hl):
            a_re, a_im, h_re, h_im = a[c], a[hl + c], h[c], h[hl + c]
            new[c] = a_re * h_re - a_im * h_im + x_ref[c, rows, :]
            new[hl + c] = a_re * h_im + a_im * h_re + x_ref[hl + c, rows, :]
        for c in range(nl):
            x_ref[c, rows, :] = new[c]
        return tuple(new)

    h0 = tuple(h_ref[:, c * LANES:(c + 1) * LANES] for c in range(nl))
    h_new = jnp.concatenate(lax.fori_loop(0, tb, scan, h0, unroll=8), axis=1)
    h_ref[...] = h_new

    for b in range(2):
        ys = []
        for k in range(nq):
            rows = pl.ds((b * nq + k) * pitch, tb)
            hs = jnp.concatenate([x_ref[c, rows, :] for c in range(nl)], axis=1).astype(BF16)
            ys.append(jnp.dot(hs, cq_ref[k], preferred_element_type=F32))
        y = jnp.concatenate(ys, axis=1) + d_ref[...] * u_ref[b]
        gate = jnp.dot(y.astype(BF16), wg_ref[...], preferred_element_type=F32) + bg_ref[...]
        o_ref[b] = y * jax.nn.sigmoid(gate)

    @pl.when(step == pl.num_programs(0) - 1)
    def _():
        hfin_ref[...] = h_new


def ssm_prompt(z3, bq, a8, cq, d_skip, w_glu, b_glu, tb=512):
    n, seq, _ = z3.shape
    assert n == 2
    pitch = tb + 8
    ucol = 3 * ATT_WIDTH // SSM_WIDTH
    kern = functools.partial(_ssm_prompt_kernel, tb=tb, pitch=pitch)
    full = lambda *shape: pl.BlockSpec(shape, lambda s: (0,) * len(shape))
    return pl.pallas_call(
        kern,
        grid=(seq // tb,),
        in_specs=[pl.BlockSpec((n, tb, SSM_WIDTH), lambda s: (0, s, ucol)),
                  full(SSM_QUARTERS, SSM_QC, 2 * SSM_QW),
                  full(2 * SSM_QUARTERS, 2 * SSM_QW),
                  full(SSM_QUARTERS, 2 * SSM_QW, SSM_QC),
                  full(1, SSM_WIDTH),
                  full(SSM_WIDTH, SSM_WIDTH),
                  full(1, SSM_WIDTH)],
        out_specs=[pl.BlockSpec((n, tb, SSM_WIDTH), lambda s: (0, s, 0)),
                   full(2 * SSM_QUARTERS, 2 * SSM_QW)],
        out_shape=[jax.ShapeDtypeStruct((n, seq, SSM_WIDTH), F32),
                   jax.ShapeDtypeStruct((2 * SSM_QUARTERS, 2 * SSM_QW), F32)],
        scratch_shapes=[pltpu.VMEM((2 * SSM_QW // LANES, 2 * SSM_QUARTERS * pitch, LANES), F32),
                        pltpu.VMEM((2 * SSM_QUARTERS, 2 * SSM_QW), F32)],
        compiler_params=_params("arbitrary"),
        name="ssm_prompt",
    )(z3, bq, a8, cq, d_skip, w_glu, b_glu)


def _ssm_sample_kernel(u_ref, hre_ref, him_ref, bq_ref, a_ref, cq_ref, d_ref, wg_ref, bg_ref,
                       o_ref, ore_ref, oim_ref, x_ref):
    n, t, _ = u_ref.shape
    nq = SSM_QUARTERS
    qw = SSM_QW
    qc = SSM_QC
    nl = 2 * qw // LANES
    hl = nl // 2
    u_all = u_ref[...].reshape(n * t, SSM_WIDTH)
    ys = []
    for k in range(nq):
        u = u_all[:, k * qc:(k + 1) * qc]
        x = jnp.dot(u, bq_ref[k], preferred_element_type=F32, precision=lax.Precision.HIGHEST)
        for c in range(nl):
            x_ref[c] = x[:, c * LANES:(c + 1) * LANES]
        h = ([hre_ref[:, k * qw + c * LANES:k * qw + (c + 1) * LANES] for c in range(hl)]
             + [him_ref[:, k * qw + c * LANES:k * qw + (c + 1) * LANES] for c in range(hl)])
        for s in range(t):
            rows = pl.ds(s, n, stride=t)
            new = [None] * nl
            for c in range(hl):
                a_re = a_ref[k:k + 1, c * LANES:(c + 1) * LANES]
                a_im = a_ref[k:k + 1, (hl + c) * LANES:(hl + c + 1) * LANES]
                new[c] = a_re * h[c] - a_im * h[hl + c] + x_ref[c, rows, :]
                new[hl + c] = a_re * h[hl + c] + a_im * h[c] + x_ref[hl + c, rows, :]
            for c in range(nl):
                x_ref[c, rows, :] = new[c]
            h = new
        for c in range(hl):
            ore_ref[:, k * qw + c * LANES:k * qw + (c + 1) * LANES] = h[c]
            oim_ref[:, k * qw + c * LANES:k * qw + (c + 1) * LANES] = h[hl + c]
        hs = jnp.concatenate([x_ref[c] for c in range(nl)], axis=1).astype(BF16)
        ys.append(jnp.dot(hs, cq_ref[k].astype(BF16), preferred_element_type=F32))
    y = jnp.concatenate(ys, axis=1) + d_ref[...] * u_all
    gate = jnp.dot(y.astype(BF16), wg_ref[...], preferred_element_type=F32) + bg_ref[...]
    o_ref[...] = (y * jax.nn.sigmoid(gate)).reshape(n, t, SSM_WIDTH)


def ssm_sample(zs3, h_re, h_im, bq32, a8, cq32, d_skip, w_glu, b_glu):
    n, t, _ = zs3.shape
    ucol = 3 * ATT_WIDTH // SSM_WIDTH
    ns = SSM_GROUPS * SSM_STATE
    full = lambda *shape: pl.BlockSpec(shape, lambda s: (0,) * len(shape))
    return pl.pallas_call(
        _ssm_sample_kernel,
        grid=(1,),
        in_specs=[pl.BlockSpec((n, t, SSM_WIDTH), lambda s: (0, 0, ucol)),
                  full(n, ns), full(n, ns),
                  full(SSM_QUARTERS, SSM_QC, 2 * SSM_QW),
                  full(2 * SSM_QUARTERS, 2 * SSM_QW),
                  full(SSM_QUARTERS, 2 * SSM_QW, SSM_QC),
                  full(1, SSM_WIDTH),
                  full(SSM_WIDTH, SSM_WIDTH),
                  full(1, SSM_WIDTH)],
        out_specs=[full(n, t, SSM_WIDTH), full(n, ns), full(n, ns)],
        out_shape=[jax.ShapeDtypeStruct((n, t, SSM_WIDTH), F32),
                   jax.ShapeDtypeStruct((n, ns), F32),
                   jax.ShapeDtypeStruct((n, ns), F32)],
        scratch_shapes=[pltpu.VMEM((2 * SSM_QW // LANES, n * t, LANES), F32)],
        compiler_params=_params("arbitrary"),
        name="ssm_sample",
    )(zs3, h_re, h_im, bq32, a8, cq32, d_skip, w_glu, b_glu)


def _sgu_norm_v(v, g):
    gv = jax.nn.gelu(v)
    vc = gv - jnp.mean(gv, axis=-1, keepdims=True)
    return vc * lax.rsqrt(jnp.mean(vc * vc, axis=-1, keepdims=True) + EPS) * g


def _sgu_causal(w_ref):
    c = SGU_CHUNK
    keep = lax.broadcasted_iota(jnp.int32, (c, c), 0) >= lax.broadcasted_iota(jnp.int32, (c, c), 1)
    return [jnp.where(keep, w_ref[h], 0.0).astype(BF16) for h in range(SGU_HEADS)]


def _sgu_prompt_kernel(u_ref, v_ref, g_ref, w_ref, b_ref, o_ref):
    c = SGU_CHUNK
    e = SGU_HEAD_DIM
    rows = u_ref.shape[1]
    w = _sgu_causal(w_ref)
    gu = jax.nn.gelu(u_ref[0])
    gv = _sgu_norm_v(v_ref[0], g_ref[...]).astype(BF16)
    for ci in range(rows // c):
        for h in range(SGU_HEADS):
            mixed = jnp.dot(w[h], gv[ci * c:(ci + 1) * c, h * e:(h + 1) * e],
                            preferred_element_type=F32) + b_ref[h]
            o_ref[0, ci * c:(ci + 1) * c, h * e:(h + 1) * e] = (
                gu[ci * c:(ci + 1) * c, h * e:(h + 1) * e] * mixed)


def sgu_prompt(z3, g, w_sp, b_rows, rows=256):
    n, seq, _ = z3.shape
    ucol = (3 * ATT_WIDTH + SSM_WIDTH) // SGU_WIDTH
    return pl.pallas_call(
        _sgu_prompt_kernel,
        grid=(n, seq // rows),
        in_specs=[pl.BlockSpec((1, rows, SGU_WIDTH), lambda i, j: (i, j, ucol)),
                  pl.BlockSpec((1, rows, SGU_WIDTH), lambda i, j: (i, j, ucol + 1)),
                  pl.BlockSpec((1, SGU_WIDTH), lambda i, j: (0, 0)),
                  pl.BlockSpec((SGU_HEADS, SGU_CHUNK, SGU_CHUNK), lambda i, j: (0, 0, 0)),
                  pl.BlockSpec((SGU_HEADS, SGU_CHUNK, SGU_HEAD_DIM), lambda i, j: (0, 0, 0))],
        out_specs=pl.BlockSpec((1, rows, SGU_WIDTH), lambda i, j: (i, j, 0)),
        out_shape=jax.ShapeDtypeStruct((n, seq, SGU_WIDTH), F32),
        compiler_params=_params("parallel", "parallel"),
        name="sgu_prompt",
    )(z3, z3, g, w_sp, b_rows)


def _sgu_sample_kernel(u_ref, v_ref, g_ref, w_ref, b_ref, o_ref, gv_ref):
    c = SGU_CHUNK
    e = SGU_HEAD_DIM
    t = u_ref.shape[1]
    w = _sgu_causal(w_ref)
    gu = jax.nn.gelu(u_ref[0])
    gv = _sgu_norm_v(v_ref[0], g_ref[...])
    gv_ref[0] = gv
    gvp = jnp.concatenate([gv, jnp.zeros((c - t, SGU_WIDTH), F32)], axis=0).astype(BF16)
    for h in range(SGU_HEADS):
        mixed = jnp.dot(w[h], gvp[:, h * e:(h + 1) * e], preferred_element_type=F32) + b_ref[h]
        o_ref[0, :, h * e:(h + 1) * e] = gu[:, h * e:(h + 1) * e] * mixed[:t]


def sgu_sample(zs3, g, w_sp, b_rows):
    n, t, _ = zs3.shape
    ucol = (3 * ATT_WIDTH + SSM_WIDTH) // SGU_WIDTH
    return pl.pallas_call(
        _sgu_sample_kernel,
        grid=(n,),
        in_specs=[pl.BlockSpec((1, t, SGU_WIDTH), lambda i: (i, 0, ucol)),
                  pl.BlockSpec((1, t, SGU_WIDTH), lambda i: (i, 0, ucol + 1)),
                  pl.BlockSpec((1, SGU_WIDTH), lambda i: (0, 0)),
                  pl.BlockSpec((SGU_HEADS, SGU_CHUNK, SGU_CHUNK), lambda i: (0, 0, 0)),
                  pl.BlockSpec((SGU_HEADS, SGU_CHUNK, SGU_HEAD_DIM), lambda i: (0, 0, 0))],
        out_specs=[pl.BlockSpec((1, t, SGU_WIDTH), lambda i: (i, 0, 0)),
                   pl.BlockSpec((1, t, SGU_WIDTH), lambda i: (i, 0, 0))],
        out_shape=[jax.ShapeDtypeStruct((n, t, SGU_WIDTH), F32),
                   jax.ShapeDtypeStruct((n, t, SGU_WIDTH), F32)],
        compiler_params=_params("parallel"),
        name="sgu_sample",
    )(zs3, zs3, g, w_sp, b_rows)


def _out_proj_kernel(att_ref, ssm_ref, sgu_ref, x_ref, w_ref, gm_ref, gp_ref, o_ref):
    a0 = ATT_WIDTH
    a1 = ATT_WIDTH + SSM_WIDTH
    a = _rms(att_ref[...], gm_ref[:, :a0]).astype(BF16)
    s = _rms(ssm_ref[...], gm_ref[:, a0:a1]).astype(BF16)
    c = _rms(sgu_ref[...], gm_ref[:, a1:]).astype(BF16)
    y = (jnp.dot(a, w_ref[:a0, :], preferred_element_type=F32)
         + jnp.dot(s, w_ref[a0:a1, :], preferred_element_type=F32)
         + jnp.dot(c, w_ref[a1:, :], preferred_element_type=F32))
    o_ref[...] = x_ref[...] + _rms(y, gp_ref[...])


def out_proj(att, ssm, sgu, x, w, gm, gp, tm):
    m, dm = x.shape
    row = lambda width: pl.BlockSpec((tm, width), lambda i: (i, 0))
    return pl.pallas_call(
        _out_proj_kernel,
        grid=(m // tm,),
        in_specs=[row(ATT_WIDTH), row(SSM_WIDTH), row(SGU_WIDTH), row(dm),
                  pl.BlockSpec(w.shape, lambda i: (0, 0)),
                  pl.BlockSpec((1, dm), lambda i: (0, 0)),
                  pl.BlockSpec((1, dm), lambda i: (0, 0))],
        out_specs=row(dm),
        out_shape=jax.ShapeDtypeStruct((m, dm), F32),
        compiler_params=_params("parallel"),
        name="out_proj",
    )(att, ssm, sgu, x, w, gm, gp)


def _ffn_kernel(x_ref, gpre_ref, wg_ref, wu_ref, wd_ref, gpost_ref, o_ref, h_ref):
    j = pl.program_id(1)

    @pl.when(j == 0)
    def _():
        h_ref[...] = _rms(x_ref[...], gpre_ref[...]).astype(BF16)
        o_ref[...] = jnp.zeros_like(o_ref)

    h = h_ref[...]
    gate = jnp.dot(h, wg_ref[...], preferred_element_type=F32)
    up = jnp.dot(h, wu_ref[...], preferred_element_type=F32)
    act = (jax.nn.silu(gate) * up).astype(BF16)
    o_ref[...] += jnp.dot(act, wd_ref[...], preferred_element_type=F32)

    @pl.when(j == pl.num_programs(1) - 1)
    def _():
        o_ref[...] = x_ref[...] + _rms(o_ref[...], gpost_ref[...])


def ffn(x, gpre, wg, wu, wd, gpost, tm, tf):
    m, dm = x.shape
    f = wg.shape[1]
    return pl.pallas_call(
        _ffn_kernel,
        grid=(m // tm, f // tf),
        in_specs=[pl.BlockSpec((tm, dm), lambda i, j: (i, 0)),
                  pl.BlockSpec((1, dm), lambda i, j: (0, 0)),
                  pl.BlockSpec((dm, tf), lambda i, j: (0, j)),
                  pl.BlockSpec((dm, tf), lambda i, j: (0, j)),
                  pl.BlockSpec((tf, dm), lambda i, j: (j, 0)),
                  pl.BlockSpec((1, dm), lambda i, j: (0, 0))],
        out_specs=pl.BlockSpec((tm, dm), lambda i, j: (i, 0)),
        out_shape=jax.ShapeDtypeStruct((m, dm), F32),
        scratch_shapes=[pltpu.VMEM((tm, dm), BF16)],
        compiler_params=_params("parallel", "arbitrary"),
        name="ffn",
    )(x, gpre, wg, wu, wd, gpost)


def _t5_bucket(dist):
    max_exact = N_REL_BUCKETS // 2
    df = jnp.maximum(dist, 1).astype(F32)
    large = max_exact + (jnp.log(df / max_exact) / math.log(REL_MAX_DIST / max_exact)
                         * (N_REL_BUCKETS - max_exact)).astype(jnp.int32)
    large = jnp.minimum(large, N_REL_BUCKETS - 1)
    return jnp.where(dist < max_exact, dist, large)


def _strided_bias(rel_bias, dilation, n_steps):
    dist = jnp.arange(n_steps + 1, dtype=jnp.int32) * dilation
    return rel_bias[_t5_bucket(dist)].astype(F32)


def _bias_lookup(sb, steps, ok):
    onehot = (steps[..., None] == jnp.arange(sb.shape[0])).astype(F32)
    vals = jnp.einsum('...j,jh->h...', onehot, sb, precision=lax.Precision.HIGHEST)
    return jnp.where(ok[None], vals, NEG_INF)


def _prompt_bias_tables(rel_bias):
    qb = ATT_BLOCK
    qi = jnp.arange(qb)[:, None]
    kj = jnp.arange(2 * qb)[None, :]
    tables = []
    for window, d in DILATED_PATTERNS:
        nk = window // d
        sb = _strided_bias(rel_bias, d, nk)
        delta = qb + qi - kj
        normal = _bias_lookup(sb, jnp.clip(delta, 0, nk), (delta >= 0) & (delta <= nk))
        delta_f = qi - kj
        first = _bias_lookup(sb, jnp.clip(delta_f, 0, nk), (delta_f >= 0) & (delta_f <= nk))
        tables.append(jnp.stack([normal, first]))
    return jnp.stack(tables).astype(F32)


def _sample_bias_tables(rel_bias, w, t):
    tq = jnp.arange(t)[:, None]
    cache_delta = w + tq - jnp.arange(w)[None, :]
    new_delta = tq - jnp.arange(ATT_BLOCK)[None, :]
    out_c, out_n = [], []
    for window, d in DILATED_PATTERNS:
        nk = window // d
        sb = _strided_bias(rel_bias, d, nk)

        def table(delta, extra_ok):
            ok = (delta >= 0) & (delta % d == 0) & (delta <= nk * d) & extra_ok
            return _bias_lookup(sb, jnp.clip(delta // d, 0, nk), ok)

        out_c.append(table(cache_delta, True))
        out_n.append(table(new_delta, jnp.arange(ATT_BLOCK)[None, :] < t))
    return jnp.stack(out_c).astype(F32), jnp.stack(out_n).astype(F32)


def _ssm_tables(lam_re, lam_im, log_step, b_re, b_im, c_re, c_im):
    g, p, cg = SSM_GROUPS, SSM_STATE, SSM_GROUP
    nq = SSM_QUARTERS
    gq = g // nq
    step = jnp.exp(log_step)[:, None]
    mag = jnp.exp(lam_re * step)
    a_re = mag * jnp.cos(lam_im * step)
    a_im = mag * jnp.sin(lam_im * step)
    den = lam_re * lam_re + lam_im * lam_im
    coef_re = ((a_re - 1.0) * lam_re + a_im * lam_im) / den
    coef_im = (a_im * lam_re - (a_re - 1.0) * lam_im) / den
    bb_re = coef_re[..., None] * b_re - coef_im[..., None] * b_im
    bb_im = coef_re[..., None] * b_im + coef_im[..., None] * b_re
    eye = jnp.eye(gq, dtype=F32)

    def in_mat(bb):
        bb = bb.reshape(nq, gq, p, cg)
        return jnp.einsum('kgpc,gh->kgchp', bb, eye).reshape(nq, gq * cg, gq * p)

    def out_mat(c):
        c = c.reshape(nq, gq, cg, p)
        return jnp.einsum('kgcp,gh->kgphc', c, eye).reshape(nq, gq * p, gq * cg)

    bq = jnp.concatenate([in_mat(bb_re), in_mat(bb_im)], axis=2)
    cq = jnp.concatenate([out_mat(c_re), -out_mat(c_im)], axis=1)
    aq = jnp.concatenate([a_re.reshape(nq, gq * p), a_im.reshape(nq, gq * p)], axis=1)
    a8 = jnp.concatenate([aq, aq], axis=0)
    return bq, cq, a8


def _row(v):
    return v.reshape(1, -1)


def kernel(x_prompt, x_sample, cache_attn_k, cache_attn_v, state_ssm_re, state_ssm_im, rel_bias, w_in, w_out, g_pre_mix, g_post_mix, g_mix_out, ssm_lam_re, ssm_lam_im, ssm_log_step, ssm_b_re, ssm_b_im, ssm_c_re, ssm_c_im, ssm_d, ssm_w_glu, ssm_b_glu, sgu_g, sgu_w, sgu_b, g_pre_ffn, g_post_ffn, w_gate, w_up, w_down):
    nb, seq, dm = x_prompt.shape
    ns, ts, _ = x_sample.shape
    wbuf = cache_attn_k.shape[2]
    n_keep = min(DILATED_PATTERNS[-1][0], seq)
    h, e = ATT_HEADS, ATT_HEAD_DIM

    bias_p = _prompt_bias_tables(rel_bias)
    bias_c, bias_n = _sample_bias_tables(rel_bias, wbuf, ts)
    cache_k = cache_attn_k.reshape(DEPTH, ns, wbuf * h, e)
    cache_v = cache_attn_v.reshape(DEPTH, ns, wbuf * h, e)
    k_buf = jnp.zeros((DEPTH, nb, n_keep * h, e), F32)
    v_buf = jnp.zeros((DEPTH, nb, n_keep * h, e), F32)
    st_re = state_ssm_re.reshape(DEPTH, ns, SSM_GROUPS * SSM_STATE)
    st_im = state_ssm_im.reshape(DEPTH, ns, SSM_GROUPS * SSM_STATE)

    xp = x_prompt.reshape(nb * seq, dm)
    xs = x_sample.reshape(ns * ts, dm)
    outs = {k: [] for k in ('rp', 'ip', 'ks', 'vs', 'rs', 'is', 'us')}

    for l in range(DEPTH):
        w_in_l = w_in[l].astype(BF16)
        w_out_l = w_out[l].astype(BF16)
        w_gate_l = w_gate[l].astype(BF16)
        w_up_l = w_up[l].astype(BF16)
        w_down_l = w_down[l].astype(BF16)
        w_glu_l = ssm_w_glu[l].astype(BF16)
        bq, cq, a8 = _ssm_tables(ssm_lam_re[l], ssm_lam_im[l], ssm_log_step[l],
                                 ssm_b_re[l], ssm_b_im[l], ssm_c_re[l], ssm_c_im[l])
        bq16, cq16 = bq.astype(BF16), cq.astype(BF16)
        sgu_b_rows = jnp.broadcast_to(sgu_b[l][:, :, None], (SGU_HEADS, SGU_CHUNK, SGU_HEAD_DIM))
        g_pre, g_post, g_mix = _row(g_pre_mix[l]), _row(g_post_mix[l]), _row(g_mix_out[l])
        d_skip, b_glu, g_sgu = _row(ssm_d[l]), _row(ssm_b_glu[l]), _row(sgu_g[l])

        z, k_buf, v_buf = in_proj(xp, g_pre, w_in_l, tm=1024, tn=512,
                                  kv=(l, seq, n_keep, k_buf, v_buf))
        z = z.reshape(nb, seq, IN_COLS)
        o_att = attn_prompt(z, bias_p)
        o_ssm, h_fin = ssm_prompt(z, bq16, a8, cq16, d_skip, w_glu_l, b_glu)
        o_sgu = sgu_prompt(z, g_sgu, sgu_w[l], sgu_b_rows)
        xp = out_proj(o_att.reshape(nb * seq, ATT_WIDTH), o_ssm.reshape(nb * seq, SSM_WIDTH),
                      o_sgu.reshape(nb * seq, SGU_WIDTH), xp, w_out_l, g_mix, g_post, tm=512)
        xp = ffn(xp, _row(g_pre_ffn[l]), w_gate_l, w_up_l, w_down_l, _row(g_post_ffn[l]),
                 tm=512, tf=512)
        hq = h_fin.reshape(nb, SSM_QUARTERS, 2, SSM_GROUPS // SSM_QUARTERS, SSM_STATE)
        outs['rp'].append(hq[:, :, 0].reshape(nb, SSM_GROUPS, SSM_STATE))
        outs['ip'].append(hq[:, :, 1].reshape(nb, SSM_GROUPS, SSM_STATE))

        zs = in_proj(xs, g_pre, w_in_l, tm=ns * ts, tn=512).reshape(ns, ts, IN_COLS)
        s_att = attn_sample(zs, cache_k, cache_v, l, bias_c, bias_n)
        s_ssm, s_re, s_im = ssm_sample(zs, st_re[l], st_im[l], bq, a8, cq, d_skip, w_glu_l, b_glu)
        s_sgu, s_gv = sgu_sample(zs, g_sgu, sgu_w[l], sgu_b_rows)
        xs = out_proj(s_att.reshape(ns * ts, ATT_WIDTH), s_ssm.reshape(ns * ts, SSM_WIDTH),
                      s_sgu.reshape(ns * ts, SGU_WIDTH), xs, w_out_l, g_mix, g_post, tm=ns * ts)
        xs = ffn(xs, _row(g_pre_ffn[l]), w_gate_l, w_up_l, w_down_l, _row(g_post_ffn[l]),
                 tm=ns * ts, tf=512)
        outs['ks'].append(zs[:, :, ATT_WIDTH:2 * ATT_WIDTH].reshape(ns, ts, h, e))
        outs['vs'].append(zs[:, :, 2 * ATT_WIDTH:3 * ATT_WIDTH].reshape(ns, ts, h, e))
        outs['rs'].append(s_re.reshape(ns, SSM_GROUPS, SSM_STATE))
        outs['is'].append(s_im.reshape(ns, SSM_GROUPS, SSM_STATE))
        outs['us'].append(s_gv)

    st = lambda key: jnp.stack(outs[key])
    kv_shape = (DEPTH, nb, n_keep, h, e)
    return (xp.reshape(nb, seq, dm), xs.reshape(ns, ts, dm),
            k_buf.reshape(kv_shape), v_buf.reshape(kv_shape), st('rp'), st('ip'),
            st('ks'), st('vs'), st('rs'), st('is'), st('us'))
```

```python
import functools
import math

import jax
import jax.numpy as jnp
from jax import lax
from jax.experimental import pallas as pl
from jax.experimental.pallas import tpu as pltpu

F32 = jnp.float32
BF16 = jnp.bfloat16

D_MODEL = 2048
DEPTH = 4
ATT_HEADS = 8
ATT_HEAD_DIM = 128
ATT_WIDTH = ATT_HEADS * ATT_HEAD_DIM
DILATED_PATTERNS = ((128, 1), (512, 4), (2048, 16))
ATT_BLOCK = 128
ATT_GROUP = 4
ATT_SCALE = ATT_HEAD_DIM ** -0.5
N_REL_BUCKETS = 32
REL_MAX_DIST = 2048
SSM_WIDTH = 512
SSM_GROUP = 16
SSM_GROUPS = SSM_WIDTH // SSM_GROUP
SSM_STATE = 64
SSM_QUARTERS = 4
SSM_QW = SSM_GROUPS * SSM_STATE // SSM_QUARTERS
SSM_QC = SSM_WIDTH // SSM_QUARTERS
SGU_WIDTH = 512
SGU_HEADS = 4
SGU_HEAD_DIM = SGU_WIDTH // SGU_HEADS
SGU_CHUNK = 128
IN_COLS = 3 * ATT_WIDTH + SSM_WIDTH + 2 * SGU_WIDTH
FFN_HIDDEN = 5632
EPS = 1e-6
NEG_INF = -1e30
LANES = 128

VMEM_LIMIT_BYTES = 56 * 1024 * 1024


def _params(*sem):
    return pltpu.CompilerParams(dimension_semantics=sem, vmem_limit_bytes=VMEM_LIMIT_BYTES)


def _rms(x, g):
    return x * lax.rsqrt(jnp.mean(x * x, axis=-1, keepdims=True) + EPS) * g


def _in_proj_kernel(x_ref, g_ref, w_ref, *rest, keep_from, tiles_per_seq):
    h_ref = rest[-1]
    outs = rest[-4:-1] if len(rest) > 2 else rest[:1]
    z_ref = outs[0]
    i = pl.program_id(0)
    j = pl.program_id(1)

    @pl.when(j == 0)
    def _():
        h_ref[...] = _rms(x_ref[...], g_ref[...]).astype(BF16)

    res = jnp.dot(h_ref[...], w_ref[...], preferred_element_type=F32)
    z_ref[...] = res

    if len(rest) > 2:
        tm, tn = res.shape
        keep = (i % tiles_per_seq) >= keep_from
        by_tile = {}
        for buf_ref, col0 in ((outs[1], ATT_WIDTH), (outs[2], 2 * ATT_WIDTH)):
            for head in range(ATT_HEADS):
                col = col0 + head * ATT_HEAD_DIM
                by_tile.setdefault(col // tn, []).append((buf_ref, head, col % tn))
        for tile, heads in by_tile.items():
            @pl.when(keep & (j == tile))
            def _(heads=heads):
                for buf_ref, head, off in heads:
                    rows = pl.ds(head, tm, stride=ATT_HEADS)
                    buf_ref[0, 0, rows, :] = res[:, off:off + ATT_HEAD_DIM]


def in_proj(x, g, w, layer, tm, tn, kv=None):
    m, k = x.shape
    n = w.shape[2]
    assert tn % ATT_HEAD_DIM == 0
    in_specs = [pl.BlockSpec((tm, k), lambda i, j: (i, 0)),
                pl.BlockSpec((1, k), lambda i, j: (0, 0)),
                pl.BlockSpec((None, k, tn), lambda i, j: (layer, 0, j))]
    z_spec = pl.BlockSpec((tm, tn), lambda i, j: (i, j))
    z_shape = jax.ShapeDtypeStruct((m, n), F32)
    scratch = [pltpu.VMEM((tm, k), BF16)]
    if kv is None:
        kern = functools.partial(_in_proj_kernel, keep_from=0, tiles_per_seq=1)
        return pl.pallas_call(
            kern, grid=(m // tm, n // tn), in_specs=in_specs, out_specs=z_spec, out_shape=z_shape,
            scratch_shapes=scratch, compiler_params=_params("parallel", "arbitrary"),
            name="in_proj",
        )(x, g, w)
    seq, n_keep, k_buf, v_buf = kv
    tiles_per_seq = seq // tm
    keep_from = (seq - n_keep) // tm
    assert seq % tm == 0 and (seq - n_keep) % tm == 0

    def buf_map(i, j):
        return (layer, i // tiles_per_seq, jnp.maximum(i % tiles_per_seq - keep_from, 0), 0)

    buf_spec = pl.BlockSpec((1, 1, tm * ATT_HEADS, ATT_HEAD_DIM), buf_map)
    any_spec = pl.BlockSpec(memory_space=pl.ANY)
    kern = functools.partial(_in_proj_kernel, keep_from=keep_from, tiles_per_seq=tiles_per_seq)
    return pl.pallas_call(
        kern, grid=(m // tm, n // tn),
        in_specs=in_specs + [any_spec, any_spec],
        out_specs=[z_spec, buf_spec, buf_spec],
        out_shape=[z_shape, jax.ShapeDtypeStruct(k_buf.shape, F32),
                   jax.ShapeDtypeStruct(v_buf.shape, F32)],
        input_output_aliases={3: 1, 4: 2},
        scratch_shapes=scratch, compiler_params=_params("arbitrary", "arbitrary"),
        name="in_proj_kv",
    )(x, g, w, k_buf, v_buf)


def _attn_prompt_kernel(q_ref, k_ref, v_ref, b_ref, o_ref, acc_ref, m_ref, l_ref, *, seq):
    qb = ATT_BLOCK
    n_pat = len(DILATED_PATTERNS)

    for p, (_, d) in enumerate(DILATED_PATTERNS):
        shift = int(math.log2(d))
        whole = seq // d == 2 * qb
        units = ATT_GROUP // 2 if whole else ATT_GROUP

        def group(g, carry, p=p, d=d, shift=shift, whole=whole, units=units):
            rows, biases = [], []
            for u in range(units):
                i = g * units + u
                if whole:
                    rows.append((pl.ds(i, 2 * qb, stride=d), pl.ds(i, 2 * qb, stride=d)))
                    biases.append(None)
                    continue
                r = i & (d - 1)
                b = i >> shift
                q_start = r + b * (qb * d)
                k_start = r + jnp.maximum(b - 1, 0) * (qb * d)
                if d == 1:
                    rows.append((pl.ds(pl.multiple_of(q_start, qb), qb),
                                 pl.ds(pl.multiple_of(k_start, qb), 2 * qb)))
                else:
                    rows.append((pl.ds(q_start, qb, stride=d), pl.ds(k_start, 2 * qb, stride=d)))
                biases.append((b == 0).astype(jnp.int32))
            nt = (((1,), (1,)), ((), ()))
            scores = [lax.dot_general(q_ref[0, qr, :].astype(BF16), k_ref[0, kr, :].astype(BF16), nt,
                                      preferred_element_type=F32) for qr, kr in rows]
            probs = []
            for (qr, _), first, s in zip(rows, biases, scores):
                if whole:
                    bias = jnp.concatenate([b_ref[p, 1, 0], b_ref[p, 0, 0]], axis=0)
                else:
                    bias = b_ref[p, first, 0]
                s = s * ATT_SCALE + bias
                m_blk = jnp.max(s, axis=1, keepdims=True)
                pe = jnp.exp(s - m_blk)
                wide = (s.shape[0], ATT_HEAD_DIM)
                m_ref[p, qr, :] = jnp.broadcast_to(m_blk, wide)
                l_ref[p, qr, :] = jnp.broadcast_to(jnp.sum(pe, axis=1, keepdims=True), wide)
                probs.append(pe.astype(BF16))
            for (qr, kr), pe in zip(rows, probs):
                acc_ref[p, qr, :] = jnp.dot(pe, v_ref[0, kr, :].astype(BF16),
                                            preferred_element_type=F32)
            return carry

        lax.fori_loop(0, seq // qb // ATT_GROUP, group, 0)

    rows_per = 2 * qb

    def merge(c, carry):
        rows = pl.ds(pl.multiple_of(c * rows_per, rows_per), rows_per)
        ms = [m_ref[p, rows, :] for p in range(n_pat)]
        m = functools.reduce(jnp.maximum, ms)
        num = den = None
        for p in range(n_pat):
            w = jnp.exp(ms[p] - m)
            num = w * acc_ref[p, rows, :] if num is None else num + w * acc_ref[p, rows, :]
            den = w * l_ref[p, rows, :] if den is None else den + w * l_ref[p, rows, :]
        o_ref[0, rows, :] = num / den
        return carry

    lax.fori_loop(0, seq // rows_per, merge, 0)


def attn_prompt(z3, bias):
    n, seq, _ = z3.shape
    h = ATT_HEADS
    kern = functools.partial(_attn_prompt_kernel, seq=seq)
    return pl.pallas_call(
        kern,
        grid=(n, h),
        in_specs=[pl.BlockSpec((1, seq, ATT_HEAD_DIM), lambda i, j: (i, 0, j)),
                  pl.BlockSpec((1, seq, ATT_HEAD_DIM), lambda i, j: (i, 0, h + j)),
                  pl.BlockSpec((1, seq, ATT_HEAD_DIM), lambda i, j: (i, 0, 2 * h + j)),
                  pl.BlockSpec((3, 2, 1, ATT_BLOCK, 2 * ATT_BLOCK), lambda i, j: (0, 0, j, 0, 0))],
        out_specs=pl.BlockSpec((1, seq, ATT_HEAD_DIM), lambda i, j: (i, 0, j)),
        out_shape=jax.ShapeDtypeStruct((n, seq, ATT_WIDTH), F32),
        scratch_shapes=[pltpu.VMEM((len(DILATED_PATTERNS), seq, ATT_HEAD_DIM), F32)] * 3,
        compiler_params=_params("parallel", "parallel"),
        name="attn_prompt",
    )(z3, z3, z3, bias)


def _attn_sample_kernel(q_ref, kn_ref, vn_ref, kc_ref, vc_ref, bc_ref, bn_ref, o_ref):
    t = q_ref.shape[1]
    e = ATT_HEAD_DIM
    slots = kc_ref.shape[2] // ATT_HEADS
    n_pat = len(DILATED_PATTERNS)
    pad = jnp.zeros((ATT_BLOCK - t, e), F32)
    nt = (((1,), (1,)), ((), ()))
    heads = range(ATT_HEADS)
    cols = [slice(h * e, (h + 1) * e) for h in heads]
    head_rows = [pl.ds(h, slots, stride=ATT_HEADS) for h in heads]

    sc, sn = [], []
    for h in heads:
        q = q_ref[0, :, cols[h]].astype(BF16)
        kn = jnp.concatenate([kn_ref[0, :, cols[h]], pad], axis=0).astype(BF16)
        kc = kc_ref[0, 0, head_rows[h], :].astype(BF16)
        sc.append(lax.dot_general(q, kc, nt, preferred_element_type=F32) * ATT_SCALE)
        sn.append(lax.dot_general(q, kn, nt, preferred_element_type=F32) * ATT_SCALE)
    pc, pn, ls = [], [], []
    for h in heads:
        scs = [sc[h] + bc_ref[p, h] for p in range(n_pat)]
        sns = [sn[h] + bn_ref[p, h] for p in range(n_pat)]
        m = None
        for s in scs + sns:
            mx = jnp.max(s, axis=1, keepdims=True)
            m = mx if m is None else jnp.maximum(m, mx)
        pch = sum(jnp.exp(s - m) for s in scs)
        pnh = sum(jnp.exp(s - m) for s in sns)
        ls.append(jnp.sum(pch, axis=1, keepdims=True) + jnp.sum(pnh, axis=1, keepdims=True))
        pc.append(pch.astype(BF16))
        pn.append(pnh.astype(BF16))
    for h in heads:
        vn = jnp.concatenate([vn_ref[0, :, cols[h]], pad], axis=0).astype(BF16)
        vc = vc_ref[0, 0, head_rows[h], :].astype(BF16)
        acc = (jnp.dot(pc[h], vc, preferred_element_type=F32)
               + jnp.dot(pn[h], vn, preferred_element_type=F32))
        o_ref[0, :, cols[h]] = acc / ls[h]


def attn_sample(zs3, cache_k, cache_v, layer, bias_c, bias_n):
    n, t, _ = zs3.shape
    h = ATT_HEADS
    e = ATT_HEAD_DIM
    w = cache_k.shape[2] // h
    return pl.pallas_call(
        _attn_sample_kernel,
        grid=(n,),
        in_specs=[pl.BlockSpec((1, t, ATT_WIDTH), lambda i: (i, 0, 0)),
                  pl.BlockSpec((1, t, ATT_WIDTH), lambda i: (i, 0, 1)),
                  pl.BlockSpec((1, t, ATT_WIDTH), lambda i: (i, 0, 2)),
                  pl.BlockSpec((1, 1, w * h, e), lambda i: (layer, i, 0, 0)),
                  pl.BlockSpec((1, 1, w * h, e), lambda i: (layer, i, 0, 0)),
                  pl.BlockSpec((3, h, t, w), lambda i: (0, 0, 0, 0)),
                  pl.BlockSpec((3, h, t, ATT_BLOCK), lambda i: (0, 0, 0, 0))],
        out_specs=pl.BlockSpec((1, t, ATT_WIDTH), lambda i: (i, 0, 0)),
        out_shape=jax.ShapeDtypeStruct((n, t, ATT_WIDTH), F32),
        compiler_params=_params("parallel"),
        name="attn_sample",
    )(zs3, zs3, zs3, cache_k, cache_v, bias_c, bias_n)


def _ssm_prompt_kernel(u_ref, bq_ref, a_ref, cq_ref, d_ref, wg_ref, bg_ref,
                       o_ref, hfin_ref, x_ref, h_ref, *, tb, pitch):
    step = pl.program_id(0)
    nq = SSM_QUARTERS
    qw = SSM_QW
    qc = SSM_QC

    @pl.when(step == 0)
    def _():
        h_ref[...] = jnp.zeros_like(h_ref)

    nl = 2 * qw // LANES
    hl = nl // 2

    for b in range(2):
        for k in range(nq):
            u = u_ref[b, :, k * qc:(k + 1) * qc].astype(BF16)
            x = jnp.dot(u, bq_ref[k], preferred_element_type=F32)
            for c in range(nl):
                x_ref[c, pl.ds((b * nq + k) * pitch, tb), :] = x[:, c * LANES:(c + 1) * LANES]

    a = [a_ref[:, c * LANES:(c + 1) * LANES] for c in range(nl)]

    def scan(t, h):
        rows = pl.ds(t, 2 * nq, stride=pitch)
        new = [None] * nl
        for c in range(hl):
            a_re, a_im, h_re, h_im = a[c], a[hl + c], h[c], h[hl + c]
            new[c] = a_re * h_re - a_im * h_im + x_ref[c, rows, :]
            new[hl + c] = a_re * h_im + a_im * h_re + x_ref[hl + c, rows, :]
        for c in range(nl):
            x_ref[c, rows, :] = new[c]
        return tuple(new)

    h0 = tuple(h_ref[:, c * LANES:(c + 1) * LANES] for c in range(nl))
    h_new = jnp.concatenate(lax.fori_loop(0, tb, scan, h0, unroll=8), axis=1)
    h_ref[...] = h_new

    for b in range(2):
        ys = []
        for k in range(nq):
            rows = pl.ds((b * nq + k) * pitch, tb)
            hs = jnp.concatenate([x_ref[c, rows, :] for c in range(nl)], axis=1).astype(BF16)
            ys.append(jnp.dot(hs, cq_ref[k], preferred_element_type=F32))
        y = jnp.concatenate(ys, axis=1) + d_ref[...] * u_ref[b]
        gate = jnp.dot(y.astype(BF16), wg_ref[...], preferred_element_type=F32) + bg_ref[...]
        o_ref[b] = y * jax.nn.sigmoid(gate)

    @pl.when(step == pl.num_programs(0) - 1)
    def _():
        hfin_ref[...] = h_new


def ssm_prompt(z3, bq, a8, cq, d_skip, w_glu, b_glu, tb=512):
    n, seq, _ = z3.shape
    assert n == 2
    pitch = tb + 8
    ucol = 3 * ATT_WIDTH // SSM_WIDTH
    kern = functools.partial(_ssm_prompt_kernel, tb=tb, pitch=pitch)
    full = lambda *shape: pl.BlockSpec(shape, lambda s: (0,) * len(shape))
    return pl.pallas_call(
        kern,
        grid=(seq // tb,),
        in_specs=[pl.BlockSpec((n, tb, SSM_WIDTH), lambda s: (0, s, ucol)),
                  full(SSM_QUARTERS, SSM_QC, 2 * SSM_QW),
                  full(2 * SSM_QUARTERS, 2 * SSM_QW),
                  full(SSM_QUARTERS, 2 * SSM_QW, SSM_QC),
                  full(1, SSM_WIDTH),
                  full(SSM_WIDTH, SSM_WIDTH),
                  full(1, SSM_WIDTH)],
        out_specs=[pl.BlockSpec((n, tb, SSM_WIDTH), lambda s: (0, s, 0)),
                   full(2 * SSM_QUARTERS, 2 * SSM_QW)],
        out_shape=[jax.ShapeDtypeStruct((n, seq, SSM_WIDTH), F32),
                   jax.ShapeDtypeStruct((2 * SSM_QUARTERS, 2 * SSM_QW), F32)],
        scratch_shapes=[pltpu.VMEM((2 * SSM_QW // LANES, 2 * SSM_QUARTERS * pitch, LANES), F32),
                        pltpu.VMEM((2 * SSM_QUARTERS, 2 * SSM_QW), F32)],
        compiler_params=_params("arbitrary"),
        name="ssm_prompt",
    )(z3, bq, a8, cq, d_skip, w_glu, b_glu)


def _ssm_sample_kernel(u_ref, hre_ref, him_ref, bq_ref, a_ref, cq_ref, d_ref, wg_ref, bg_ref,
                       o_ref, ore_ref, oim_ref, x_ref):
    n, t, _ = u_ref.shape
    nq = SSM_QUARTERS
    qw = SSM_QW
    qc = SSM_QC
    nl = 2 * qw // LANES
    hl = nl // 2
    u_all = u_ref[...].reshape(n * t, SSM_WIDTH)
    ys = []
    for k in range(nq):
        u = u_all[:, k * qc:(k + 1) * qc]
        x = jnp.dot(u, bq_ref[k], preferred_element_type=F32, precision=lax.Precision.HIGHEST)
        for c in range(nl):
            x_ref[c] = x[:, c * LANES:(c + 1) * LANES]
        h = ([hre_ref[:, k * qw + c * LANES:k * qw + (c + 1) * LANES] for c in range(hl)]
             + [him_ref[:, k * qw + c * LANES:k * qw + (c + 1) * LANES] for c in range(hl)])
        for s in range(t):
            rows = pl.ds(s, n, stride=t)
            new = [None] * nl
            for c in range(hl):
                a_re = a_ref[k:k + 1, c * LANES:(c + 1) * LANES]
                a_im = a_ref[k:k + 1, (hl + c) * LANES:(hl + c + 1) * LANES]
                new[c] = a_re * h[c] - a_im * h[hl + c] + x_ref[c, rows, :]
                new[hl + c] = a_re * h[hl + c] + a_im * h[c] + x_ref[hl + c, rows, :]
            for c in range(nl):
                x_ref[c, rows, :] = new[c]
            h = new
        for c in range(hl):
            ore_ref[:, k * qw + c * LANES:k * qw + (c + 1) * LANES] = h[c]
            oim_ref[:, k * qw + c * LANES:k * qw + (c + 1) * LANES] = h[hl + c]
        hs = jnp.concatenate([x_ref[c] for c in range(nl)], axis=1).astype(BF16)
        ys.append(jnp.dot(hs, cq_ref[k].astype(BF16), preferred_element_type=F32))
    y = jnp.concatenate(ys, axis=1) + d_ref[...] * u_all
    gate = jnp.dot(y.astype(BF16), wg_ref[...], preferred_element_type=F32) + bg_ref[...]
    o_ref[...] = (y * jax.nn.sigmoid(gate)).reshape(n, t, SSM_WIDTH)


def ssm_sample(zs3, h_re, h_im, bq32, a8, cq32, d_skip, w_glu, b_glu):
    n, t, _ = zs3.shape
    ucol = 3 * ATT_WIDTH // SSM_WIDTH
    ns = SSM_GROUPS * SSM_STATE
    full = lambda *shape: pl.BlockSpec(shape, lambda s: (0,) * len(shape))
    return pl.pallas_call(
        _ssm_sample_kernel,
        grid=(1,),
        in_specs=[pl.BlockSpec((n, t, SSM_WIDTH), lambda s: (0, 0, ucol)),
                  full(n, ns), full(n, ns),
                  full(SSM_QUARTERS, SSM_QC, 2 * SSM_QW),
                  full(2 * SSM_QUARTERS, 2 * SSM_QW),
                  full(SSM_QUARTERS, 2 * SSM_QW, SSM_QC),
                  full(1, SSM_WIDTH),
                  full(SSM_WIDTH, SSM_WIDTH),
                  full(1, SSM_WIDTH)],
        out_specs=[full(n, t, SSM_WIDTH), full(n, ns), full(n, ns)],
        out_shape=[jax.ShapeDtypeStruct((n, t, SSM_WIDTH), F32),
                   jax.ShapeDtypeStruct((n, ns), F32),
                   jax.ShapeDtypeStruct((n, ns), F32)],
        scratch_shapes=[pltpu.VMEM((2 * SSM_QW // LANES, n * t, LANES), F32)],
        compiler_params=_params("arbitrary"),
        name="ssm_sample",
    )(zs3, h_re, h_im, bq32, a8, cq32, d_skip, w_glu, b_glu)


def _sgu_norm_v(v, g):
    gv = jax.nn.gelu(v)
    vc = gv - jnp.mean(gv, axis=-1, keepdims=True)
    return vc * lax.rsqrt(jnp.mean(vc * vc, axis=-1, keepdims=True) + EPS) * g


def _sgu_causal(w_ref):
    c = SGU_CHUNK
    keep = lax.broadcasted_iota(jnp.int32, (c, c), 0) >= lax.broadcasted_iota(jnp.int32, (c, c), 1)
    return [jnp.where(keep, w_ref[h], 0.0).astype(BF16) for h in range(SGU_HEADS)]


def _sgu_prompt_kernel(u_ref, v_ref, g_ref, w_ref, b_ref, o_ref):
    c = SGU_CHUNK
    e = SGU_HEAD_DIM
    rows = u_ref.shape[1]
    w = _sgu_causal(w_ref)
    gu = jax.nn.gelu(u_ref[0])
    gv = _sgu_norm_v(v_ref[0], g_ref[...]).astype(BF16)
    for ci in range(rows // c):
        for h in range(SGU_HEADS):
            mixed = jnp.dot(w[h], gv[ci * c:(ci + 1) * c, h * e:(h + 1) * e],
                            preferred_element_type=F32) + b_ref[h]
            o_ref[0, ci * c:(ci + 1) * c, h * e:(h + 1) * e] = (
                gu[ci * c:(ci + 1) * c, h * e:(h + 1) * e] * mixed)


def sgu_prompt(z3, g, w_sp, b_rows, rows=256):
    n, seq, _ = z3.shape
    ucol = (3 * ATT_WIDTH + SSM_WIDTH) // SGU_WIDTH
    return pl.pallas_call(
        _sgu_prompt_kernel,
        grid=(n, seq // rows),
        in_specs=[pl.BlockSpec((1, rows, SGU_WIDTH), lambda i, j: (i, j, ucol)),
                  pl.BlockSpec((1, rows, SGU_WIDTH), lambda i, j: (i, j, ucol + 1)),
                  pl.BlockSpec((1, SGU_WIDTH), lambda i, j: (0, 0)),
                  pl.BlockSpec((SGU_HEADS, SGU_CHUNK, SGU_CHUNK), lambda i, j: (0, 0, 0)),
                  pl.BlockSpec((SGU_HEADS, SGU_CHUNK, SGU_HEAD_DIM), lambda i, j: (0, 0, 0))],
        out_specs=pl.BlockSpec((1, rows, SGU_WIDTH), lambda i, j: (i, j, 0)),
        out_shape=jax.ShapeDtypeStruct((n, seq, SGU_WIDTH), F32),
        compiler_params=_params("parallel", "parallel"),
        name="sgu_prompt",
    )(z3, z3, g, w_sp, b_rows)


def _sgu_sample_kernel(u_ref, v_ref, g_ref, w_ref, b_ref, o_ref, gv_ref):
    c = SGU_CHUNK
    e = SGU_HEAD_DIM
    t = u_ref.shape[1]
    w = _sgu_causal(w_ref)
    gu = jax.nn.gelu(u_ref[0])
    gv = _sgu_norm_v(v_ref[0], g_ref[...])
    gv_ref[0] = gv
    gvp = jnp.concatenate([gv, jnp.zeros((c - t, SGU_WIDTH), F32)], axis=0).astype(BF16)
    for h in range(SGU_HEADS):
        mixed = jnp.dot(w[h], gvp[:, h * e:(h + 1) * e], preferred_element_type=F32) + b_ref[h]
        o_ref[0, :, h * e:(h + 1) * e] = gu[:, h * e:(h + 1) * e] * mixed[:t]


def sgu_sample(zs3, g, w_sp, b_rows):
    n, t, _ = zs3.shape
    ucol = (3 * ATT_WIDTH + SSM_WIDTH) // SGU_WIDTH
    return pl.pallas_call(
        _sgu_sample_kernel,
        grid=(n,),
        in_specs=[pl.BlockSpec((1, t, SGU_WIDTH), lambda i: (i, 0, ucol)),
                  pl.BlockSpec((1, t, SGU_WIDTH), lambda i: (i, 0, ucol + 1)),
                  pl.BlockSpec((1, SGU_WIDTH), lambda i: (0, 0)),
                  pl.BlockSpec((SGU_HEADS, SGU_CHUNK, SGU_CHUNK), lambda i: (0, 0, 0)),
                  pl.BlockSpec((SGU_HEADS, SGU_CHUNK, SGU_HEAD_DIM), lambda i: (0, 0, 0))],
        out_specs=[pl.BlockSpec((1, t, SGU_WIDTH), lambda i: (i, 0, 0)),
                   pl.BlockSpec((1, t, SGU_WIDTH), lambda i: (i, 0, 0))],
        out_shape=[jax.ShapeDtypeStruct((n, t, SGU_WIDTH), F32),
                   jax.ShapeDtypeStruct((n, t, SGU_WIDTH), F32)],
        compiler_params=_params("parallel"),
        name="sgu_sample",
    )(zs3, zs3, g, w_sp, b_rows)


def _out_proj_kernel(att_ref, ssm_ref, sgu_ref, x_ref, w_ref, gm_ref, gp_ref, o_ref):
    a0 = ATT_WIDTH
    a1 = ATT_WIDTH + SSM_WIDTH
    a = _rms(att_ref[...], gm_ref[:, :a0]).astype(BF16)
    s = _rms(ssm_ref[...], gm_ref[:, a0:a1]).astype(BF16)
    c = _rms(sgu_ref[...], gm_ref[:, a1:]).astype(BF16)
    y = (jnp.dot(a, w_ref[:a0, :], preferred_element_type=F32)
         + jnp.dot(s, w_ref[a0:a1, :], preferred_element_type=F32)
         + jnp.dot(c, w_ref[a1:, :], preferred_element_type=F32))
    o_ref[...] = x_ref[...] + _rms(y, gp_ref[...])


def out_proj(att, ssm, sgu, x, w, layer, gm, gp, tm):
    m, dm = x.shape
    row = lambda width: pl.BlockSpec((tm, width), lambda i: (i, 0))
    return pl.pallas_call(
        _out_proj_kernel,
        grid=(m // tm,),
        in_specs=[row(ATT_WIDTH), row(SSM_WIDTH), row(SGU_WIDTH), row(dm),
                  pl.BlockSpec((None,) + w.shape[1:], lambda i: (layer, 0, 0)),
                  pl.BlockSpec((1, dm), lambda i: (0, 0)),
                  pl.BlockSpec((1, dm), lambda i: (0, 0))],
        out_specs=row(dm),
        out_shape=jax.ShapeDtypeStruct((m, dm), F32),
        compiler_params=_params("parallel"),
        name="out_proj",
    )(att, ssm, sgu, x, w, gm, gp)


def _ffn_kernel(x_ref, gpre_ref, wg_ref, wu_ref, wd_ref, gpost_ref, o_ref, h_ref):
    j = pl.program_id(1)

    @pl.when(j == 0)
    def _():
        h_ref[...] = _rms(x_ref[...], gpre_ref[...]).astype(BF16)
        o_ref[...] = jnp.zeros_like(o_ref)

    h = h_ref[...]
    gate = jnp.dot(h, wg_ref[...], preferred_element_type=F32)
    up = jnp.dot(h, wu_ref[...], preferred_element_type=F32)
    act = (jax.nn.silu(gate) * up).astype(BF16)
    o_ref[...] += jnp.dot(act, wd_ref[...], preferred_element_type=F32)

    @pl.when(j == pl.num_programs(1) - 1)
    def _():
        o_ref[...] = x_ref[...] + _rms(o_ref[...], gpost_ref[...])


def ffn(x, gpre, wg, wu, wd, layer, gpost, tm, tf):
    m, dm = x.shape
    f = wg.shape[2]
    return pl.pallas_call(
        _ffn_kernel,
        grid=(m // tm, f // tf),
        in_specs=[pl.BlockSpec((tm, dm), lambda i, j: (i, 0)),
                  pl.BlockSpec((1, dm), lambda i, j: (0, 0)),
                  pl.BlockSpec((None, dm, tf), lambda i, j: (layer, 0, j)),
                  pl.BlockSpec((None, dm, tf), lambda i, j: (layer, 0, j)),
                  pl.BlockSpec((None, tf, dm), lambda i, j: (layer, j, 0)),
                  pl.BlockSpec((1, dm), lambda i, j: (0, 0))],
        out_specs=pl.BlockSpec((tm, dm), lambda i, j: (i, 0)),
        out_shape=jax.ShapeDtypeStruct((m, dm), F32),
        scratch_shapes=[pltpu.VMEM((tm, dm), BF16)],
        compiler_params=_params("parallel", "arbitrary"),
        name="ffn",
    )(x, gpre, wg, wu, wd, gpost)


def _t5_bucket(dist):
    max_exact = N_REL_BUCKETS // 2
    df = jnp.maximum(dist, 1).astype(F32)
    large = max_exact + (jnp.log(df / max_exact) / math.log(REL_MAX_DIST / max_exact)
                         * (N_REL_BUCKETS - max_exact)).astype(jnp.int32)
    large = jnp.minimum(large, N_REL_BUCKETS - 1)
    return jnp.where(dist < max_exact, dist, large)


def _strided_bias(rel_bias, dilation, n_steps):
    dist = jnp.arange(n_steps + 1, dtype=jnp.int32) * dilation
    return rel_bias[_t5_bucket(dist)].astype(F32)


def _bias_lookup(sb, steps, ok):
    onehot = (steps[..., None] == jnp.arange(sb.shape[0])).astype(F32)
    vals = jnp.einsum('...j,jh->h...', onehot, sb, precision=lax.Precision.HIGHEST)
    return jnp.where(ok[None], vals, NEG_INF)


def _prompt_bias_tables(rel_bias):
    qb = ATT_BLOCK
    qi = jnp.arange(qb)[:, None]
    kj = jnp.arange(2 * qb)[None, :]
    tables = []
    for window, d in DILATED_PATTERNS:
        nk = window // d
        sb = _strided_bias(rel_bias, d, nk)
        delta = qb + qi - kj
        normal = _bias_lookup(sb, jnp.clip(delta, 0, nk), (delta >= 0) & (delta <= nk))
        delta_f = qi - kj
        first = _bias_lookup(sb, jnp.clip(delta_f, 0, nk), (delta_f >= 0) & (delta_f <= nk))
        tables.append(jnp.stack([normal, first]))
    return jnp.stack(tables).astype(F32)


def _sample_bias_tables(rel_bias, w, t):
    tq = jnp.arange(t)[:, None]
    cache_delta = w + tq - jnp.arange(w)[None, :]
    new_delta = tq - jnp.arange(ATT_BLOCK)[None, :]
    out_c, out_n = [], []
    for window, d in DILATED_PATTERNS:
        nk = window // d
        sb = _strided_bias(rel_bias, d, nk)

        def table(delta, extra_ok):
            ok = (delta >= 0) & (delta % d == 0) & (delta <= nk * d) & extra_ok
            return _bias_lookup(sb, jnp.clip(delta // d, 0, nk), ok)

        out_c.append(table(cache_delta, True))
        out_n.append(table(new_delta, jnp.arange(ATT_BLOCK)[None, :] < t))
    return jnp.stack(out_c).astype(F32), jnp.stack(out_n).astype(F32)


def _ssm_tables(lam_re, lam_im, log_step, b_re, b_im, c_re, c_im):
    g, p, cg = SSM_GROUPS, SSM_STATE, SSM_GROUP
    nq = SSM_QUARTERS
    gq = g // nq
    step = jnp.exp(log_step)[:, None]
    mag = jnp.exp(lam_re * step)
    a_re = mag * jnp.cos(lam_im * step)
    a_im = mag * jnp.sin(lam_im * step)
    den = lam_re * lam_re + lam_im * lam_im
    coef_re = ((a_re - 1.0) * lam_re + a_im * lam_im) / den
    coef_im = (a_im * lam_re - (a_re - 1.0) * lam_im) / den
    bb_re = coef_re[..., None] * b_re - coef_im[..., None] * b_im
    bb_im = coef_re[..., None] * b_im + coef_im[..., None] * b_re
    eye = jnp.eye(gq, dtype=F32)

    def in_mat(bb):
        bb = bb.reshape(nq, gq, p, cg)
        return jnp.einsum('kgpc,gh->kgchp', bb, eye).reshape(nq, gq * cg, gq * p)

    def out_mat(c):
        c = c.reshape(nq, gq, cg, p)
        return jnp.einsum('kgcp,gh->kgphc', c, eye).reshape(nq, gq * p, gq * cg)

    bq = jnp.concatenate([in_mat(bb_re), in_mat(bb_im)], axis=2)
    cq = jnp.concatenate([out_mat(c_re), -out_mat(c_im)], axis=1)
    aq = jnp.concatenate([a_re.reshape(nq, gq * p), a_im.reshape(nq, gq * p)], axis=1)
    a8 = jnp.concatenate([aq, aq], axis=0)
    return bq, cq, a8


def _row(v):
    return v.reshape(1, -1)


def kernel(x_prompt, x_sample, cache_attn_k, cache_attn_v, state_ssm_re, state_ssm_im, rel_bias, w_in, w_out, g_pre_mix, g_post_mix, g_mix_out, ssm_lam_re, ssm_lam_im, ssm_log_step, ssm_b_re, ssm_b_im, ssm_c_re, ssm_c_im, ssm_d, ssm_w_glu, ssm_b_glu, sgu_g, sgu_w, sgu_b, g_pre_ffn, g_post_ffn, w_gate, w_up, w_down):
    nb, seq, dm = x_prompt.shape
    ns, ts, _ = x_sample.shape
    wbuf = cache_attn_k.shape[2]
    n_keep = min(DILATED_PATTERNS[-1][0], seq)
    h, e = ATT_HEADS, ATT_HEAD_DIM

    bias_p = _prompt_bias_tables(rel_bias)
    bias_c, bias_n = _sample_bias_tables(rel_bias, wbuf, ts)
    cache_k = cache_attn_k.reshape(DEPTH, ns, wbuf * h, e)
    cache_v = cache_attn_v.reshape(DEPTH, ns, wbuf * h, e)
    k_buf = jnp.zeros((DEPTH, nb, n_keep * h, e), F32)
    v_buf = jnp.zeros((DEPTH, nb, n_keep * h, e), F32)
    st_re = state_ssm_re.reshape(DEPTH, ns, SSM_GROUPS * SSM_STATE)
    st_im = state_ssm_im.reshape(DEPTH, ns, SSM_GROUPS * SSM_STATE)

    xp = x_prompt.reshape(nb * seq, dm)
    xs = x_sample.reshape(ns * ts, dm)
    outs = {k: [] for k in ('rp', 'ip', 'ks', 'vs', 'rs', 'is', 'us')}

    w_in16, w_out16 = w_in.astype(BF16), w_out.astype(BF16)
    w_gate16, w_up16, w_down16 = w_gate.astype(BF16), w_up.astype(BF16), w_down.astype(BF16)

    for l in range(DEPTH):
        w_glu_l = ssm_w_glu[l].astype(BF16)
        bq, cq, a8 = _ssm_tables(ssm_lam_re[l], ssm_lam_im[l], ssm_log_step[l],
                                 ssm_b_re[l], ssm_b_im[l], ssm_c_re[l], ssm_c_im[l])
        bq16, cq16 = bq.astype(BF16), cq.astype(BF16)
        sgu_b_rows = jnp.broadcast_to(sgu_b[l][:, :, None], (SGU_HEADS, SGU_CHUNK, SGU_HEAD_DIM))
        g_pre, g_post, g_mix = _row(g_pre_mix[l]), _row(g_post_mix[l]), _row(g_mix_out[l])
        d_skip, b_glu, g_sgu = _row(ssm_d[l]), _row(ssm_b_glu[l]), _row(sgu_g[l])

        z, k_buf, v_buf = in_proj(xp, g_pre, w_in16, l, tm=1024, tn=768,
                                  kv=(seq, n_keep, k_buf, v_buf))
        z = z.reshape(nb, seq, IN_COLS)
        o_att = attn_prompt(z, bias_p)
        o_ssm, h_fin = ssm_prompt(z, bq16, a8, cq16, d_skip, w_glu_l, b_glu)
        o_sgu = sgu_prompt(z, g_sgu, sgu_w[l], sgu_b_rows)
        xp = out_proj(o_att.reshape(nb * seq, ATT_WIDTH), o_ssm.reshape(nb * seq, SSM_WIDTH),
                      o_sgu.reshape(nb * seq, SGU_WIDTH), xp, w_out16, l, g_mix, g_post, tm=512)
        xp = ffn(xp, _row(g_pre_ffn[l]), w_gate16, w_up16, w_down16, l, _row(g_post_ffn[l]),
                 tm=512, tf=512)
        hq = h_fin.reshape(nb, SSM_QUARTERS, 2, SSM_GROUPS // SSM_QUARTERS, SSM_STATE)
        outs['rp'].append(hq[:, :, 0].reshape(nb, SSM_GROUPS, SSM_STATE))
        outs['ip'].append(hq[:, :, 1].reshape(nb, SSM_GROUPS, SSM_STATE))

        zs = in_proj(xs, g_pre, w_in16, l, tm=ns * ts, tn=768).reshape(ns, ts, IN_COLS)
        s_att = attn_sample(zs, cache_k, cache_v, l, bias_c, bias_n)
        s_ssm, s_re, s_im = ssm_sample(zs, st_re[l], st_im[l], bq, a8, cq, d_skip, w_glu_l, b_glu)
        s_sgu, s_gv = sgu_sample(zs, g_sgu, sgu_w[l], sgu_b_rows)
        xs = out_proj(s_att.reshape(ns * ts, ATT_WIDTH), s_ssm.reshape(ns * ts, SSM_WIDTH),
                      s_sgu.reshape(ns * ts, SGU_WIDTH), xs, w_out16, l, g_mix, g_post, tm=ns * ts)
        xs = ffn(xs, _row(g_pre_ffn[l]), w_gate16, w_up16, w_down16, l, _row(g_post_ffn[l]),
                 tm=ns * ts, tf=512)
        outs['ks'].append(zs[:, :, ATT_WIDTH:2 * ATT_WIDTH].reshape(ns, ts, h, e))
        outs['vs'].append(zs[:, :, 2 * ATT_WIDTH:3 * ATT_WIDTH].reshape(ns, ts, h, e))
        outs['rs'].append(s_re.reshape(ns, SSM_GROUPS, SSM_STATE))
        outs['is'].append(s_im.reshape(ns, SSM_GROUPS, SSM_STATE))
        outs['us'].append(s_gv)

    st = lambda key: jnp.stack(outs[key])
    kv_shape = (DEPTH, nb, n_keep, h, e)
    return (xp.reshape(nb, seq, dm), xs.reshape(ns, ts, dm),
            k_buf.reshape(kv_shape), v_buf.reshape(kv_shape), st('rp'), st('ip'),
            st('ks'), st('vs'), st('rs'), st('is'), st('us'))
```

```python
import functools
import math

import jax
import jax.numpy as jnp
from jax import lax
from jax.experimental import pallas as pl
from jax.experimental.pallas import tpu as pltpu

F32 = jnp.float32
BF16 = jnp.bfloat16

D_MODEL = 2048
DEPTH = 4
ATT_HEADS = 8
ATT_HEAD_DIM = 128
ATT_WIDTH = ATT_HEADS * ATT_HEAD_DIM
DILATED_PATTERNS = ((128, 1), (512, 4), (2048, 16))
ATT_BLOCK = 128
ATT_GROUP = (8, 8, 4)
ATT_SCALE = ATT_HEAD_DIM ** -0.5
N_REL_BUCKETS = 32
REL_MAX_DIST = 2048
SSM_WIDTH = 512
SSM_GROUP = 16
SSM_GROUPS = SSM_WIDTH // SSM_GROUP
SSM_STATE = 64
SSM_QUARTERS = 4
SSM_QW = SSM_GROUPS * SSM_STATE // SSM_QUARTERS
SSM_QC = SSM_WIDTH // SSM_QUARTERS
SGU_WIDTH = 512
SGU_HEADS = 4
SGU_HEAD_DIM = SGU_WIDTH // SGU_HEADS
SGU_CHUNK = 128
IN_COLS = 3 * ATT_WIDTH + SSM_WIDTH + 2 * SGU_WIDTH
FFN_HIDDEN = 5632
EPS = 1e-6
NEG_INF = -1e30
LANES = 128

VMEM_LIMIT_BYTES = 56 * 1024 * 1024


def _params(*sem):
    return pltpu.CompilerParams(dimension_semantics=sem, vmem_limit_bytes=VMEM_LIMIT_BYTES)


def _layer_spec(arr, layer):
    zeros = (0,) * (arr.ndim - 1)
    return pl.BlockSpec((None,) + arr.shape[1:], lambda *_: (layer,) + zeros)


def _rms(x, g):
    return x * lax.rsqrt(jnp.mean(x * x, axis=-1, keepdims=True) + EPS) * g


def _in_proj_kernel(x_ref, g_ref, w_ref, *rest, keep_from, tiles_per_seq):
    h_ref = rest[-1]
    outs = rest[-4:-1] if len(rest) > 2 else rest[:1]
    z_ref = outs[0]
    i = pl.program_id(0)
    j = pl.program_id(1)

    @pl.when(j == 0)
    def _():
        h_ref[...] = _rms(x_ref[...], g_ref[...]).astype(BF16)

    res = jnp.dot(h_ref[...], w_ref[...], preferred_element_type=F32)
    z_ref[...] = res

    if len(rest) > 2:
        tm, tn = res.shape
        keep = (i % tiles_per_seq) >= keep_from
        by_tile = {}
        for buf_ref, col0 in ((outs[1], ATT_WIDTH), (outs[2], 2 * ATT_WIDTH)):
            for head in range(ATT_HEADS):
                col = col0 + head * ATT_HEAD_DIM
                by_tile.setdefault(col // tn, []).append((buf_ref, head, col % tn))
        for tile, heads in by_tile.items():
            @pl.when(keep & (j == tile))
            def _(heads=heads):
                for buf_ref, head, off in heads:
                    rows = pl.ds(head, tm, stride=ATT_HEADS)
                    buf_ref[0, 0, rows, :] = res[:, off:off + ATT_HEAD_DIM]


def in_proj(x, g, w, layer, tm, tn, kv=None):
    m, k = x.shape
    n = w.shape[2]
    assert tn % ATT_HEAD_DIM == 0
    in_specs = [pl.BlockSpec((tm, k), lambda i, j: (i, 0)),
                _layer_spec(g, layer),
                pl.BlockSpec((None, k, tn), lambda i, j: (layer, 0, j))]
    z_spec = pl.BlockSpec((tm, tn), lambda i, j: (i, j))
    z_shape = jax.ShapeDtypeStruct((m, n), F32)
    scratch = [pltpu.VMEM((tm, k), BF16)]
    if kv is None:
        kern = functools.partial(_in_proj_kernel, keep_from=0, tiles_per_seq=1)
        return pl.pallas_call(
            kern, grid=(m // tm, n // tn), in_specs=in_specs, out_specs=z_spec, out_shape=z_shape,
            scratch_shapes=scratch, compiler_params=_params("parallel", "arbitrary"),
            name="in_proj",
        )(x, g, w)
    seq, n_keep, k_buf, v_buf = kv
    tiles_per_seq = seq // tm
    keep_from = (seq - n_keep) // tm
    assert seq % tm == 0 and (seq - n_keep) % tm == 0

    def buf_map(i, j):
        return (layer, i // tiles_per_seq, jnp.maximum(i % tiles_per_seq - keep_from, 0), 0)

    buf_spec = pl.BlockSpec((1, 1, tm * ATT_HEADS, ATT_HEAD_DIM), buf_map)
    any_spec = pl.BlockSpec(memory_space=pl.ANY)
    kern = functools.partial(_in_proj_kernel, keep_from=keep_from, tiles_per_seq=tiles_per_seq)
    return pl.pallas_call(
        kern, grid=(m // tm, n // tn),
        in_specs=in_specs + [any_spec, any_spec],
        out_specs=[z_spec, buf_spec, buf_spec],
        out_shape=[z_shape, jax.ShapeDtypeStruct(k_buf.shape, F32),
                   jax.ShapeDtypeStruct(v_buf.shape, F32)],
        input_output_aliases={3: 1, 4: 2},
        scratch_shapes=scratch, compiler_params=_params("arbitrary", "arbitrary"),
        name="in_proj_kv",
    )(x, g, w, k_buf, v_buf)


def _attn_prompt_kernel(q_ref, k_ref, v_ref, b_ref, o_ref, acc_ref, m_ref, l_ref, *, seq):
    qb = ATT_BLOCK
    n_pat = len(DILATED_PATTERNS)

    for p, (_, d) in enumerate(DILATED_PATTERNS):
        shift = int(math.log2(d))
        whole = seq // d == 2 * qb
        blocks_per_trip = ATT_GROUP[p]
        units = blocks_per_trip // 2 if whole else blocks_per_trip

        def group(g, carry, p=p, d=d, shift=shift, whole=whole, units=units):
            rows, biases = [], []
            for u in range(units):
                i = g * units + u
                if whole:
                    rows.append((pl.ds(i, 2 * qb, stride=d), pl.ds(i, 2 * qb, stride=d)))
                    biases.append(None)
                    continue
                r = i & (d - 1)
                b = i >> shift
                q_start = r + b * (qb * d)
                k_start = r + jnp.maximum(b - 1, 0) * (qb * d)
                if d == 1:
                    rows.append((pl.ds(pl.multiple_of(q_start, qb), qb),
                                 pl.ds(pl.multiple_of(k_start, qb), 2 * qb)))
                else:
                    rows.append((pl.ds(q_start, qb, stride=d), pl.ds(k_start, 2 * qb, stride=d)))
                biases.append((b == 0).astype(jnp.int32))
            nt = (((1,), (1,)), ((), ()))
            scores = [lax.dot_general(q_ref[0, qr, :].astype(BF16), k_ref[0, kr, :].astype(BF16), nt,
                                      preferred_element_type=F32) for qr, kr in rows]
            probs = []
            for (qr, _), first, s in zip(rows, biases, scores):
                if whole:
                    bias = jnp.concatenate([b_ref[p, 1, 0], b_ref[p, 0, 0]], axis=0)
                else:
                    bias = b_ref[p, first, 0]
                s = s * ATT_SCALE + bias
                m_blk = jnp.max(s, axis=1, keepdims=True)
                pe = jnp.exp(s - m_blk)
                wide = (s.shape[0], ATT_HEAD_DIM)
                m_ref[p, qr, :] = jnp.broadcast_to(m_blk, wide)
                l_ref[p, qr, :] = jnp.broadcast_to(jnp.sum(pe, axis=1, keepdims=True), wide)
                probs.append(pe.astype(BF16))
            for (qr, kr), pe in zip(rows, probs):
                acc_ref[p, qr, :] = jnp.dot(pe, v_ref[0, kr, :].astype(BF16),
                                            preferred_element_type=F32)
            return carry

        lax.fori_loop(0, seq // qb // blocks_per_trip, group, 0)

    rows_per = 2 * qb

    def merge(c, carry):
        rows = pl.ds(pl.multiple_of(c * rows_per, rows_per), rows_per)
        ms = [m_ref[p, rows, :] for p in range(n_pat)]
        m = functools.reduce(jnp.maximum, ms)
        num = den = None
        for p in range(n_pat):
            w = jnp.exp(ms[p] - m)
            num = w * acc_ref[p, rows, :] if num is None else num + w * acc_ref[p, rows, :]
            den = w * l_ref[p, rows, :] if den is None else den + w * l_ref[p, rows, :]
        o_ref[0, rows, :] = num / den
        return carry

    lax.fori_loop(0, seq // rows_per, merge, 0)


def attn_prompt(z3, bias):
    n, seq, _ = z3.shape
    h = ATT_HEADS
    kern = functools.partial(_attn_prompt_kernel, seq=seq)
    return pl.pallas_call(
        kern,
        grid=(n, h),
        in_specs=[pl.BlockSpec((1, seq, ATT_HEAD_DIM), lambda i, j: (i, 0, j)),
                  pl.BlockSpec((1, seq, ATT_HEAD_DIM), lambda i, j: (i, 0, h + j)),
                  pl.BlockSpec((1, seq, ATT_HEAD_DIM), lambda i, j: (i, 0, 2 * h + j)),
                  pl.BlockSpec((3, 2, 1, ATT_BLOCK, 2 * ATT_BLOCK), lambda i, j: (0, 0, j, 0, 0))],
        out_specs=pl.BlockSpec((1, seq, ATT_HEAD_DIM), lambda i, j: (i, 0, j)),
        out_shape=jax.ShapeDtypeStruct((n, seq, ATT_WIDTH), F32),
        scratch_shapes=[pltpu.VMEM((len(DILATED_PATTERNS), seq, ATT_HEAD_DIM), F32)] * 3,
        compiler_params=_params("parallel", "parallel"),
        name="attn_prompt",
    )(z3, z3, z3, bias)


def _attn_sample_kernel(q_ref, kn_ref, vn_ref, kc_ref, vc_ref, bc_ref, bn_ref, o_ref):
    t = q_ref.shape[1]
    e = ATT_HEAD_DIM
    slots = kc_ref.shape[2] // ATT_HEADS
    n_pat = len(DILATED_PATTERNS)
    pad = jnp.zeros((ATT_BLOCK - t, e), F32)
    nt = (((1,), (1,)), ((), ()))
    heads = range(ATT_HEADS)
    cols = [slice(h * e, (h + 1) * e) for h in heads]
    head_rows = [pl.ds(h, slots, stride=ATT_HEADS) for h in heads]

    sc, sn = [], []
    for h in heads:
        q = q_ref[0, :, cols[h]].astype(BF16)
        kn = jnp.concatenate([kn_ref[0, :, cols[h]], pad], axis=0).astype(BF16)
        kc = kc_ref[0, 0, head_rows[h], :].astype(BF16)
        sc.append(lax.dot_general(q, kc, nt, preferred_element_type=F32) * ATT_SCALE)
        sn.append(lax.dot_general(q, kn, nt, preferred_element_type=F32) * ATT_SCALE)
    pc, pn, ls = [], [], []
    for h in heads:
        scs = [sc[h] + bc_ref[p, h] for p in range(n_pat)]
        sns = [sn[h] + bn_ref[p, h] for p in range(n_pat)]
        m = None
        for s in scs + sns:
            mx = jnp.max(s, axis=1, keepdims=True)
            m = mx if m is None else jnp.maximum(m, mx)
        pch = sum(jnp.exp(s - m) for s in scs)
        pnh = sum(jnp.exp(s - m) for s in sns)
        ls.append(jnp.sum(pch, axis=1, keepdims=True) + jnp.sum(pnh, axis=1, keepdims=True))
        pc.append(pch.astype(BF16))
        pn.append(pnh.astype(BF16))
    for h in heads:
        vn = jnp.concatenate([vn_ref[0, :, cols[h]], pad], axis=0).astype(BF16)
        vc = vc_ref[0, 0, head_rows[h], :].astype(BF16)
        acc = (jnp.dot(pc[h], vc, preferred_element_type=F32)
               + jnp.dot(pn[h], vn, preferred_element_type=F32))
        o_ref[0, :, cols[h]] = acc / ls[h]


def attn_sample(zs3, cache_k, cache_v, layer, bias_c, bias_n):
    n, t, _ = zs3.shape
    h = ATT_HEADS
    e = ATT_HEAD_DIM
    w = cache_k.shape[2] // h
    return pl.pallas_call(
        _attn_sample_kernel,
        grid=(n,),
        in_specs=[pl.BlockSpec((1, t, ATT_WIDTH), lambda i: (i, 0, 0)),
                  pl.BlockSpec((1, t, ATT_WIDTH), lambda i: (i, 0, 1)),
                  pl.BlockSpec((1, t, ATT_WIDTH), lambda i: (i, 0, 2)),
                  pl.BlockSpec((1, 1, w * h, e), lambda i: (layer, i, 0, 0)),
                  pl.BlockSpec((1, 1, w * h, e), lambda i: (layer, i, 0, 0)),
                  pl.BlockSpec((3, h, t, w), lambda i: (0, 0, 0, 0)),
                  pl.BlockSpec((3, h, t, ATT_BLOCK), lambda i: (0, 0, 0, 0))],
        out_specs=pl.BlockSpec((1, t, ATT_WIDTH), lambda i: (i, 0, 0)),
        out_shape=jax.ShapeDtypeStruct((n, t, ATT_WIDTH), F32),
        compiler_params=_params("parallel"),
        name="attn_sample",
    )(zs3, zs3, zs3, cache_k, cache_v, bias_c, bias_n)


def _ssm_prompt_kernel(u_ref, bq_ref, a_ref, cq_ref, d_ref, wg_ref, bg_ref,
                       o_ref, hfin_ref, x_ref, h_ref, *, tb, pitch):
    step = pl.program_id(0)
    nq = SSM_QUARTERS
    qw = SSM_QW
    qc = SSM_QC

    @pl.when(step == 0)
    def _():
        h_ref[...] = jnp.zeros_like(h_ref)

    nl = 2 * qw // LANES
    hl = nl // 2

    for b in range(2):
        for k in range(nq):
            u = u_ref[b, :, k * qc:(k + 1) * qc].astype(BF16)
            x = jnp.dot(u, bq_ref[k], preferred_element_type=F32)
            for c in range(nl):
                x_ref[c, pl.ds((b * nq + k) * pitch, tb), :] = x[:, c * LANES:(c + 1) * LANES]

    a = [a_ref[:, c * LANES:(c + 1) * LANES] for c in range(nl)]

    def scan(t, h):
        rows = pl.ds(t, 2 * nq, stride=pitch)
        new = [None] * nl
        for c in range(hl):
            a_re, a_im, h_re, h_im = a[c], a[hl + c], h[c], h[hl + c]
            new[c] = a_re * h_re - a_im * h_im + x_ref[c, rows, :]
            new[hl + c] = a_re * h_im + a_im * h_re + x_ref[hl + c, rows, :]
        for c in range(nl):
            x_ref[c, rows, :] = new[c]
        return tuple(new)

    h0 = tuple(h_ref[:, c * LANES:(c + 1) * LANES] for c in range(nl))
    h_new = jnp.concatenate(lax.fori_loop(0, tb, scan, h0, unroll=8), axis=1)
    h_ref[...] = h_new

    for b in range(2):
        ys = []
        for k in range(nq):
            rows = pl.ds((b * nq + k) * pitch, tb)
            hs = jnp.concatenate([x_ref[c, rows, :] for c in range(nl)], axis=1).astype(BF16)
            ys.append(jnp.dot(hs, cq_ref[k], preferred_element_type=F32))
        y = jnp.concatenate(ys, axis=1) + d_ref[...] * u_ref[b]
        gate = jnp.dot(y.astype(BF16), wg_ref[...], preferred_element_type=F32) + bg_ref[...]
        o_ref[b] = y * jax.nn.sigmoid(gate)

    @pl.when(step == pl.num_programs(0) - 1)
    def _():
        hfin_ref[...] = h_new


def ssm_prompt(z3, layer, bq, a8, cq, d_skip, w_glu, b_glu, tb=512):
    n, seq, _ = z3.shape
    assert n == 2
    pitch = tb + 8
    ucol = 3 * ATT_WIDTH // SSM_WIDTH
    kern = functools.partial(_ssm_prompt_kernel, tb=tb, pitch=pitch)
    full = lambda *shape: pl.BlockSpec(shape, lambda s: (0,) * len(shape))
    return pl.pallas_call(
        kern,
        grid=(seq // tb,),
        in_specs=[pl.BlockSpec((n, tb, SSM_WIDTH), lambda s: (0, s, ucol))]
        + [_layer_spec(p, layer) for p in (bq, a8, cq, d_skip, w_glu, b_glu)],
        out_specs=[pl.BlockSpec((n, tb, SSM_WIDTH), lambda s: (0, s, 0)),
                   full(2 * SSM_QUARTERS, 2 * SSM_QW)],
        out_shape=[jax.ShapeDtypeStruct((n, seq, SSM_WIDTH), F32),
                   jax.ShapeDtypeStruct((2 * SSM_QUARTERS, 2 * SSM_QW), F32)],
        scratch_shapes=[pltpu.VMEM((2 * SSM_QW // LANES, 2 * SSM_QUARTERS * pitch, LANES), F32),
                        pltpu.VMEM((2 * SSM_QUARTERS, 2 * SSM_QW), F32)],
        compiler_params=_params("arbitrary"),
        name="ssm_prompt",
    )(z3, bq, a8, cq, d_skip, w_glu, b_glu)


def _ssm_sample_kernel(u_ref, hre_ref, him_ref, bq_ref, a_ref, cq_ref, d_ref, wg_ref, bg_ref,
                       o_ref, ore_ref, oim_ref, x_ref):
    n, t, _ = u_ref.shape
    nq = SSM_QUARTERS
    qw = SSM_QW
    qc = SSM_QC
    nl = 2 * qw // LANES
    hl = nl // 2
    u_all = u_ref[...].reshape(n * t, SSM_WIDTH)
    ys = []
    for k in range(nq):
        u = u_all[:, k * qc:(k + 1) * qc]
        x = jnp.dot(u, bq_ref[k], preferred_element_type=F32, precision=lax.Precision.HIGHEST)
        for c in range(nl):
            x_ref[c] = x[:, c * LANES:(c + 1) * LANES]
        h = ([hre_ref[:, k * qw + c * LANES:k * qw + (c + 1) * LANES] for c in range(hl)]
             + [him_ref[:, k * qw + c * LANES:k * qw + (c + 1) * LANES] for c in range(hl)])
        for s in range(t):
            rows = pl.ds(s, n, stride=t)
            new = [None] * nl
            for c in range(hl):
                a_re = a_ref[k:k + 1, c * LANES:(c + 1) * LANES]
                a_im = a_ref[k:k + 1, (hl + c) * LANES:(hl + c + 1) * LANES]
                new[c] = a_re * h[c] - a_im * h[hl + c] + x_ref[c, rows, :]
                new[hl + c] = a_re * h[hl + c] + a_im * h[c] + x_ref[hl + c, rows, :]
            for c in range(nl):
                x_ref[c, rows, :] = new[c]
            h = new
        for c in range(hl):
            ore_ref[:, k * qw + c * LANES:k * qw + (c + 1) * LANES] = h[c]
            oim_ref[:, k * qw + c * LANES:k * qw + (c + 1) * LANES] = h[hl + c]
        hs = jnp.concatenate([x_ref[c] for c in range(nl)], axis=1).astype(BF16)
        ys.append(jnp.dot(hs, cq_ref[k].astype(BF16), preferred_element_type=F32))
    y = jnp.concatenate(ys, axis=1) + d_ref[...] * u_all
    gate = jnp.dot(y.astype(BF16), wg_ref[...], preferred_element_type=F32) + bg_ref[...]
    o_ref[...] = (y * jax.nn.sigmoid(gate)).reshape(n, t, SSM_WIDTH)


def ssm_sample(zs3, layer, h_re, h_im, bq32, a8, cq32, d_skip, w_glu, b_glu):
    n, t, _ = zs3.shape
    ucol = 3 * ATT_WIDTH // SSM_WIDTH
    ns = SSM_GROUPS * SSM_STATE
    full = lambda *shape: pl.BlockSpec(shape, lambda s: (0,) * len(shape))
    return pl.pallas_call(
        _ssm_sample_kernel,
        grid=(1,),
        in_specs=[pl.BlockSpec((n, t, SSM_WIDTH), lambda s: (0, 0, ucol))]
        + [_layer_spec(p, layer) for p in (h_re, h_im, bq32, a8, cq32, d_skip, w_glu, b_glu)],
        out_specs=[full(n, t, SSM_WIDTH), full(n, ns), full(n, ns)],
        out_shape=[jax.ShapeDtypeStruct((n, t, SSM_WIDTH), F32),
                   jax.ShapeDtypeStruct((n, ns), F32),
                   jax.ShapeDtypeStruct((n, ns), F32)],
        scratch_shapes=[pltpu.VMEM((2 * SSM_QW // LANES, n * t, LANES), F32)],
        compiler_params=_params("arbitrary"),
        name="ssm_sample",
    )(zs3, h_re, h_im, bq32, a8, cq32, d_skip, w_glu, b_glu)


def _sgu_norm_v(v, g):
    gv = jax.nn.gelu(v)
    vc = gv - jnp.mean(gv, axis=-1, keepdims=True)
    return vc * lax.rsqrt(jnp.mean(vc * vc, axis=-1, keepdims=True) + EPS) * g


def _sgu_causal(w_ref):
    c = SGU_CHUNK
    keep = lax.broadcasted_iota(jnp.int32, (c, c), 0) >= lax.broadcasted_iota(jnp.int32, (c, c), 1)
    return [jnp.where(keep, w_ref[h], 0.0).astype(BF16) for h in range(SGU_HEADS)]


def _sgu_prompt_kernel(u_ref, v_ref, g_ref, w_ref, b_ref, o_ref):
    c = SGU_CHUNK
    e = SGU_HEAD_DIM
    rows = u_ref.shape[1]
    w = _sgu_causal(w_ref)
    gu = jax.nn.gelu(u_ref[0])
    gv = _sgu_norm_v(v_ref[0], g_ref[...]).astype(BF16)
    for ci in range(rows // c):
        for h in range(SGU_HEADS):
            mixed = jnp.dot(w[h], gv[ci * c:(ci + 1) * c, h * e:(h + 1) * e],
                            preferred_element_type=F32) + b_ref[h]
            o_ref[0, ci * c:(ci + 1) * c, h * e:(h + 1) * e] = (
                gu[ci * c:(ci + 1) * c, h * e:(h + 1) * e] * mixed)


def sgu_prompt(z3, layer, g, w_sp, b_rows, rows=256):
    n, seq, _ = z3.shape
    ucol = (3 * ATT_WIDTH + SSM_WIDTH) // SGU_WIDTH
    return pl.pallas_call(
        _sgu_prompt_kernel,
        grid=(n, seq // rows),
        in_specs=[pl.BlockSpec((1, rows, SGU_WIDTH), lambda i, j: (i, j, ucol)),
                  pl.BlockSpec((1, rows, SGU_WIDTH), lambda i, j: (i, j, ucol + 1)),
                  _layer_spec(g, layer), _layer_spec(w_sp, layer), _layer_spec(b_rows, layer)],
        out_specs=pl.BlockSpec((1, rows, SGU_WIDTH), lambda i, j: (i, j, 0)),
        out_shape=jax.ShapeDtypeStruct((n, seq, SGU_WIDTH), F32),
        compiler_params=_params("parallel", "parallel"),
        name="sgu_prompt",
    )(z3, z3, g, w_sp, b_rows)


def _sgu_sample_kernel(u_ref, v_ref, g_ref, w_ref, b_ref, o_ref, gv_ref):
    c = SGU_CHUNK
    e = SGU_HEAD_DIM
    t = u_ref.shape[1]
    w = _sgu_causal(w_ref)
    gu = jax.nn.gelu(u_ref[0])
    gv = _sgu_norm_v(v_ref[0], g_ref[...])
    gv_ref[0] = gv
    gvp = jnp.concatenate([gv, jnp.zeros((c - t, SGU_WIDTH), F32)], axis=0).astype(BF16)
    for h in range(SGU_HEADS):
        mixed = jnp.dot(w[h], gvp[:, h * e:(h + 1) * e], preferred_element_type=F32) + b_ref[h]
        o_ref[0, :, h * e:(h + 1) * e] = gu[:, h * e:(h + 1) * e] * mixed[:t]


def sgu_sample(zs3, layer, g, w_sp, b_rows):
    n, t, _ = zs3.shape
    ucol = (3 * ATT_WIDTH + SSM_WIDTH) // SGU_WIDTH
    return pl.pallas_call(
        _sgu_sample_kernel,
        grid=(n,),
        in_specs=[pl.BlockSpec((1, t, SGU_WIDTH), lambda i: (i, 0, ucol)),
                  pl.BlockSpec((1, t, SGU_WIDTH), lambda i: (i, 0, ucol + 1)),
                  _layer_spec(g, layer), _layer_spec(w_sp, layer), _layer_spec(b_rows, layer)],
        out_specs=[pl.BlockSpec((1, t, SGU_WIDTH), lambda i: (i, 0, 0)),
                   pl.BlockSpec((1, t, SGU_WIDTH), lambda i: (i, 0, 0))],
        out_shape=[jax.ShapeDtypeStruct((n, t, SGU_WIDTH), F32),
                   jax.ShapeDtypeStruct((n, t, SGU_WIDTH), F32)],
        compiler_params=_params("parallel"),
        name="sgu_sample",
    )(zs3, zs3, g, w_sp, b_rows)


def _out_proj_kernel(att_ref, ssm_ref, sgu_ref, x_ref, w_ref, gm_ref, gp_ref, o_ref):
    a0 = ATT_WIDTH
    a1 = ATT_WIDTH + SSM_WIDTH
    a = _rms(att_ref[...], gm_ref[:, :a0]).astype(BF16)
    s = _rms(ssm_ref[...], gm_ref[:, a0:a1]).astype(BF16)
    c = _rms(sgu_ref[...], gm_ref[:, a1:]).astype(BF16)
    y = (jnp.dot(a, w_ref[:a0, :], preferred_element_type=F32)
         + jnp.dot(s, w_ref[a0:a1, :], preferred_element_type=F32)
         + jnp.dot(c, w_ref[a1:, :], preferred_element_type=F32))
    o_ref[...] = x_ref[...] + _rms(y, gp_ref[...])


def out_proj(att, ssm, sgu, x, w, layer, gm, gp, tm):
    m, dm = x.shape
    row = lambda width: pl.BlockSpec((tm, width), lambda i: (i, 0))
    return pl.pallas_call(
        _out_proj_kernel,
        grid=(m // tm,),
        in_specs=[row(ATT_WIDTH), row(SSM_WIDTH), row(SGU_WIDTH), row(dm),
                  _layer_spec(w, layer), _layer_spec(gm, layer), _layer_spec(gp, layer)],
        out_specs=row(dm),
        out_shape=jax.ShapeDtypeStruct((m, dm), F32),
        compiler_params=_params("parallel"),
        name="out_proj",
    )(att, ssm, sgu, x, w, gm, gp)


def _ffn_kernel(x_ref, gpre_ref, wg_ref, wu_ref, wd_ref, gpost_ref, o_ref, h_ref):
    j = pl.program_id(1)

    @pl.when(j == 0)
    def _():
        h_ref[...] = _rms(x_ref[...], gpre_ref[...]).astype(BF16)
        o_ref[...] = jnp.zeros_like(o_ref)

    h = h_ref[...]
    gate = jnp.dot(h, wg_ref[...], preferred_element_type=F32)
    up = jnp.dot(h, wu_ref[...], preferred_element_type=F32)
    act = (jax.nn.silu(gate) * up).astype(BF16)
    o_ref[...] += jnp.dot(act, wd_ref[...], preferred_element_type=F32)

    @pl.when(j == pl.num_programs(1) - 1)
    def _():
        o_ref[...] = x_ref[...] + _rms(o_ref[...], gpost_ref[...])


def ffn(x, gpre, wg, wu, wd, layer, gpost, tm, tf):
    m, dm = x.shape
    f = wg.shape[2]
    return pl.pallas_call(
        _ffn_kernel,
        grid=(m // tm, f // tf),
        in_specs=[pl.BlockSpec((tm, dm), lambda i, j: (i, 0)),
                  _layer_spec(gpre, layer),
                  pl.BlockSpec((None, dm, tf), lambda i, j: (layer, 0, j)),
                  pl.BlockSpec((None, dm, tf), lambda i, j: (layer, 0, j)),
                  pl.BlockSpec((None, tf, dm), lambda i, j: (layer, j, 0)),
                  _layer_spec(gpost, layer)],
        out_specs=pl.BlockSpec((tm, dm), lambda i, j: (i, 0)),
        out_shape=jax.ShapeDtypeStruct((m, dm), F32),
        scratch_shapes=[pltpu.VMEM((tm, dm), BF16)],
        compiler_params=_params("parallel", "arbitrary"),
        name="ffn",
    )(x, gpre, wg, wu, wd, gpost)


def _t5_bucket(dist):
    max_exact = N_REL_BUCKETS // 2
    df = jnp.maximum(dist, 1).astype(F32)
    large = max_exact + (jnp.log(df / max_exact) / math.log(REL_MAX_DIST / max_exact)
                         * (N_REL_BUCKETS - max_exact)).astype(jnp.int32)
    large = jnp.minimum(large, N_REL_BUCKETS - 1)
    return jnp.where(dist < max_exact, dist, large)


def _strided_bias(rel_bias, dilation, n_steps):
    dist = jnp.arange(n_steps + 1, dtype=jnp.int32) * dilation
    return rel_bias[_t5_bucket(dist)].astype(F32)


def _bias_lookup(sb, steps, ok):
    onehot = (steps[..., None] == jnp.arange(sb.shape[0])).astype(F32)
    vals = jnp.einsum('...j,jh->h...', onehot, sb, precision=lax.Precision.HIGHEST)
    return jnp.where(ok[None], vals, NEG_INF)


def _prompt_bias_tables(rel_bias):
    qb = ATT_BLOCK
    qi = jnp.arange(qb)[:, None]
    kj = jnp.arange(2 * qb)[None, :]
    tables = []
    for window, d in DILATED_PATTERNS:
        nk = window // d
        sb = _strided_bias(rel_bias, d, nk)
        delta = qb + qi - kj
        normal = _bias_lookup(sb, jnp.clip(delta, 0, nk), (delta >= 0) & (delta <= nk))
        delta_f = qi - kj
        first = _bias_lookup(sb, jnp.clip(delta_f, 0, nk), (delta_f >= 0) & (delta_f <= nk))
        tables.append(jnp.stack([normal, first]))
    return jnp.stack(tables).astype(F32)


def _sample_bias_tables(rel_bias, w, t):
    tq = jnp.arange(t)[:, None]
    cache_delta = w + tq - jnp.arange(w)[None, :]
    new_delta = tq - jnp.arange(ATT_BLOCK)[None, :]
    out_c, out_n = [], []
    for window, d in DILATED_PATTERNS:
        nk = window // d
        sb = _strided_bias(rel_bias, d, nk)

        def table(delta, extra_ok):
            ok = (delta >= 0) & (delta % d == 0) & (delta <= nk * d) & extra_ok
            return _bias_lookup(sb, jnp.clip(delta // d, 0, nk), ok)

        out_c.append(table(cache_delta, True))
        out_n.append(table(new_delta, jnp.arange(ATT_BLOCK)[None, :] < t))
    return jnp.stack(out_c).astype(F32), jnp.stack(out_n).astype(F32)


def _ssm_tables(lam_re, lam_im, log_step, b_re, b_im, c_re, c_im):
    g, p, cg = SSM_GROUPS, SSM_STATE, SSM_GROUP
    nq = SSM_QUARTERS
    gq = g // nq
    step = jnp.exp(log_step)[:, None]
    mag = jnp.exp(lam_re * step)
    a_re = mag * jnp.cos(lam_im * step)
    a_im = mag * jnp.sin(lam_im * step)
    den = lam_re * lam_re + lam_im * lam_im
    coef_re = ((a_re - 1.0) * lam_re + a_im * lam_im) / den
    coef_im = (a_im * lam_re - (a_re - 1.0) * lam_im) / den
    bb_re = coef_re[..., None] * b_re - coef_im[..., None] * b_im
    bb_im = coef_re[..., None] * b_im + coef_im[..., None] * b_re
    eye = jnp.eye(gq, dtype=F32)

    def in_mat(bb):
        bb = bb.reshape(nq, gq, p, cg)
        return jnp.einsum('kgpc,gh->kgchp', bb, eye).reshape(nq, gq * cg, gq * p)

    def out_mat(c):
        c = c.reshape(nq, gq, cg, p)
        return jnp.einsum('kgcp,gh->kgphc', c, eye).reshape(nq, gq * p, gq * cg)

    bq = jnp.concatenate([in_mat(bb_re), in_mat(bb_im)], axis=2)
    cq = jnp.concatenate([out_mat(c_re), -out_mat(c_im)], axis=1)
    aq = jnp.concatenate([a_re.reshape(nq, gq * p), a_im.reshape(nq, gq * p)], axis=1)
    a8 = jnp.concatenate([aq, aq], axis=0)
    return bq, cq, a8


def _rows(v):
    return v.reshape(v.shape[0], 1, -1)


def kernel(x_prompt, x_sample, cache_attn_k, cache_attn_v, state_ssm_re, state_ssm_im, rel_bias, w_in, w_out, g_pre_mix, g_post_mix, g_mix_out, ssm_lam_re, ssm_lam_im, ssm_log_step, ssm_b_re, ssm_b_im, ssm_c_re, ssm_c_im, ssm_d, ssm_w_glu, ssm_b_glu, sgu_g, sgu_w, sgu_b, g_pre_ffn, g_post_ffn, w_gate, w_up, w_down):
    nb, seq, dm = x_prompt.shape
    ns, ts, _ = x_sample.shape
    wbuf = cache_attn_k.shape[2]
    n_keep = min(DILATED_PATTERNS[-1][0], seq)
    h, e = ATT_HEADS, ATT_HEAD_DIM

    bias_p = _prompt_bias_tables(rel_bias)
    bias_c, bias_n = _sample_bias_tables(rel_bias, wbuf, ts)
    cache_k = cache_attn_k.reshape(DEPTH, ns, wbuf * h, e)
    cache_v = cache_attn_v.reshape(DEPTH, ns, wbuf * h, e)
    k_buf = jnp.zeros((DEPTH, nb, n_keep * h, e), F32)
    v_buf = jnp.zeros((DEPTH, nb, n_keep * h, e), F32)
    st_re = state_ssm_re.reshape(DEPTH, ns, SSM_GROUPS * SSM_STATE)
    st_im = state_ssm_im.reshape(DEPTH, ns, SSM_GROUPS * SSM_STATE)

    xp = x_prompt.reshape(nb * seq, dm)
    xs = x_sample.reshape(ns * ts, dm)
    outs = {k: [] for k in ('rp', 'ip', 'ks', 'vs', 'rs', 'is', 'us')}

    w_in16, w_out16 = w_in.astype(BF16), w_out.astype(BF16)
    w_gate16, w_up16, w_down16 = w_gate.astype(BF16), w_up.astype(BF16), w_down.astype(BF16)
    w_glu16 = ssm_w_glu.astype(BF16)
    bq, cq, a8 = jax.vmap(_ssm_tables)(ssm_lam_re, ssm_lam_im, ssm_log_step,
                                       ssm_b_re, ssm_b_im, ssm_c_re, ssm_c_im)
    bq16, cq16 = bq.astype(BF16), cq.astype(BF16)
    sgu_b_rows = jnp.broadcast_to(sgu_b[..., None], (DEPTH, SGU_HEADS, SGU_CHUNK, SGU_HEAD_DIM))
    g_pre, g_post, g_mix = _rows(g_pre_mix), _rows(g_post_mix), _rows(g_mix_out)
    g_pre_f, g_post_f = _rows(g_pre_ffn), _rows(g_post_ffn)
    d_skip, b_glu, g_sgu = _rows(ssm_d), _rows(ssm_b_glu), _rows(sgu_g)

    for l in range(DEPTH):
        z, k_buf, v_buf = in_proj(xp, g_pre, w_in16, l, tm=1024, tn=768,
                                  kv=(seq, n_keep, k_buf, v_buf))
        z = z.reshape(nb, seq, IN_COLS)
        o_att = attn_prompt(z, bias_p)
        o_ssm, h_fin = ssm_prompt(z, l, bq16, a8, cq16, d_skip, w_glu16, b_glu)
        o_sgu = sgu_prompt(z, l, g_sgu, sgu_w, sgu_b_rows)
        xp = out_proj(o_att.reshape(nb * seq, ATT_WIDTH), o_ssm.reshape(nb * seq, SSM_WIDTH),
                      o_sgu.reshape(nb * seq, SGU_WIDTH), xp, w_out16, l, g_mix, g_post, tm=512)
        xp = ffn(xp, g_pre_f, w_gate16, w_up16, w_down16, l, g_post_f, tm=512, tf=512)
        hq = h_fin.reshape(nb, SSM_QUARTERS, 2, SSM_GROUPS // SSM_QUARTERS, SSM_STATE)
        outs['rp'].append(hq[:, :, 0].reshape(nb, SSM_GROUPS, SSM_STATE))
        outs['ip'].append(hq[:, :, 1].reshape(nb, SSM_GROUPS, SSM_STATE))

        zs = in_proj(xs, g_pre, w_in16, l, tm=ns * ts, tn=768).reshape(ns, ts, IN_COLS)
        s_att = attn_sample(zs, cache_k, cache_v, l, bias_c, bias_n)
        s_ssm, s_re, s_im = ssm_sample(zs, l, st_re, st_im, bq, a8, cq, d_skip, w_glu16, b_glu)
        s_sgu, s_gv = sgu_sample(zs, l, g_sgu, sgu_w, sgu_b_rows)
        xs = out_proj(s_att.reshape(ns * ts, ATT_WIDTH), s_ssm.reshape(ns * ts, SSM_WIDTH),
                      s_sgu.reshape(ns * ts, SGU_WIDTH), xs, w_out16, l, g_mix, g_post, tm=ns * ts)
        xs = ffn(xs, g_pre_f, w_gate16, w_up16, w_down16, l, g_post_f, tm=ns * ts, tf=512)
        outs['ks'].append(zs[:, :, ATT_WIDTH:2 * ATT_WIDTH].reshape(ns, ts, h, e))
        outs['vs'].append(zs[:, :, 2 * ATT_WIDTH:3 * ATT_WIDTH].reshape(ns, ts, h, e))
        outs['rs'].append(s_re.reshape(ns, SSM_GROUPS, SSM_STATE))
        outs['is'].append(s_im.reshape(ns, SSM_GROUPS, SSM_STATE))
        outs['us'].append(s_gv)

    st = lambda key: jnp.stack(outs[key])
    kv_shape = (DEPTH, nb, n_keep, h, e)
    return (xp.reshape(nb, seq, dm), xs.reshape(ns, ts, dm),
            k_buf.reshape(kv_shape), v_buf.reshape(kv_shape), st('rp'), st('ip'),
            st('ks'), st('vs'), st('rs'), st('is'), st('us'))
```

```python
import functools
import math

import jax
import jax.numpy as jnp
from jax import lax
from jax.experimental import pallas as pl
from jax.experimental.pallas import tpu as pltpu

F32 = jnp.float32
BF16 = jnp.bfloat16

D_MODEL = 2048
DEPTH = 4
ATT_HEADS = 8
ATT_HEAD_DIM = 128
ATT_WIDTH = ATT_HEADS * ATT_HEAD_DIM
DILATED_PATTERNS = ((128, 1), (512, 4), (2048, 16))
ATT_BLOCK = 128
ATT_GROUP = (8, 8, 4)
ATT_SCALE = ATT_HEAD_DIM ** -0.5
N_REL_BUCKETS = 32
REL_MAX_DIST = 2048
SSM_WIDTH = 512
SSM_GROUP = 16
SSM_GROUPS = SSM_WIDTH // SSM_GROUP
SSM_STATE = 64
SSM_QUARTERS = 4
SSM_QW = SSM_GROUPS * SSM_STATE // SSM_QUARTERS
SSM_QC = SSM_WIDTH // SSM_QUARTERS
SGU_WIDTH = 512
SGU_HEADS = 4
SGU_HEAD_DIM = SGU_WIDTH // SGU_HEADS
SGU_CHUNK = 128
IN_COLS = 3 * ATT_WIDTH + SSM_WIDTH + 2 * SGU_WIDTH
FFN_HIDDEN = 5632
EPS = 1e-6
NEG_INF = -1e30
LANES = 128
CAST_BLOCK = 256

VMEM_LIMIT_BYTES = 56 * 1024 * 1024


def _params(*sem):
    return pltpu.CompilerParams(dimension_semantics=sem, vmem_limit_bytes=VMEM_LIMIT_BYTES)


def _layer_spec(arr, layer):
    zeros = (0,) * (arr.ndim - 1)
    return pl.BlockSpec((None,) + arr.shape[1:], lambda *_: (layer,) + zeros)


def _rms(x, g):
    return x * lax.rsqrt(jnp.mean(x * x, axis=-1, keepdims=True) + EPS) * g


def _in_proj_kernel(x_ref, g_ref, w_ref, *rest, keep_from, tiles_per_seq):
    h_ref = rest[-1]
    outs = rest[-4:-1] if len(rest) > 2 else rest[:1]
    z_ref = outs[0]
    i = pl.program_id(0)
    j = pl.program_id(1)

    @pl.when(j == 0)
    def _():
        h_ref[...] = _rms(x_ref[...], g_ref[...]).astype(BF16)

    res = jnp.dot(h_ref[...], w_ref[...], preferred_element_type=F32)
    z_ref[...] = res

    if len(rest) > 2:
        tm, tn = res.shape
        keep = (i % tiles_per_seq) >= keep_from
        by_tile = {}
        for buf_ref, col0 in ((outs[1], ATT_WIDTH), (outs[2], 2 * ATT_WIDTH)):
            for head in range(ATT_HEADS):
                col = col0 + head * ATT_HEAD_DIM
                by_tile.setdefault(col // tn, []).append((buf_ref, head, col % tn))
        for tile, heads in by_tile.items():
            @pl.when(keep & (j == tile))
            def _(heads=heads):
                for buf_ref, head, off in heads:
                    rows = pl.ds(head, tm, stride=ATT_HEADS)
                    buf_ref[0, 0, rows, :] = res[:, off:off + ATT_HEAD_DIM]


def in_proj(x, g, w, layer, tm, tn, kv=None):
    m, k = x.shape
    n = w.shape[1]
    assert tn % ATT_HEAD_DIM == 0
    in_specs = [pl.BlockSpec((tm, k), lambda i, j: (i, 0)),
                _layer_spec(g, layer),
                pl.BlockSpec((k, tn), lambda i, j: (0, j))]
    z_spec = pl.BlockSpec((tm, tn), lambda i, j: (i, j))
    z_shape = jax.ShapeDtypeStruct((m, n), F32)
    scratch = [pltpu.VMEM((tm, k), BF16)]
    if kv is None:
        kern = functools.partial(_in_proj_kernel, keep_from=0, tiles_per_seq=1)
        return pl.pallas_call(
            kern, grid=(m // tm, n // tn), in_specs=in_specs, out_specs=z_spec, out_shape=z_shape,
            scratch_shapes=scratch, compiler_params=_params("parallel", "arbitrary"),
            name="in_proj",
        )(x, g, w)
    seq, n_keep, k_buf, v_buf = kv
    tiles_per_seq = seq // tm
    keep_from = (seq - n_keep) // tm
    assert seq % tm == 0 and (seq - n_keep) % tm == 0

    def buf_map(i, j):
        return (layer, i // tiles_per_seq, jnp.maximum(i % tiles_per_seq - keep_from, 0), 0)

    buf_spec = pl.BlockSpec((1, 1, tm * ATT_HEADS, ATT_HEAD_DIM), buf_map)
    any_spec = pl.BlockSpec(memory_space=pl.ANY)
    kern = functools.partial(_in_proj_kernel, keep_from=keep_from, tiles_per_seq=tiles_per_seq)
    return pl.pallas_call(
        kern, grid=(m // tm, n // tn),
        in_specs=in_specs + [any_spec, any_spec],
        out_specs=[z_spec, buf_spec, buf_spec],
        out_shape=[z_shape, jax.ShapeDtypeStruct(k_buf.shape, F32),
                   jax.ShapeDtypeStruct(v_buf.shape, F32)],
        input_output_aliases={3: 1, 4: 2},
        scratch_shapes=scratch, compiler_params=_params("arbitrary", "arbitrary"),
        name="in_proj_kv",
    )(x, g, w, k_buf, v_buf)


def _attn_prompt_kernel(q_ref, k_ref, v_ref, b_ref, o_ref, acc_ref, m_ref, l_ref, *, seq):
    qb = ATT_BLOCK
    n_pat = len(DILATED_PATTERNS)

    for p, (_, d) in enumerate(DILATED_PATTERNS):
        shift = int(math.log2(d))
        whole = seq // d == 2 * qb
        blocks_per_trip = ATT_GROUP[p]
        units = blocks_per_trip // 2 if whole else blocks_per_trip

        def group(g, carry, p=p, d=d, shift=shift, whole=whole, units=units):
            rows, biases = [], []
            for u in range(units):
                i = g * units + u
                if whole:
                    rows.append((pl.ds(i, 2 * qb, stride=d), pl.ds(i, 2 * qb, stride=d)))
                    biases.append(None)
                    continue
                r = i & (d - 1)
                b = i >> shift
                q_start = r + b * (qb * d)
                k_start = r + jnp.maximum(b - 1, 0) * (qb * d)
                if d == 1:
                    rows.append((pl.ds(pl.multiple_of(q_start, qb), qb),
                                 pl.ds(pl.multiple_of(k_start, qb), 2 * qb)))
                else:
                    rows.append((pl.ds(q_start, qb, stride=d), pl.ds(k_start, 2 * qb, stride=d)))
                biases.append((b == 0).astype(jnp.int32))
            nt = (((1,), (1,)), ((), ()))
            scores = [lax.dot_general(q_ref[0, qr, :].astype(BF16), k_ref[0, kr, :].astype(BF16), nt,
                                      preferred_element_type=F32) for qr, kr in rows]
            probs = []
            for (qr, _), first, s in zip(rows, biases, scores):
                if whole:
                    bias = jnp.concatenate([b_ref[p, 1, 0], b_ref[p, 0, 0]], axis=0)
                else:
                    bias = b_ref[p, first, 0]
                s = s * ATT_SCALE + bias
                m_blk = jnp.max(s, axis=1, keepdims=True)
                pe = jnp.exp(s - m_blk)
                wide = (s.shape[0], ATT_HEAD_DIM)
                m_ref[p, qr, :] = jnp.broadcast_to(m_blk, wide)
                l_ref[p, qr, :] = jnp.broadcast_to(jnp.sum(pe, axis=1, keepdims=True), wide)
                probs.append(pe.astype(BF16))
            for (qr, kr), pe in zip(rows, probs):
                acc_ref[p, qr, :] = jnp.dot(pe, v_ref[0, kr, :].astype(BF16),
                                            preferred_element_type=F32)
            return carry

        lax.fori_loop(0, seq // qb // blocks_per_trip, group, 0)

    rows_per = 2 * qb

    def merge(c, carry):
        rows = pl.ds(pl.multiple_of(c * rows_per, rows_per), rows_per)
        ms = [m_ref[p, rows, :] for p in range(n_pat)]
        m = functools.reduce(jnp.maximum, ms)
        num = den = None
        for p in range(n_pat):
            w = jnp.exp(ms[p] - m)
            num = w * acc_ref[p, rows, :] if num is None else num + w * acc_ref[p, rows, :]
            den = w * l_ref[p, rows, :] if den is None else den + w * l_ref[p, rows, :]
        o_ref[0, rows, :] = num / den
        return carry

    lax.fori_loop(0, seq // rows_per, merge, 0)


def attn_prompt(z3, bias):
    n, seq, _ = z3.shape
    h = ATT_HEADS
    kern = functools.partial(_attn_prompt_kernel, seq=seq)
    return pl.pallas_call(
        kern,
        grid=(n, h),
        in_specs=[pl.BlockSpec((1, seq, ATT_HEAD_DIM), lambda i, j: (i, 0, j)),
                  pl.BlockSpec((1, seq, ATT_HEAD_DIM), lambda i, j: (i, 0, h + j)),
                  pl.BlockSpec((1, seq, ATT_HEAD_DIM), lambda i, j: (i, 0, 2 * h + j)),
                  pl.BlockSpec((3, 2, 1, ATT_BLOCK, 2 * ATT_BLOCK), lambda i, j: (0, 0, j, 0, 0))],
        out_specs=pl.BlockSpec((1, seq, ATT_HEAD_DIM), lambda i, j: (i, 0, j)),
        out_shape=jax.ShapeDtypeStruct((n, seq, ATT_WIDTH), F32),
        scratch_shapes=[pltpu.VMEM((len(DILATED_PATTERNS), seq, ATT_HEAD_DIM), F32)] * 3,
        compiler_params=_params("parallel", "parallel"),
        name="attn_prompt",
    )(z3, z3, z3, bias)


def _attn_sample_kernel(q_ref, kn_ref, vn_ref, kc_ref, vc_ref, bc_ref, bn_ref, o_ref):
    t = q_ref.shape[1]
    e = ATT_HEAD_DIM
    slots = kc_ref.shape[2] // ATT_HEADS
    n_pat = len(DILATED_PATTERNS)
    pad = jnp.zeros((ATT_BLOCK - t, e), F32)
    nt = (((1,), (1,)), ((), ()))
    heads = range(ATT_HEADS)
    cols = [slice(h * e, (h + 1) * e) for h in heads]
    head_rows = [pl.ds(h, slots, stride=ATT_HEADS) for h in heads]

    sc, sn = [], []
    for h in heads:
        q = q_ref[0, :, cols[h]].astype(BF16)
        kn = jnp.concatenate([kn_ref[0, :, cols[h]], pad], axis=0).astype(BF16)
        kc = kc_ref[0, 0, head_rows[h], :].astype(BF16)
        sc.append(lax.dot_general(q, kc, nt, preferred_element_type=F32) * ATT_SCALE)
        sn.append(lax.dot_general(q, kn, nt, preferred_element_type=F32) * ATT_SCALE)
    pc, pn, ls = [], [], []
    for h in heads:
        scs = [sc[h] + bc_ref[p, h] for p in range(n_pat)]
        sns = [sn[h] + bn_ref[p, h] for p in range(n_pat)]
        m = None
        for s in scs + sns:
            mx = jnp.max(s, axis=1, keepdims=True)
            m = mx if m is None else jnp.maximum(m, mx)
        pch = sum(jnp.exp(s - m) for s in scs)
        pnh = sum(jnp.exp(s - m) for s in sns)
        ls.append(jnp.sum(pch, axis=1, keepdims=True) + jnp.sum(pnh, axis=1, keepdims=True))
        pc.append(pch.astype(BF16))
        pn.append(pnh.astype(BF16))
    for h in heads:
        vn = jnp.concatenate([vn_ref[0, :, cols[h]], pad], axis=0).astype(BF16)
        vc = vc_ref[0, 0, head_rows[h], :].astype(BF16)
        acc = (jnp.dot(pc[h], vc, preferred_element_type=F32)
               + jnp.dot(pn[h], vn, preferred_element_type=F32))
        o_ref[0, :, cols[h]] = acc / ls[h]


def attn_sample(zs3, cache_k, cache_v, layer, bias_c, bias_n):
    n, t, _ = zs3.shape
    h = ATT_HEADS
    e = ATT_HEAD_DIM
    w = cache_k.shape[2] // h
    return pl.pallas_call(
        _attn_sample_kernel,
        grid=(n,),
        in_specs=[pl.BlockSpec((1, t, ATT_WIDTH), lambda i: (i, 0, 0)),
                  pl.BlockSpec((1, t, ATT_WIDTH), lambda i: (i, 0, 1)),
                  pl.BlockSpec((1, t, ATT_WIDTH), lambda i: (i, 0, 2)),
                  pl.BlockSpec((1, 1, w * h, e), lambda i: (layer, i, 0, 0)),
                  pl.BlockSpec((1, 1, w * h, e), lambda i: (layer, i, 0, 0)),
                  pl.BlockSpec((3, h, t, w), lambda i: (0, 0, 0, 0)),
                  pl.BlockSpec((3, h, t, ATT_BLOCK), lambda i: (0, 0, 0, 0))],
        out_specs=pl.BlockSpec((1, t, ATT_WIDTH), lambda i: (i, 0, 0)),
        out_shape=jax.ShapeDtypeStruct((n, t, ATT_WIDTH), F32),
        compiler_params=_params("parallel"),
        name="attn_sample",
    )(zs3, zs3, zs3, cache_k, cache_v, bias_c, bias_n)


def _ssm_prompt_kernel(u_ref, bq_ref, a_ref, cq_ref, d_ref, wg_ref, bg_ref,
                       o_ref, hfin_ref, x_ref, h_ref, *, tb, pitch):
    step = pl.program_id(0)
    nq = SSM_QUARTERS
    qw = SSM_QW
    qc = SSM_QC

    @pl.when(step == 0)
    def _():
        h_ref[...] = jnp.zeros_like(h_ref)

    nl = 2 * qw // LANES
    hl = nl // 2

    for b in range(2):
        for k in range(nq):
            u = u_ref[b, :, k * qc:(k + 1) * qc].astype(BF16)
            x = jnp.dot(u, bq_ref[k], preferred_element_type=F32)
            for c in range(nl):
                x_ref[c, pl.ds((b * nq + k) * pitch, tb), :] = x[:, c * LANES:(c + 1) * LANES]

    a = [a_ref[:, c * LANES:(c + 1) * LANES] for c in range(nl)]

    def scan(t, h):
        rows = pl.ds(t, 2 * nq, stride=pitch)
        new = [None] * nl
        for c in range(hl):
            a_re, a_im, h_re, h_im = a[c], a[hl + c], h[c], h[hl + c]
            new[c] = a_re * h_re - a_im * h_im + x_ref[c, rows, :]
            new[hl + c] = a_re * h_im + a_im * h_re + x_ref[hl + c, rows, :]
        for c in range(nl):
            x_ref[c, rows, :] = new[c]
        return tuple(new)

    h0 = tuple(h_ref[:, c * LANES:(c + 1) * LANES] for c in range(nl))
    h_new = jnp.concatenate(lax.fori_loop(0, tb, scan, h0, unroll=8), axis=1)
    h_ref[...] = h_new

    for b in range(2):
        ys = []
        for k in range(nq):
            rows = pl.ds((b * nq + k) * pitch, tb)
            hs = jnp.concatenate([x_ref[c, rows, :] for c in range(nl)], axis=1).astype(BF16)
            ys.append(jnp.dot(hs, cq_ref[k], preferred_element_type=F32))
        y = jnp.concatenate(ys, axis=1) + d_ref[...] * u_ref[b]
        gate = jnp.dot(y.astype(BF16), wg_ref[...], preferred_element_type=F32) + bg_ref[...]
        o_ref[b] = y * jax.nn.sigmoid(gate)

    @pl.when(step == pl.num_programs(0) - 1)
    def _():
        hfin_ref[...] = h_new


def ssm_prompt(z3, layer, bq, a8, cq, d_skip, w_glu, b_glu, tb=512):
    n, seq, _ = z3.shape
    assert n == 2
    pitch = tb + 8
    ucol = 3 * ATT_WIDTH // SSM_WIDTH
    kern = functools.partial(_ssm_prompt_kernel, tb=tb, pitch=pitch)
    full = lambda *shape: pl.BlockSpec(shape, lambda s: (0,) * len(shape))
    return pl.pallas_call(
        kern,
        grid=(seq // tb,),
        in_specs=[pl.BlockSpec((n, tb, SSM_WIDTH), lambda s: (0, s, ucol))]
        + [_layer_spec(p, layer) for p in (bq, a8, cq, d_skip, w_glu, b_glu)],
        out_specs=[pl.BlockSpec((n, tb, SSM_WIDTH), lambda s: (0, s, 0)),
                   full(2 * SSM_QUARTERS, 2 * SSM_QW)],
        out_shape=[jax.ShapeDtypeStruct((n, seq, SSM_WIDTH), F32),
                   jax.ShapeDtypeStruct((2 * SSM_QUARTERS, 2 * SSM_QW), F32)],
        scratch_shapes=[pltpu.VMEM((2 * SSM_QW // LANES, 2 * SSM_QUARTERS * pitch, LANES), F32),
                        pltpu.VMEM((2 * SSM_QUARTERS, 2 * SSM_QW), F32)],
        compiler_params=_params("arbitrary"),
        name="ssm_prompt",
    )(z3, bq, a8, cq, d_skip, w_glu, b_glu)


def _ssm_sample_kernel(u_ref, hre_ref, him_ref, bq_ref, a_ref, cq_ref, d_ref, wg_ref, bg_ref,
                       o_ref, ore_ref, oim_ref, x_ref):
    n, t, _ = u_ref.shape
    nq = SSM_QUARTERS
    qw = SSM_QW
    qc = SSM_QC
    nl = 2 * qw // LANES
    hl = nl // 2
    u_all = u_ref[...].reshape(n * t, SSM_WIDTH)
    ys = []
    for k in range(nq):
        u = u_all[:, k * qc:(k + 1) * qc]
        x = jnp.dot(u, bq_ref[k], preferred_element_type=F32, precision=lax.Precision.HIGHEST)
        for c in range(nl):
            x_ref[c] = x[:, c * LANES:(c + 1) * LANES]
        h = ([hre_ref[:, k * qw + c * LANES:k * qw + (c + 1) * LANES] for c in range(hl)]
             + [him_ref[:, k * qw + c * LANES:k * qw + (c + 1) * LANES] for c in range(hl)])
        for s in range(t):
            rows = pl.ds(s, n, stride=t)
            new = [None] * nl
            for c in range(hl):
                a_re = a_ref[k:k + 1, c * LANES:(c + 1) * LANES]
                a_im = a_ref[k:k + 1, (hl + c) * LANES:(hl + c + 1) * LANES]
                new[c] = a_re * h[c] - a_im * h[hl + c] + x_ref[c, rows, :]
                new[hl + c] = a_re * h[hl + c] + a_im * h[c] + x_ref[hl + c, rows, :]
            for c in range(nl):
                x_ref[c, rows, :] = new[c]
            h = new
        for c in range(hl):
            ore_ref[:, k * qw + c * LANES:k * qw + (c + 1) * LANES] = h[c]
            oim_ref[:, k * qw + c * LANES:k * qw + (c + 1) * LANES] = h[hl + c]
        hs = jnp.concatenate([x_ref[c] for c in range(nl)], axis=1).astype(BF16)
        ys.append(jnp.dot(hs, cq_ref[k].astype(BF16), preferred_element_type=F32))
    y = jnp.concatenate(ys, axis=1) + d_ref[...] * u_all
    gate = jnp.dot(y.astype(BF16), wg_ref[...], preferred_element_type=F32) + bg_ref[...]
    o_ref[...] = (y * jax.nn.sigmoid(gate)).reshape(n, t, SSM_WIDTH)


def ssm_sample(zs3, layer, h_re, h_im, bq32, a8, cq32, d_skip, w_glu, b_glu):
    n, t, _ = zs3.shape
    ucol = 3 * ATT_WIDTH // SSM_WIDTH
    ns = SSM_GROUPS * SSM_STATE
    full = lambda *shape: pl.BlockSpec(shape, lambda s: (0,) * len(shape))
    return pl.pallas_call(
        _ssm_sample_kernel,
        grid=(1,),
        in_specs=[pl.BlockSpec((n, t, SSM_WIDTH), lambda s: (0, 0, ucol))]
        + [_layer_spec(p, layer) for p in (h_re, h_im, bq32, a8, cq32, d_skip, w_glu, b_glu)],
        out_specs=[full(n, t, SSM_WIDTH), full(n, ns), full(n, ns)],
        out_shape=[jax.ShapeDtypeStruct((n, t, SSM_WIDTH), F32),
                   jax.ShapeDtypeStruct((n, ns), F32),
                   jax.ShapeDtypeStruct((n, ns), F32)],
        scratch_shapes=[pltpu.VMEM((2 * SSM_QW // LANES, n * t, LANES), F32)],
        compiler_params=_params("arbitrary"),
        name="ssm_sample",
    )(zs3, h_re, h_im, bq32, a8, cq32, d_skip, w_glu, b_glu)


def _sgu_norm_v(v, g):
    gv = jax.nn.gelu(v)
    vc = gv - jnp.mean(gv, axis=-1, keepdims=True)
    return vc * lax.rsqrt(jnp.mean(vc * vc, axis=-1, keepdims=True) + EPS) * g


def _sgu_causal(w_ref):
    c = SGU_CHUNK
    keep = lax.broadcasted_iota(jnp.int32, (c, c), 0) >= lax.broadcasted_iota(jnp.int32, (c, c), 1)
    return [jnp.where(keep, w_ref[h], 0.0).astype(BF16) for h in range(SGU_HEADS)]


def _sgu_prompt_kernel(u_ref, v_ref, g_ref, w_ref, b_ref, o_ref):
    c = SGU_CHUNK
    e = SGU_HEAD_DIM
    rows = u_ref.shape[1]
    w = _sgu_causal(w_ref)
    gu = jax.nn.gelu(u_ref[0])
    gv = _sgu_norm_v(v_ref[0], g_ref[...]).astype(BF16)
    for ci in range(rows // c):
        for h in range(SGU_HEADS):
            mixed = jnp.dot(w[h], gv[ci * c:(ci + 1) * c, h * e:(h + 1) * e],
                            preferred_element_type=F32) + b_ref[h]
            o_ref[0, ci * c:(ci + 1) * c, h * e:(h + 1) * e] = (
                gu[ci * c:(ci + 1) * c, h * e:(h + 1) * e] * mixed)


def sgu_prompt(z3, layer, g, w_sp, b_rows, rows=256):
    n, seq, _ = z3.shape
    ucol = (3 * ATT_WIDTH + SSM_WIDTH) // SGU_WIDTH
    return pl.pallas_call(
        _sgu_prompt_kernel,
        grid=(n, seq // rows),
        in_specs=[pl.BlockSpec((1, rows, SGU_WIDTH), lambda i, j: (i, j, ucol)),
                  pl.BlockSpec((1, rows, SGU_WIDTH), lambda i, j: (i, j, ucol + 1)),
                  _layer_spec(g, layer), _layer_spec(w_sp, layer), _layer_spec(b_rows, layer)],
        out_specs=pl.BlockSpec((1, rows, SGU_WIDTH), lambda i, j: (i, j, 0)),
        out_shape=jax.ShapeDtypeStruct((n, seq, SGU_WIDTH), F32),
        compiler_params=_params("parallel", "parallel"),
        name="sgu_prompt",
    )(z3, z3, g, w_sp, b_rows)


def _sgu_sample_kernel(u_ref, v_ref, g_ref, w_ref, b_ref, o_ref, gv_ref):
    c = SGU_CHUNK
    e = SGU_HEAD_DIM
    t = u_ref.shape[1]
    w = _sgu_causal(w_ref)
    gu = jax.nn.gelu(u_ref[0])
    gv = _sgu_norm_v(v_ref[0], g_ref[...])
    gv_ref[0] = gv
    gvp = jnp.concatenate([gv, jnp.zeros((c - t, SGU_WIDTH), F32)], axis=0).astype(BF16)
    for h in range(SGU_HEADS):
        mixed = jnp.dot(w[h], gvp[:, h * e:(h + 1) * e], preferred_element_type=F32) + b_ref[h]
        o_ref[0, :, h * e:(h + 1) * e] = gu[:, h * e:(h + 1) * e] * mixed[:t]


def sgu_sample(zs3, layer, g, w_sp, b_rows):
    n, t, _ = zs3.shape
    ucol = (3 * ATT_WIDTH + SSM_WIDTH) // SGU_WIDTH
    return pl.pallas_call(
        _sgu_sample_kernel,
        grid=(n,),
        in_specs=[pl.BlockSpec((1, t, SGU_WIDTH), lambda i: (i, 0, ucol)),
                  pl.BlockSpec((1, t, SGU_WIDTH), lambda i: (i, 0, ucol + 1)),
                  _layer_spec(g, layer), _layer_spec(w_sp, layer), _layer_spec(b_rows, layer)],
        out_specs=[pl.BlockSpec((1, t, SGU_WIDTH), lambda i: (i, 0, 0)),
                   pl.BlockSpec((1, t, SGU_WIDTH), lambda i: (i, 0, 0))],
        out_shape=[jax.ShapeDtypeStruct((n, t, SGU_WIDTH), F32),
                   jax.ShapeDtypeStruct((n, t, SGU_WIDTH), F32)],
        compiler_params=_params("parallel"),
        name="sgu_sample",
    )(zs3, zs3, g, w_sp, b_rows)


def _out_proj_kernel(att_ref, ssm_ref, sgu_ref, x_ref, w_ref, gm_ref, gp_ref, o_ref):
    a0 = ATT_WIDTH
    a1 = ATT_WIDTH + SSM_WIDTH
    a = _rms(att_ref[...], gm_ref[:, :a0]).astype(BF16)
    s = _rms(ssm_ref[...], gm_ref[:, a0:a1]).astype(BF16)
    c = _rms(sgu_ref[...], gm_ref[:, a1:]).astype(BF16)
    y = (jnp.dot(a, w_ref[:a0, :], preferred_element_type=F32)
         + jnp.dot(s, w_ref[a0:a1, :], preferred_element_type=F32)
         + jnp.dot(c, w_ref[a1:, :], preferred_element_type=F32))
    o_ref[...] = x_ref[...] + _rms(y, gp_ref[...])


def out_proj(att, ssm, sgu, x, w, layer, gm, gp, tm):
    m, dm = x.shape
    row = lambda width: pl.BlockSpec((tm, width), lambda i: (i, 0))
    return pl.pallas_call(
        _out_proj_kernel,
        grid=(m // tm,),
        in_specs=[row(ATT_WIDTH), row(SSM_WIDTH), row(SGU_WIDTH), row(dm),
                  pl.BlockSpec(w.shape, lambda i: (0, 0)), _layer_spec(gm, layer), _layer_spec(gp, layer)],
        out_specs=row(dm),
        out_shape=jax.ShapeDtypeStruct((m, dm), F32),
        compiler_params=_params("parallel"),
        name="out_proj",
    )(att, ssm, sgu, x, w, gm, gp)


def _ffn_kernel(x_ref, gpre_ref, wg_ref, wu_ref, wd_ref, gpost_ref, *rest, n_cast):
    w32_refs, o_ref = rest[:n_cast], rest[n_cast]
    w16_refs, h_ref = rest[n_cast + 1:2 * n_cast + 1], rest[-1]
    j = pl.program_id(1)

    @pl.when(j == 0)
    def _():
        h_ref[...] = _rms(x_ref[...], gpre_ref[...]).astype(BF16)
        o_ref[...] = jnp.zeros_like(o_ref)

    h = h_ref[...]
    gate = jnp.dot(h, wg_ref[...], preferred_element_type=F32)
    up = jnp.dot(h, wu_ref[...], preferred_element_type=F32)
    act = (jax.nn.silu(gate) * up).astype(BF16)
    o_ref[...] += jnp.dot(act, wd_ref[...], preferred_element_type=F32)

    for src_ref, dst_ref in zip(w32_refs, w16_refs):
        dst_ref[...] = src_ref[...].astype(BF16)

    @pl.when(j == pl.num_programs(1) - 1)
    def _():
        o_ref[...] = x_ref[...] + _rms(o_ref[...], gpost_ref[...])


def ffn(x, gpre, wg, wu, wd, layer, gpost, tm, tf, cast=()):
    m, dm = x.shape
    f = wg.shape[1]
    gi, gj = m // tm, f // tf
    cb = CAST_BLOCK
    cast_in, cast_out, cast_shape = [], [], []
    for w32, lyr, rows_follow_i in cast:
        _, r, c = w32.shape
        nr, nc = r // cb, c // cb
        assert r % cb == 0 and c % cb == 0
        if rows_follow_i:
            assert nr == gi and nc <= gj
            imap = lambda i, j, nc=nc: (i, jnp.minimum(j, nc - 1))
        else:
            assert nc == gi and nr <= gj
            imap = lambda i, j, nr=nr: (jnp.minimum(j, nr - 1), i)
        cast_in.append(pl.BlockSpec((None, cb, cb), lambda i, j, imap=imap, lyr=lyr: (lyr,) + imap(i, j)))
        cast_out.append(pl.BlockSpec((cb, cb), imap))
        cast_shape.append(jax.ShapeDtypeStruct((r, c), BF16))
    kern = functools.partial(_ffn_kernel, n_cast=len(cast))
    res = pl.pallas_call(
        kern,
        grid=(gi, gj),
        in_specs=[pl.BlockSpec((tm, dm), lambda i, j: (i, 0)),
                  _layer_spec(gpre, layer),
                  pl.BlockSpec((dm, tf), lambda i, j: (0, j)),
                  pl.BlockSpec((dm, tf), lambda i, j: (0, j)),
                  pl.BlockSpec((tf, dm), lambda i, j: (j, 0)),
                  _layer_spec(gpost, layer)] + cast_in,
        out_specs=[pl.BlockSpec((tm, dm), lambda i, j: (i, 0))] + cast_out,
        out_shape=[jax.ShapeDtypeStruct((m, dm), F32)] + cast_shape,
        scratch_shapes=[pltpu.VMEM((tm, dm), BF16)],
        compiler_params=_params("arbitrary", "arbitrary"),
        name="ffn_cast" if cast else "ffn",
    )(x, gpre, wg, wu, wd, gpost, *[w for w, _, _ in cast])
    return res[0], res[1:]


def _t5_bucket(dist):
    max_exact = N_REL_BUCKETS // 2
    df = jnp.maximum(dist, 1).astype(F32)
    large = max_exact + (jnp.log(df / max_exact) / math.log(REL_MAX_DIST / max_exact)
                         * (N_REL_BUCKETS - max_exact)).astype(jnp.int32)
    large = jnp.minimum(large, N_REL_BUCKETS - 1)
    return jnp.where(dist < max_exact, dist, large)


def _strided_bias(rel_bias, dilation, n_steps):
    dist = jnp.arange(n_steps + 1, dtype=jnp.int32) * dilation
    return rel_bias[_t5_bucket(dist)].astype(F32)


def _bias_lookup(sb, steps, ok):
    onehot = (steps[..., None] == jnp.arange(sb.shape[0])).astype(F32)
    vals = jnp.einsum('...j,jh->h...', onehot, sb, precision=lax.Precision.HIGHEST)
    return jnp.where(ok[None], vals, NEG_INF)


def _prompt_bias_tables(rel_bias):
    qb = ATT_BLOCK
    qi = jnp.arange(qb)[:, None]
    kj = jnp.arange(2 * qb)[None, :]
    tables = []
    for window, d in DILATED_PATTERNS:
        nk = window // d
        sb = _strided_bias(rel_bias, d, nk)
        delta = qb + qi - kj
        normal = _bias_lookup(sb, jnp.clip(delta, 0, nk), (delta >= 0) & (delta <= nk))
        delta_f = qi - kj
        first = _bias_lookup(sb, jnp.clip(delta_f, 0, nk), (delta_f >= 0) & (delta_f <= nk))
        tables.append(jnp.stack([normal, first]))
    return jnp.stack(tables).astype(F32)


def _sample_bias_tables(rel_bias, w, t):
    tq = jnp.arange(t)[:, None]
    cache_delta = w + tq - jnp.arange(w)[None, :]
    new_delta = tq - jnp.arange(ATT_BLOCK)[None, :]
    out_c, out_n = [], []
    for window, d in DILATED_PATTERNS:
        nk = window // d
        sb = _strided_bias(rel_bias, d, nk)

        def table(delta, extra_ok):
            ok = (delta >= 0) & (delta % d == 0) & (delta <= nk * d) & extra_ok
            return _bias_lookup(sb, jnp.clip(delta // d, 0, nk), ok)

        out_c.append(table(cache_delta, True))
        out_n.append(table(new_delta, jnp.arange(ATT_BLOCK)[None, :] < t))
    return jnp.stack(out_c).astype(F32), jnp.stack(out_n).astype(F32)


def _ssm_tables(lam_re, lam_im, log_step, b_re, b_im, c_re, c_im):
    g, p, cg = SSM_GROUPS, SSM_STATE, SSM_GROUP
    nq = SSM_QUARTERS
    gq = g // nq
    step = jnp.exp(log_step)[:, None]
    mag = jnp.exp(lam_re * step)
    a_re = mag * jnp.cos(lam_im * step)
    a_im = mag * jnp.sin(lam_im * step)
    den = lam_re * lam_re + lam_im * lam_im
    coef_re = ((a_re - 1.0) * lam_re + a_im * lam_im) / den
    coef_im = (a_im * lam_re - (a_re - 1.0) * lam_im) / den
    bb_re = coef_re[..., None] * b_re - coef_im[..., None] * b_im
    bb_im = coef_re[..., None] * b_im + coef_im[..., None] * b_re
    eye = jnp.eye(gq, dtype=F32)

    def in_mat(bb):
        bb = bb.reshape(nq, gq, p, cg)
        return jnp.einsum('kgpc,gh->kgchp', bb, eye).reshape(nq, gq * cg, gq * p)

    def out_mat(c):
        c = c.reshape(nq, gq, cg, p)
        return jnp.einsum('kgcp,gh->kgphc', c, eye).reshape(nq, gq * p, gq * cg)

    bq = jnp.concatenate([in_mat(bb_re), in_mat(bb_im)], axis=2)
    cq = jnp.concatenate([out_mat(c_re), -out_mat(c_im)], axis=1)
    aq = jnp.concatenate([a_re.reshape(nq, gq * p), a_im.reshape(nq, gq * p)], axis=1)
    a8 = jnp.concatenate([aq, aq], axis=0)
    return bq, cq, a8


def _rows(v):
    return v.reshape(v.shape[0], 1, -1)


def kernel(x_prompt, x_sample, cache_attn_k, cache_attn_v, state_ssm_re, state_ssm_im, rel_bias, w_in, w_out, g_pre_mix, g_post_mix, g_mix_out, ssm_lam_re, ssm_lam_im, ssm_log_step, ssm_b_re, ssm_b_im, ssm_c_re, ssm_c_im, ssm_d, ssm_w_glu, ssm_b_glu, sgu_g, sgu_w, sgu_b, g_pre_ffn, g_post_ffn, w_gate, w_up, w_down):
    nb, seq, dm = x_prompt.shape
    ns, ts, _ = x_sample.shape
    wbuf = cache_attn_k.shape[2]
    n_keep = min(DILATED_PATTERNS[-1][0], seq)
    h, e = ATT_HEADS, ATT_HEAD_DIM

    bias_p = _prompt_bias_tables(rel_bias)
    bias_c, bias_n = _sample_bias_tables(rel_bias, wbuf, ts)
    cache_k = cache_attn_k.reshape(DEPTH, ns, wbuf * h, e)
    cache_v = cache_attn_v.reshape(DEPTH, ns, wbuf * h, e)
    k_buf = jnp.zeros((DEPTH, nb, n_keep * h, e), F32)
    v_buf = jnp.zeros((DEPTH, nb, n_keep * h, e), F32)
    st_re = state_ssm_re.reshape(DEPTH, ns, SSM_GROUPS * SSM_STATE)
    st_im = state_ssm_im.reshape(DEPTH, ns, SSM_GROUPS * SSM_STATE)

    xp = x_prompt.reshape(nb * seq, dm)
    xs = x_sample.reshape(ns * ts, dm)
    outs = {k: [] for k in ('rp', 'ip', 'ks', 'vs', 'rs', 'is', 'us')}

    big = (w_in, w_out, w_gate, w_up, w_down)
    big16 = tuple(w[0].astype(BF16) for w in big)
    w_glu16 = ssm_w_glu.astype(BF16)
    bq, cq, a8 = jax.vmap(_ssm_tables)(ssm_lam_re, ssm_lam_im, ssm_log_step,
                                       ssm_b_re, ssm_b_im, ssm_c_re, ssm_c_im)
    bq16, cq16 = bq.astype(BF16), cq.astype(BF16)
    sgu_b_rows = jnp.broadcast_to(sgu_b[..., None], (DEPTH, SGU_HEADS, SGU_CHUNK, SGU_HEAD_DIM))
    g_pre, g_post, g_mix = _rows(g_pre_mix), _rows(g_post_mix), _rows(g_mix_out)
    g_pre_f, g_post_f = _rows(g_pre_ffn), _rows(g_post_ffn)
    d_skip, b_glu, g_sgu = _rows(ssm_d), _rows(ssm_b_glu), _rows(sgu_g)

    for l in range(DEPTH):
        w_in16, w_out16, w_gate16, w_up16, w_down16 = big16
        cast = [(w, l + 1, w is not w_down) for w in big] if l + 1 < DEPTH else []

        z, k_buf, v_buf = in_proj(xp, g_pre, w_in16, l, tm=1024, tn=768,
                                  kv=(seq, n_keep, k_buf, v_buf))
        z = z.reshape(nb, seq, IN_COLS)
        o_att = attn_prompt(z, bias_p)
        o_ssm, h_fin = ssm_prompt(z, l, bq16, a8, cq16, d_skip, w_glu16, b_glu)
        o_sgu = sgu_prompt(z, l, g_sgu, sgu_w, sgu_b_rows)
        xp = out_proj(o_att.reshape(nb * seq, ATT_WIDTH), o_ssm.reshape(nb * seq, SSM_WIDTH),
                      o_sgu.reshape(nb * seq, SGU_WIDTH), xp, w_out16, l, g_mix, g_post, tm=512)
        xp, next16 = ffn(xp, g_pre_f, w_gate16, w_up16, w_down16, l, g_post_f, tm=1024, tf=256,
                         cast=cast)
        hq = h_fin.reshape(nb, SSM_QUARTERS, 2, SSM_GROUPS // SSM_QUARTERS, SSM_STATE)
        outs['rp'].append(hq[:, :, 0].reshape(nb, SSM_GROUPS, SSM_STATE))
        outs['ip'].append(hq[:, :, 1].reshape(nb, SSM_GROUPS, SSM_STATE))

        zs = in_proj(xs, g_pre, w_in16, l, tm=ns * ts, tn=768).reshape(ns, ts, IN_COLS)
        s_att = attn_sample(zs, cache_k, cache_v, l, bias_c, bias_n)
        s_ssm, s_re, s_im = ssm_sample(zs, l, st_re, st_im, bq, a8, cq, d_skip, w_glu16, b_glu)
        s_sgu, s_gv = sgu_sample(zs, l, g_sgu, sgu_w, sgu_b_rows)
        xs = out_proj(s_att.reshape(ns * ts, ATT_WIDTH), s_ssm.reshape(ns * ts, SSM_WIDTH),
                      s_sgu.reshape(ns * ts, SGU_WIDTH), xs, w_out16, l, g_mix, g_post, tm=ns * ts)
        xs, _ = ffn(xs, g_pre_f, w_gate16, w_up16, w_down16, l, g_post_f, tm=ns * ts, tf=512)
        big16 = tuple(next16) or big16
        outs['ks'].append(zs[:, :, ATT_WIDTH:2 * ATT_WIDTH].reshape(ns, ts, h, e))
        outs['vs'].append(zs[:, :, 2 * ATT_WIDTH:3 * ATT_WIDTH].reshape(ns, ts, h, e))
        outs['rs'].append(s_re.reshape(ns, SSM_GROUPS, SSM_STATE))
        outs['is'].append(s_im.reshape(ns, SSM_GROUPS, SSM_STATE))
        outs['us'].append(s_gv)

    st = lambda key: jnp.stack(outs[key])
    kv_shape = (DEPTH, nb, n_keep, h, e)
    return (xp.reshape(nb, seq, dm), xs.reshape(ns, ts, dm),
            k_buf.reshape(kv_shape), v_buf.reshape(kv_shape), st('rp'), st('ip'),
            st('ks'), st('vs'), st('rs'), st('is'), st('us'))
```

```python
import functools
import math

import jax
import jax.numpy as jnp
from jax import lax
from jax.experimental import pallas as pl
from jax.experimental.pallas import tpu as pltpu

F32 = jnp.float32
BF16 = jnp.bfloat16

D_MODEL = 2048
DEPTH = 4
ATT_HEADS = 8
ATT_HEAD_DIM = 128
ATT_WIDTH = ATT_HEADS * ATT_HEAD_DIM
DILATED_PATTERNS = ((128, 1), (512, 4), (2048, 16))
ATT_BLOCK = 128
ATT_GROUP = (8, 8, 4)
ATT_SCALE = ATT_HEAD_DIM ** -0.5
N_REL_BUCKETS = 32
REL_MAX_DIST = 2048
SSM_WIDTH = 512
SSM_GROUP = 16
SSM_GROUPS = SSM_WIDTH // SSM_GROUP
SSM_STATE = 64
SSM_QUARTERS = 4
SSM_QW = SSM_GROUPS * SSM_STATE // SSM_QUARTERS
SSM_QC = SSM_WIDTH // SSM_QUARTERS
SGU_WIDTH = 512
SGU_HEADS = 4
SGU_HEAD_DIM = SGU_WIDTH // SGU_HEADS
SGU_CHUNK = 128
IN_COLS = 3 * ATT_WIDTH + SSM_WIDTH + 2 * SGU_WIDTH
FFN_HIDDEN = 5632
EPS = 1e-6
NEG_INF = -1e30
LANES = 128
CAST_BLOCK = 256
FFN_EDGE_ROWS = 256

VMEM_LIMIT_BYTES = 56 * 1024 * 1024


def _params(*sem):
    return pltpu.CompilerParams(dimension_semantics=sem, vmem_limit_bytes=VMEM_LIMIT_BYTES)


def _layer_spec(arr, layer):
    zeros = (0,) * (arr.ndim - 1)
    return pl.BlockSpec((None,) + arr.shape[1:], lambda *_: (layer,) + zeros)


def _rms(x, g):
    return x * lax.rsqrt(jnp.mean(x * x, axis=-1, keepdims=True) + EPS) * g


def _in_proj_kernel(x_ref, g_ref, w_ref, *rest, keep_from, tiles_per_seq):
    h_ref = rest[-1]
    outs = rest[-4:-1] if len(rest) > 2 else rest[:1]
    z_ref = outs[0]
    i = pl.program_id(0)
    j = pl.program_id(1)

    tm, tn = z_ref.shape
    chunk = min(tm, FFN_EDGE_ROWS)

    @pl.when(j == 0)
    def _():
        for r in range(0, tm, chunk):
            rows = pl.ds(r, chunk)
            h = _rms(x_ref[rows, :], g_ref[...]).astype(BF16)
            h_ref[rows, :] = h
            z_ref[rows, :] = jnp.dot(h, w_ref[...], preferred_element_type=F32)

    @pl.when(j > 0)
    def _():
        _in_proj_tile(i, j, h_ref, w_ref, outs, keep_from, tiles_per_seq)


def _in_proj_tile(i, j, h_ref, w_ref, outs, keep_from, tiles_per_seq):
    z_ref = outs[0]
    res = jnp.dot(h_ref[...], w_ref[...], preferred_element_type=F32)
    z_ref[...] = res

    if len(outs) > 1:
        tm, tn = res.shape
        keep = (i % tiles_per_seq) >= keep_from
        by_tile = {}
        for buf_ref, col0 in ((outs[1], ATT_WIDTH), (outs[2], 2 * ATT_WIDTH)):
            for head in range(ATT_HEADS):
                col = col0 + head * ATT_HEAD_DIM
                by_tile.setdefault(col // tn, []).append((buf_ref, head, col % tn))
        for tile, heads in by_tile.items():
            @pl.when(keep & (j == tile))
            def _(heads=heads):
                for buf_ref, head, off in heads:
                    rows = pl.ds(head, tm, stride=ATT_HEADS)
                    buf_ref[0, 0, rows, :] = res[:, off:off + ATT_HEAD_DIM]


def in_proj(x, g, w, layer, tm, tn, kv=None):
    m, k = x.shape
    n = w.shape[1]
    assert tn % ATT_HEAD_DIM == 0
    in_specs = [pl.BlockSpec((tm, k), lambda i, j: (i, 0)),
                _layer_spec(g, layer),
                pl.BlockSpec((k, tn), lambda i, j: (0, j))]
    z_spec = pl.BlockSpec((tm, tn), lambda i, j: (i, j))
    z_shape = jax.ShapeDtypeStruct((m, n), F32)
    scratch = [pltpu.VMEM((tm, k), BF16)]
    if kv is None:
        kern = functools.partial(_in_proj_kernel, keep_from=0, tiles_per_seq=1)
        return pl.pallas_call(
            kern, grid=(m // tm, n // tn), in_specs=in_specs, out_specs=z_spec, out_shape=z_shape,
            scratch_shapes=scratch, compiler_params=_params("parallel", "arbitrary"),
            name="in_proj",
        )(x, g, w)
    seq, n_keep, k_buf, v_buf = kv
    tiles_per_seq = seq // tm
    keep_from = (seq - n_keep) // tm
    assert seq % tm == 0 and (seq - n_keep) % tm == 0
    assert tn <= ATT_WIDTH

    def buf_map(i, j):
        return (layer, i // tiles_per_seq, jnp.maximum(i % tiles_per_seq - keep_from, 0), 0)

    buf_spec = pl.BlockSpec((1, 1, tm * ATT_HEADS, ATT_HEAD_DIM), buf_map)
    any_spec = pl.BlockSpec(memory_space=pl.ANY)
    kern = functools.partial(_in_proj_kernel, keep_from=keep_from, tiles_per_seq=tiles_per_seq)
    return pl.pallas_call(
        kern, grid=(m // tm, n // tn),
        in_specs=in_specs + [any_spec, any_spec],
        out_specs=[z_spec, buf_spec, buf_spec],
        out_shape=[z_shape, jax.ShapeDtypeStruct(k_buf.shape, F32),
                   jax.ShapeDtypeStruct(v_buf.shape, F32)],
        input_output_aliases={3: 1, 4: 2},
        scratch_shapes=scratch, compiler_params=_params("arbitrary", "arbitrary"),
        name="in_proj_kv",
    )(x, g, w, k_buf, v_buf)


def _attn_prompt_kernel(q_ref, k_ref, v_ref, b_ref, o_ref, acc_ref, m_ref, l_ref, *, seq):
    qb = ATT_BLOCK
    n_pat = len(DILATED_PATTERNS)

    for p, (_, d) in enumerate(DILATED_PATTERNS):
        shift = int(math.log2(d))
        whole = seq // d == 2 * qb
        blocks_per_trip = ATT_GROUP[p]
        units = blocks_per_trip // 2 if whole else blocks_per_trip

        def group(g, carry, p=p, d=d, shift=shift, whole=whole, units=units):
            rows, biases = [], []
            for u in range(units):
                i = g * units + u
                if whole:
                    rows.append((pl.ds(i, 2 * qb, stride=d), pl.ds(i, 2 * qb, stride=d)))
                    biases.append(None)
                    continue
                r = i & (d - 1)
                b = i >> shift
                q_start = r + b * (qb * d)
                k_start = r + jnp.maximum(b - 1, 0) * (qb * d)
                if d == 1:
                    rows.append((pl.ds(pl.multiple_of(q_start, qb), qb),
                                 pl.ds(pl.multiple_of(k_start, qb), 2 * qb)))
                else:
                    rows.append((pl.ds(q_start, qb, stride=d), pl.ds(k_start, 2 * qb, stride=d)))
                biases.append((b == 0).astype(jnp.int32))
            nt = (((1,), (1,)), ((), ()))
            scores = [lax.dot_general(q_ref[0, qr, :].astype(BF16), k_ref[0, kr, :].astype(BF16), nt,
                                      preferred_element_type=F32) for qr, kr in rows]
            probs = []
            for (qr, _), first, s in zip(rows, biases, scores):
                if whole:
                    bias = jnp.concatenate([b_ref[p, 1, 0], b_ref[p, 0, 0]], axis=0)
                else:
                    bias = b_ref[p, first, 0]
                s = s * ATT_SCALE + bias
                m_blk = jnp.max(s, axis=1, keepdims=True)
                pe = jnp.exp(s - m_blk)
                wide = (s.shape[0], ATT_HEAD_DIM)
                m_ref[p, qr, :] = jnp.broadcast_to(m_blk, wide)
                l_ref[p, qr, :] = jnp.broadcast_to(jnp.sum(pe, axis=1, keepdims=True), wide)
                probs.append(pe.astype(BF16))
            for (qr, kr), pe in zip(rows, probs):
                acc_ref[p, qr, :] = jnp.dot(pe, v_ref[0, kr, :].astype(BF16),
                                            preferred_element_type=F32)
            return carry

        lax.fori_loop(0, seq // qb // blocks_per_trip, group, 0)

    rows_per = 2 * qb

    def merge(c, carry):
        rows = pl.ds(pl.multiple_of(c * rows_per, rows_per), rows_per)
        ms = [m_ref[p, rows, :] for p in range(n_pat)]
        m = functools.reduce(jnp.maximum, ms)
        num = den = None
        for p in range(n_pat):
            w = jnp.exp(ms[p] - m)
            num = w * acc_ref[p, rows, :] if num is None else num + w * acc_ref[p, rows, :]
            den = w * l_ref[p, rows, :] if den is None else den + w * l_ref[p, rows, :]
        o_ref[0, rows, :] = num / den
        return carry

    lax.fori_loop(0, seq // rows_per, merge, 0)


def attn_prompt(z3, bias):
    n, seq, _ = z3.shape
    h = ATT_HEADS
    kern = functools.partial(_attn_prompt_kernel, seq=seq)
    return pl.pallas_call(
        kern,
        grid=(n, h),
        in_specs=[pl.BlockSpec((1, seq, ATT_HEAD_DIM), lambda i, j: (i, 0, j)),
                  pl.BlockSpec((1, seq, ATT_HEAD_DIM), lambda i, j: (i, 0, h + j)),
                  pl.BlockSpec((1, seq, ATT_HEAD_DIM), lambda i, j: (i, 0, 2 * h + j)),
                  pl.BlockSpec((3, 2, 1, ATT_BLOCK, 2 * ATT_BLOCK), lambda i, j: (0, 0, j, 0, 0))],
        out_specs=pl.BlockSpec((1, seq, ATT_HEAD_DIM), lambda i, j: (i, 0, j)),
        out_shape=jax.ShapeDtypeStruct((n, seq, ATT_WIDTH), F32),
        scratch_shapes=[pltpu.VMEM((len(DILATED_PATTERNS), seq, ATT_HEAD_DIM), F32)] * 3,
        compiler_params=_params("parallel", "parallel"),
        name="attn_prompt",
    )(z3, z3, z3, bias)


def _attn_sample_kernel(q_ref, kn_ref, vn_ref, kc_ref, vc_ref, bc_ref, bn_ref, o_ref):
    t = q_ref.shape[1]
    e = ATT_HEAD_DIM
    slots = kc_ref.shape[2] // ATT_HEADS
    n_pat = len(DILATED_PATTERNS)
    pad = jnp.zeros((ATT_BLOCK - t, e), F32)
    nt = (((1,), (1,)), ((), ()))
    heads = range(ATT_HEADS)
    cols = [slice(h * e, (h + 1) * e) for h in heads]
    head_rows = [pl.ds(h, slots, stride=ATT_HEADS) for h in heads]

    sc, sn = [], []
    for h in heads:
        q = q_ref[0, :, cols[h]].astype(BF16)
        kn = jnp.concatenate([kn_ref[0, :, cols[h]], pad], axis=0).astype(BF16)
        kc = kc_ref[0, 0, head_rows[h], :].astype(BF16)
        sc.append(lax.dot_general(q, kc, nt, preferred_element_type=F32) * ATT_SCALE)
        sn.append(lax.dot_general(q, kn, nt, preferred_element_type=F32) * ATT_SCALE)
    pc, pn, ls = [], [], []
    for h in heads:
        scs = [sc[h] + bc_ref[p, h] for p in range(n_pat)]
        sns = [sn[h] + bn_ref[p, h] for p in range(n_pat)]
        m = None
        for s in scs + sns:
            mx = jnp.max(s, axis=1, keepdims=True)
            m = mx if m is None else jnp.maximum(m, mx)
        pch = sum(jnp.exp(s - m) for s in scs)
        pnh = sum(jnp.exp(s - m) for s in sns)
        ls.append(jnp.sum(pch, axis=1, keepdims=True) + jnp.sum(pnh, axis=1, keepdims=True))
        pc.append(pch.astype(BF16))
        pn.append(pnh.astype(BF16))
    for h in heads:
        vn = jnp.concatenate([vn_ref[0, :, cols[h]], pad], axis=0).astype(BF16)
        vc = vc_ref[0, 0, head_rows[h], :].astype(BF16)
        acc = (jnp.dot(pc[h], vc, preferred_element_type=F32)
               + jnp.dot(pn[h], vn, preferred_element_type=F32))
        o_ref[0, :, cols[h]] = acc / ls[h]


def attn_sample(zs3, cache_k, cache_v, layer, bias_c, bias_n):
    n, t, _ = zs3.shape
    h = ATT_HEADS
    e = ATT_HEAD_DIM
    w = cache_k.shape[2] // h
    return pl.pallas_call(
        _attn_sample_kernel,
        grid=(n,),
        in_specs=[pl.BlockSpec((1, t, ATT_WIDTH), lambda i: (i, 0, 0)),
                  pl.BlockSpec((1, t, ATT_WIDTH), lambda i: (i, 0, 1)),
                  pl.BlockSpec((1, t, ATT_WIDTH), lambda i: (i, 0, 2)),
                  pl.BlockSpec((1, 1, w * h, e), lambda i: (layer, i, 0, 0)),
                  pl.BlockSpec((1, 1, w * h, e), lambda i: (layer, i, 0, 0)),
                  pl.BlockSpec((3, h, t, w), lambda i: (0, 0, 0, 0)),
                  pl.BlockSpec((3, h, t, ATT_BLOCK), lambda i: (0, 0, 0, 0))],
        out_specs=pl.BlockSpec((1, t, ATT_WIDTH), lambda i: (i, 0, 0)),
        out_shape=jax.ShapeDtypeStruct((n, t, ATT_WIDTH), F32),
        compiler_params=_params("parallel"),
        name="attn_sample",
    )(zs3, zs3, zs3, cache_k, cache_v, bias_c, bias_n)


def _ssm_prompt_kernel(u_ref, bq_ref, a_ref, cq_ref, d_ref, wg_ref, bg_ref,
                       o_ref, hfin_ref, x_ref, h_ref, *, tb, pitch):
    step = pl.program_id(0)
    nq = SSM_QUARTERS
    qw = SSM_QW
    qc = SSM_QC

    @pl.when(step == 0)
    def _():
        h_ref[...] = jnp.zeros_like(h_ref)

    nl = 2 * qw // LANES
    hl = nl // 2

    for b in range(2):
        for k in range(nq):
            u = u_ref[b, :, k * qc:(k + 1) * qc].astype(BF16)
            x = jnp.dot(u, bq_ref[k], preferred_element_type=F32)
            for c in range(nl):
                x_ref[c, pl.ds((b * nq + k) * pitch, tb), :] = x[:, c * LANES:(c + 1) * LANES]

    a = [a_ref[:, c * LANES:(c + 1) * LANES] for c in range(nl)]

    def scan(t, h):
        rows = pl.ds(t, 2 * nq, stride=pitch)
        new = [None] * nl
        for c in range(hl):
            a_re, a_im, h_re, h_im = a[c], a[hl + c], h[c], h[hl + c]
            new[c] = a_re * h_re - a_im * h_im + x_ref[c, rows, :]
            new[hl + c] = a_re * h_im + a_im * h_re + x_ref[hl + c, rows, :]
        for c in range(nl):
            x_ref[c, rows, :] = new[c]
        return tuple(new)

    h0 = tuple(h_ref[:, c * LANES:(c + 1) * LANES] for c in range(nl))
    h_new = jnp.concatenate(lax.fori_loop(0, tb, scan, h0, unroll=8), axis=1)
    h_ref[...] = h_new

    for b in range(2):
        ys = []
        for k in range(nq):
            rows = pl.ds((b * nq + k) * pitch, tb)
            hs = jnp.concatenate([x_ref[c, rows, :] for c in range(nl)], axis=1).astype(BF16)
            ys.append(jnp.dot(hs, cq_ref[k], preferred_element_type=F32))
        y = jnp.concatenate(ys, axis=1) + d_ref[...] * u_ref[b]
        gate = jnp.dot(y.astype(BF16), wg_ref[...], preferred_element_type=F32) + bg_ref[...]
        o_ref[b] = y * jax.nn.sigmoid(gate)

    @pl.when(step == pl.num_programs(0) - 1)
    def _():
        hfin_ref[...] = h_new


def ssm_prompt(z3, layer, bq, a8, cq, d_skip, w_glu, b_glu, tb=512):
    n, seq, _ = z3.shape
    assert n == 2
    pitch = tb + 8
    ucol = 3 * ATT_WIDTH // SSM_WIDTH
    kern = functools.partial(_ssm_prompt_kernel, tb=tb, pitch=pitch)
    full = lambda *shape: pl.BlockSpec(shape, lambda s: (0,) * len(shape))
    return pl.pallas_call(
        kern,
        grid=(seq // tb,),
        in_specs=[pl.BlockSpec((n, tb, SSM_WIDTH), lambda s: (0, s, ucol))]
        + [_layer_spec(p, layer) for p in (bq, a8, cq, d_skip, w_glu, b_glu)],
        out_specs=[pl.BlockSpec((n, tb, SSM_WIDTH), lambda s: (0, s, 0)),
                   full(2 * SSM_QUARTERS, 2 * SSM_QW)],
        out_shape=[jax.ShapeDtypeStruct((n, seq, SSM_WIDTH), F32),
                   jax.ShapeDtypeStruct((2 * SSM_QUARTERS, 2 * SSM_QW), F32)],
        scratch_shapes=[pltpu.VMEM((2 * SSM_QW // LANES, 2 * SSM_QUARTERS * pitch, LANES), F32),
                        pltpu.VMEM((2 * SSM_QUARTERS, 2 * SSM_QW), F32)],
        compiler_params=_params("arbitrary"),
        name="ssm_prompt",
    )(z3, bq, a8, cq, d_skip, w_glu, b_glu)


def _ssm_sample_kernel(u_ref, hre_ref, him_ref, bq_ref, a_ref, cq_ref, d_ref, wg_ref, bg_ref,
                       o_ref, ore_ref, oim_ref, x_ref):
    n, t, _ = u_ref.shape
    nq = SSM_QUARTERS
    qw = SSM_QW
    qc = SSM_QC
    nl = 2 * qw // LANES
    hl = nl // 2
    u_all = u_ref[...].reshape(n * t, SSM_WIDTH)
    ys = []
    for k in range(nq):
        u = u_all[:, k * qc:(k + 1) * qc]
        x = jnp.dot(u, bq_ref[k], preferred_element_type=F32, precision=lax.Precision.HIGHEST)
        for c in range(nl):
            x_ref[c] = x[:, c * LANES:(c + 1) * LANES]
        h = ([hre_ref[:, k * qw + c * LANES:k * qw + (c + 1) * LANES] for c in range(hl)]
             + [him_ref[:, k * qw + c * LANES:k * qw + (c + 1) * LANES] for c in range(hl)])
        for s in range(t):
            rows = pl.ds(s, n, stride=t)
            new = [None] * nl
            for c in range(hl):
                a_re = a_ref[k:k + 1, c * LANES:(c + 1) * LANES]
                a_im = a_ref[k:k + 1, (hl + c) * LANES:(hl + c + 1) * LANES]
                new[c] = a_re * h[c] - a_im * h[hl + c] + x_ref[c, rows, :]
                new[hl + c] = a_re * h[hl + c] + a_im * h[c] + x_ref[hl + c, rows, :]
            for c in range(nl):
                x_ref[c, rows, :] = new[c]
            h = new
        for c in range(hl):
            ore_ref[:, k * qw + c * LANES:k * qw + (c + 1) * LANES] = h[c]
            oim_ref[:, k * qw + c * LANES:k * qw + (c + 1) * LANES] = h[hl + c]
        hs = jnp.concatenate([x_ref[c] for c in range(nl)], axis=1).astype(BF16)
        ys.append(jnp.dot(hs, cq_ref[k].astype(BF16), preferred_element_type=F32))
    y = jnp.concatenate(ys, axis=1) + d_ref[...] * u_all
    gate = jnp.dot(y.astype(BF16), wg_ref[...], preferred_element_type=F32) + bg_ref[...]
    o_ref[...] = (y * jax.nn.sigmoid(gate)).reshape(n, t, SSM_WIDTH)


def ssm_sample(zs3, layer, h_re, h_im, bq32, a8, cq32, d_skip, w_glu, b_glu):
    n, t, _ = zs3.shape
    ucol = 3 * ATT_WIDTH // SSM_WIDTH
    ns = SSM_GROUPS * SSM_STATE
    full = lambda *shape: pl.BlockSpec(shape, lambda s: (0,) * len(shape))
    return pl.pallas_call(
        _ssm_sample_kernel,
        grid=(1,),
        in_specs=[pl.BlockSpec((n, t, SSM_WIDTH), lambda s: (0, 0, ucol))]
        + [_layer_spec(p, layer) for p in (h_re, h_im, bq32, a8, cq32, d_skip, w_glu, b_glu)],
        out_specs=[full(n, t, SSM_WIDTH), full(n, ns), full(n, ns)],
        out_shape=[jax.ShapeDtypeStruct((n, t, SSM_WIDTH), F32),
                   jax.ShapeDtypeStruct((n, ns), F32),
                   jax.ShapeDtypeStruct((n, ns), F32)],
        scratch_shapes=[pltpu.VMEM((2 * SSM_QW // LANES, n * t, LANES), F32)],
        compiler_params=_params("arbitrary"),
        name="ssm_sample",
    )(zs3, h_re, h_im, bq32, a8, cq32, d_skip, w_glu, b_glu)


def _sgu_norm_v(v, g):
    gv = jax.nn.gelu(v)
    vc = gv - jnp.mean(gv, axis=-1, keepdims=True)
    return vc * lax.rsqrt(jnp.mean(vc * vc, axis=-1, keepdims=True) + EPS) * g


def _sgu_causal(w_ref):
    c = SGU_CHUNK
    keep = lax.broadcasted_iota(jnp.int32, (c, c), 0) >= lax.broadcasted_iota(jnp.int32, (c, c), 1)
    return [jnp.where(keep, w_ref[h], 0.0).astype(BF16) for h in range(SGU_HEADS)]


def _sgu_prompt_kernel(u_ref, v_ref, g_ref, w_ref, b_ref, o_ref):
    c = SGU_CHUNK
    e = SGU_HEAD_DIM
    rows = u_ref.shape[1]
    w = _sgu_causal(w_ref)
    gu = jax.nn.gelu(u_ref[0])
    gv = _sgu_norm_v(v_ref[0], g_ref[...]).astype(BF16)
    for ci in range(rows // c):
        for h in range(SGU_HEADS):
            mixed = jnp.dot(w[h], gv[ci * c:(ci + 1) * c, h * e:(h + 1) * e],
                            preferred_element_type=F32) + b_ref[h]
            o_ref[0, ci * c:(ci + 1) * c, h * e:(h + 1) * e] = (
                gu[ci * c:(ci + 1) * c, h * e:(h + 1) * e] * mixed)


def sgu_prompt(z3, layer, g, w_sp, b_rows, rows=256):
    n, seq, _ = z3.shape
    ucol = (3 * ATT_WIDTH + SSM_WIDTH) // SGU_WIDTH
    return pl.pallas_call(
        _sgu_prompt_kernel,
        grid=(n, seq // rows),
        in_specs=[pl.BlockSpec((1, rows, SGU_WIDTH), lambda i, j: (i, j, ucol)),
                  pl.BlockSpec((1, rows, SGU_WIDTH), lambda i, j: (i, j, ucol + 1)),
                  _layer_spec(g, layer), _layer_spec(w_sp, layer), _layer_spec(b_rows, layer)],
        out_specs=pl.BlockSpec((1, rows, SGU_WIDTH), lambda i, j: (i, j, 0)),
        out_shape=jax.ShapeDtypeStruct((n, seq, SGU_WIDTH), F32),
        compiler_params=_params("parallel", "parallel"),
        name="sgu_prompt",
    )(z3, z3, g, w_sp, b_rows)


def _sgu_sample_kernel(u_ref, v_ref, g_ref, w_ref, b_ref, o_ref, gv_ref):
    c = SGU_CHUNK
    e = SGU_HEAD_DIM
    t = u_ref.shape[1]
    w = _sgu_causal(w_ref)
    gu = jax.nn.gelu(u_ref[0])
    gv = _sgu_norm_v(v_ref[0], g_ref[...])
    gv_ref[0] = gv
    gvp = jnp.concatenate([gv, jnp.zeros((c - t, SGU_WIDTH), F32)], axis=0).astype(BF16)
    for h in range(SGU_HEADS):
        mixed = jnp.dot(w[h], gvp[:, h * e:(h + 1) * e], preferred_element_type=F32) + b_ref[h]
        o_ref[0, :, h * e:(h + 1) * e] = gu[:, h * e:(h + 1) * e] * mixed[:t]


def sgu_sample(zs3, layer, g, w_sp, b_rows):
    n, t, _ = zs3.shape
    ucol = (3 * ATT_WIDTH + SSM_WIDTH) // SGU_WIDTH
    return pl.pallas_call(
        _sgu_sample_kernel,
        grid=(n,),
        in_specs=[pl.BlockSpec((1, t, SGU_WIDTH), lambda i: (i, 0, ucol)),
                  pl.BlockSpec((1, t, SGU_WIDTH), lambda i: (i, 0, ucol + 1)),
                  _layer_spec(g, layer), _layer_spec(w_sp, layer), _layer_spec(b_rows, layer)],
        out_specs=[pl.BlockSpec((1, t, SGU_WIDTH), lambda i: (i, 0, 0)),
                   pl.BlockSpec((1, t, SGU_WIDTH), lambda i: (i, 0, 0))],
        out_shape=[jax.ShapeDtypeStruct((n, t, SGU_WIDTH), F32),
                   jax.ShapeDtypeStruct((n, t, SGU_WIDTH), F32)],
        compiler_params=_params("parallel"),
        name="sgu_sample",
    )(zs3, zs3, g, w_sp, b_rows)


def _out_proj_kernel(att_ref, ssm_ref, sgu_ref, x_ref, w_ref, gm_ref, gp_ref, o_ref):
    a0 = ATT_WIDTH
    a1 = ATT_WIDTH + SSM_WIDTH
    a = _rms(att_ref[...], gm_ref[:, :a0]).astype(BF16)
    s = _rms(ssm_ref[...], gm_ref[:, a0:a1]).astype(BF16)
    c = _rms(sgu_ref[...], gm_ref[:, a1:]).astype(BF16)
    mixed = jnp.concatenate([a, s, c], axis=1)
    y = jnp.dot(mixed, w_ref[...], preferred_element_type=F32)
    o_ref[...] = x_ref[...] + _rms(y, gp_ref[...])


def out_proj(att, ssm, sgu, x, w, layer, gm, gp, tm):
    m, dm = x.shape
    row = lambda width: pl.BlockSpec((tm, width), lambda i: (i, 0))
    return pl.pallas_call(
        _out_proj_kernel,
        grid=(m // tm,),
        in_specs=[row(ATT_WIDTH), row(SSM_WIDTH), row(SGU_WIDTH), row(dm),
                  pl.BlockSpec(w.shape, lambda i: (0, 0)), _layer_spec(gm, layer), _layer_spec(gp, layer)],
        out_specs=row(dm),
        out_shape=jax.ShapeDtypeStruct((m, dm), F32),
        compiler_params=_params("parallel"),
        name="out_proj",
    )(att, ssm, sgu, x, w, gm, gp)


def _ffn_kernel(x_ref, gpre_ref, wg_ref, wu_ref, wd_ref, gpost_ref, *rest, n_cast):
    w32_refs, o_ref = rest[:n_cast], rest[n_cast]
    w16_refs, h_ref = rest[n_cast + 1:2 * n_cast + 1], rest[-1]
    j = pl.program_id(1)
    last = pl.num_programs(1) - 1
    tm = x_ref.shape[0]
    chunk = min(tm, FFN_EDGE_ROWS)

    def partial_out(h):
        gate = jnp.dot(h, wg_ref[...], preferred_element_type=F32)
        up = jnp.dot(h, wu_ref[...], preferred_element_type=F32)
        act = (jax.nn.silu(gate) * up).astype(BF16)
        return jnp.dot(act, wd_ref[...], preferred_element_type=F32)

    def cast_side_stream():
        for src_ref, dst_ref in zip(w32_refs, w16_refs):
            dst_ref[...] = src_ref[...].astype(BF16)

    @pl.when(j == 0)
    def _():
        for r in range(0, tm, chunk):
            rows = pl.ds(r, chunk)
            h = _rms(x_ref[rows, :], gpre_ref[...]).astype(BF16)
            h_ref[rows, :] = h
            o_ref[rows, :] = partial_out(h)
        cast_side_stream()

    @pl.when((j > 0) & (j < last))
    def _():
        o_ref[...] += partial_out(h_ref[...])
        cast_side_stream()

    @pl.when(j == last)
    def _():
        for r in range(0, tm, chunk):
            rows = pl.ds(r, chunk)
            f = o_ref[rows, :] + partial_out(h_ref[rows, :])
            o_ref[rows, :] = x_ref[rows, :] + _rms(f, gpost_ref[...])
        cast_side_stream()


def ffn(x, gpre, wg, wu, wd, layer, gpost, tm, tf, cast=()):
    m, dm = x.shape
    f = wg.shape[1]
    gi, gj = m // tm, f // tf
    assert gj >= 2
    cb = CAST_BLOCK
    cast_in, cast_out, cast_shape = [], [], []
    for w32, lyr, rows_follow_i in cast:
        _, r, c = w32.shape
        nr, nc = r // cb, c // cb
        assert r % cb == 0 and c % cb == 0
        if rows_follow_i:
            assert nr == gi and nc <= gj
            imap = lambda i, j, nc=nc: (i, jnp.minimum(j, nc - 1))
        else:
            assert nc == gi and nr <= gj
            imap = lambda i, j, nr=nr: (jnp.minimum(j, nr - 1), i)
        cast_in.append(pl.BlockSpec((None, cb, cb), lambda i, j, imap=imap, lyr=lyr: (lyr,) + imap(i, j)))
        cast_out.append(pl.BlockSpec((cb, cb), imap))
        cast_shape.append(jax.ShapeDtypeStruct((r, c), BF16))
    kern = functools.partial(_ffn_kernel, n_cast=len(cast))
    res = pl.pallas_call(
        kern,
        grid=(gi, gj),
        in_specs=[pl.BlockSpec((tm, dm), lambda i, j: (i, 0)),
                  _layer_spec(gpre, layer),
                  pl.BlockSpec((dm, tf), lambda i, j: (0, j)),
                  pl.BlockSpec((dm, tf), lambda i, j: (0, j)),
                  pl.BlockSpec((tf, dm), lambda i, j: (j, 0)),
                  _layer_spec(gpost, layer)] + cast_in,
        out_specs=[pl.BlockSpec((tm, dm), lambda i, j: (i, 0))] + cast_out,
        out_shape=[jax.ShapeDtypeStruct((m, dm), F32)] + cast_shape,
        scratch_shapes=[pltpu.VMEM((tm, dm), BF16)],
        compiler_params=_params("arbitrary", "arbitrary"),
        name="ffn_cast" if cast else "ffn",
    )(x, gpre, wg, wu, wd, gpost, *[w for w, _, _ in cast])
    return res[0], res[1:]


def _t5_bucket(dist):
    max_exact = N_REL_BUCKETS // 2
    df = jnp.maximum(dist, 1).astype(F32)
    large = max_exact + (jnp.log(df / max_exact) / math.log(REL_MAX_DIST / max_exact)
                         * (N_REL_BUCKETS - max_exact)).astype(jnp.int32)
    large = jnp.minimum(large, N_REL_BUCKETS - 1)
    return jnp.where(dist < max_exact, dist, large)


def _strided_bias(rel_bias, dilation, n_steps):
    dist = jnp.arange(n_steps + 1, dtype=jnp.int32) * dilation
    return rel_bias[_t5_bucket(dist)].astype(F32)


def _bias_lookup(sb, steps, ok):
    onehot = (steps[..., None] == jnp.arange(sb.shape[0])).astype(F32)
    vals = jnp.einsum('...j,jh->h...', onehot, sb, precision=lax.Precision.HIGHEST)
    return jnp.where(ok[None], vals, NEG_INF)


def _prompt_bias_tables(rel_bias):
    qb = ATT_BLOCK
    qi = jnp.arange(qb)[:, None]
    kj = jnp.arange(2 * qb)[None, :]
    tables = []
    for window, d in DILATED_PATTERNS:
        nk = window // d
        sb = _strided_bias(rel_bias, d, nk)
        delta = qb + qi - kj
        normal = _bias_lookup(sb, jnp.clip(delta, 0, nk), (delta >= 0) & (delta <= nk))
        delta_f = qi - kj
        first = _bias_lookup(sb, jnp.clip(delta_f, 0, nk), (delta_f >= 0) & (delta_f <= nk))
        tables.append(jnp.stack([normal, first]))
    return jnp.stack(tables).astype(F32)


def _sample_bias_tables(rel_bias, w, t):
    tq = jnp.arange(t)[:, None]
    cache_delta = w + tq - jnp.arange(w)[None, :]
    new_delta = tq - jnp.arange(ATT_BLOCK)[None, :]
    out_c, out_n = [], []
    for window, d in DILATED_PATTERNS:
        nk = window // d
        sb = _strided_bias(rel_bias, d, nk)

        def table(delta, extra_ok):
            ok = (delta >= 0) & (delta % d == 0) & (delta <= nk * d) & extra_ok
            return _bias_lookup(sb, jnp.clip(delta // d, 0, nk), ok)

        out_c.append(table(cache_delta, True))
        out_n.append(table(new_delta, jnp.arange(ATT_BLOCK)[None, :] < t))
    return jnp.stack(out_c).astype(F32), jnp.stack(out_n).astype(F32)


def _ssm_tables(lam_re, lam_im, log_step, b_re, b_im, c_re, c_im):
    g, p, cg = SSM_GROUPS, SSM_STATE, SSM_GROUP
    nq = SSM_QUARTERS
    gq = g // nq
    step = jnp.exp(log_step)[:, None]
    mag = jnp.exp(lam_re * step)
    a_re = mag * jnp.cos(lam_im * step)
    a_im = mag * jnp.sin(lam_im * step)
    den = lam_re * lam_re + lam_im * lam_im
    coef_re = ((a_re - 1.0) * lam_re + a_im * lam_im) / den
    coef_im = (a_im * lam_re - (a_re - 1.0) * lam_im) / den
    bb_re = coef_re[..., None] * b_re - coef_im[..., None] * b_im
    bb_im = coef_re[..., None] * b_im + coef_im[..., None] * b_re
    eye = jnp.eye(gq, dtype=F32)

    def in_mat(bb):
        bb = bb.reshape(nq, gq, p, cg)
        return jnp.einsum('kgpc,gh->kgchp', bb, eye).reshape(nq, gq * cg, gq * p)

    def out_mat(c):
        c = c.reshape(nq, gq, cg, p)
        return jnp.einsum('kgcp,gh->kgphc', c, eye).reshape(nq, gq * p, gq * cg)

    bq = jnp.concatenate([in_mat(bb_re), in_mat(bb_im)], axis=2)
    cq = jnp.concatenate([out_mat(c_re), -out_mat(c_im)], axis=1)
    aq = jnp.concatenate([a_re.reshape(nq, gq * p), a_im.reshape(nq, gq * p)], axis=1)
    a8 = jnp.concatenate([aq, aq], axis=0)
    return bq, cq, a8


def _rows(v):
    return v.reshape(v.shape[0], 1, -1)


def kernel(x_prompt, x_sample, cache_attn_k, cache_attn_v, state_ssm_re, state_ssm_im, rel_bias, w_in, w_out, g_pre_mix, g_post_mix, g_mix_out, ssm_lam_re, ssm_lam_im, ssm_log_step, ssm_b_re, ssm_b_im, ssm_c_re, ssm_c_im, ssm_d, ssm_w_glu, ssm_b_glu, sgu_g, sgu_w, sgu_b, g_pre_ffn, g_post_ffn, w_gate, w_up, w_down):
    nb, seq, dm = x_prompt.shape
    ns, ts, _ = x_sample.shape
    wbuf = cache_attn_k.shape[2]
    n_keep = min(DILATED_PATTERNS[-1][0], seq)
    h, e = ATT_HEADS, ATT_HEAD_DIM

    bias_p = _prompt_bias_tables(rel_bias)
    bias_c, bias_n = _sample_bias_tables(rel_bias, wbuf, ts)
    cache_k = cache_attn_k.reshape(DEPTH, ns, wbuf * h, e)
    cache_v = cache_attn_v.reshape(DEPTH, ns, wbuf * h, e)
    k_buf = jnp.zeros((DEPTH, nb, n_keep * h, e), F32)
    v_buf = jnp.zeros((DEPTH, nb, n_keep * h, e), F32)
    st_re = state_ssm_re.reshape(DEPTH, ns, SSM_GROUPS * SSM_STATE)
    st_im = state_ssm_im.reshape(DEPTH, ns, SSM_GROUPS * SSM_STATE)

    xp = x_prompt.reshape(nb * seq, dm)
    xs = x_sample.reshape(ns * ts, dm)
    outs = {k: [] for k in ('rp', 'ip', 'ks', 'vs', 'rs', 'is', 'us')}

    big = (w_in, w_out, w_gate, w_up, w_down)
    big16 = tuple(w[0].astype(BF16) for w in big)
    w_glu16 = ssm_w_glu.astype(BF16)
    bq, cq, a8 = jax.vmap(_ssm_tables)(ssm_lam_re, ssm_lam_im, ssm_log_step,
                                       ssm_b_re, ssm_b_im, ssm_c_re, ssm_c_im)
    bq16, cq16 = bq.astype(BF16), cq.astype(BF16)
    sgu_b_rows = jnp.broadcast_to(sgu_b[..., None], (DEPTH, SGU_HEADS, SGU_CHUNK, SGU_HEAD_DIM))
    g_pre, g_post, g_mix = _rows(g_pre_mix), _rows(g_post_mix), _rows(g_mix_out)
    g_pre_f, g_post_f = _rows(g_pre_ffn), _rows(g_post_ffn)
    d_skip, b_glu, g_sgu = _rows(ssm_d), _rows(ssm_b_glu), _rows(sgu_g)

    for l in range(DEPTH):
        w_in16, w_out16, w_gate16, w_up16, w_down16 = big16
        cast = [(w, l + 1, w is not w_down) for w in big] if l + 1 < DEPTH else []

        z, k_buf, v_buf = in_proj(xp, g_pre, w_in16, l, tm=1024, tn=768,
                                  kv=(seq, n_keep, k_buf, v_buf))
        z = z.reshape(nb, seq, IN_COLS)
        o_att = attn_prompt(z, bias_p)
        o_ssm, h_fin = ssm_prompt(z, l, bq16, a8, cq16, d_skip, w_glu16, b_glu)
        o_sgu = sgu_prompt(z, l, g_sgu, sgu_w, sgu_b_rows)
        xp = out_proj(o_att.reshape(nb * seq, ATT_WIDTH), o_ssm.reshape(nb * seq, SSM_WIDTH),
                      o_sgu.reshape(nb * seq, SGU_WIDTH), xp, w_out16, l, g_mix, g_post, tm=512)
        xp, next16 = ffn(xp, g_pre_f, w_gate16, w_up16, w_down16, l, g_post_f, tm=1024, tf=256,
                         cast=cast)
        hq = h_fin.reshape(nb, SSM_QUARTERS, 2, SSM_GROUPS // SSM_QUARTERS, SSM_STATE)
        outs['rp'].append(hq[:, :, 0].reshape(nb, SSM_GROUPS, SSM_STATE))
        outs['ip'].append(hq[:, :, 1].reshape(nb, SSM_GROUPS, SSM_STATE))

        zs = in_proj(xs, g_pre, w_in16, l, tm=ns * ts, tn=768).reshape(ns, ts, IN_COLS)
        s_att = attn_sample(zs, cache_k, cache_v, l, bias_c, bias_n)
        s_ssm, s_re, s_im = ssm_sample(zs, l, st_re, st_im, bq, a8, cq, d_skip, w_glu16, b_glu)
        s_sgu, s_gv = sgu_sample(zs, l, g_sgu, sgu_w, sgu_b_rows)
        xs = out_proj(s_att.reshape(ns * ts, ATT_WIDTH), s_ssm.reshape(ns * ts, SSM_WIDTH),
                      s_sgu.reshape(ns * ts, SGU_WIDTH), xs, w_out16, l, g_mix, g_post, tm=ns * ts)
        xs, _ = ffn(xs, g_pre_f, w_gate16, w_up16, w_down16, l, g_post_f, tm=ns * ts, tf=512)
        big16 = tuple(next16) or big16
        outs['ks'].append(zs[:, :, ATT_WIDTH:2 * ATT_WIDTH].reshape(ns, ts, h, e))
        outs['vs'].append(zs[:, :, 2 * ATT_WIDTH:3 * ATT_WIDTH].reshape(ns, ts, h, e))
        outs['rs'].append(s_re.reshape(ns, SSM_GROUPS, SSM_STATE))
        outs['is'].append(s_im.reshape(ns, SSM_GROUPS, SSM_STATE))
        outs['us'].append(s_gv)

    st = lambda key: jnp.stack(outs[key])
    kv_shape = (DEPTH, nb, n_keep, h, e)
    return (xp.reshape(nb, seq, dm), xs.reshape(ns, ts, dm),
            k_buf.reshape(kv_shape), v_buf.reshape(kv_shape), st('rp'), st('ip'),
            st('ks'), st('vs'), st('rs'), st('is'), st('us'))
```

```python
import functools
import math

import jax
import jax.numpy as jnp
from jax import lax
from jax.experimental import pallas as pl
from jax.experimental.pallas import tpu as pltpu

F32 = jnp.float32
BF16 = jnp.bfloat16

D_MODEL = 2048
DEPTH = 4
ATT_HEADS = 8
ATT_HEAD_DIM = 128
ATT_WIDTH = ATT_HEADS * ATT_HEAD_DIM
DILATED_PATTERNS = ((128, 1), (512, 4), (2048, 16))
ATT_BLOCK = 128
ATT_GROUP = (8, 8, 4)
ATT_CLASSES = 4
ATT_SCALE = ATT_HEAD_DIM ** -0.5
N_REL_BUCKETS = 32
REL_MAX_DIST = 2048
SSM_WIDTH = 512
SSM_GROUP = 16
SSM_GROUPS = SSM_WIDTH // SSM_GROUP
SSM_STATE = 64
SSM_QUARTERS = 4
SSM_QW = SSM_GROUPS * SSM_STATE // SSM_QUARTERS
SSM_QC = SSM_WIDTH // SSM_QUARTERS
SGU_WIDTH = 512
SGU_HEADS = 4
SGU_HEAD_DIM = SGU_WIDTH // SGU_HEADS
SGU_CHUNK = 128
IN_COLS = 3 * ATT_WIDTH + SSM_WIDTH + 2 * SGU_WIDTH
FFN_HIDDEN = 5632
EPS = 1e-6
NEG_INF = -1e30
LANES = 128
CAST_BLOCK = 256
FFN_EDGE_ROWS = 256

VMEM_LIMIT_BYTES = 56 * 1024 * 1024


def _params(*sem):
    return pltpu.CompilerParams(dimension_semantics=sem, vmem_limit_bytes=VMEM_LIMIT_BYTES)


def _layer_spec(arr, layer):
    zeros = (0,) * (arr.ndim - 1)
    return pl.BlockSpec((None,) + arr.shape[1:], lambda *_: (layer,) + zeros)


def _rms(x, g):
    return x * lax.rsqrt(jnp.mean(x * x, axis=-1, keepdims=True) + EPS) * g


def _in_proj_kernel(x_ref, g_ref, w_ref, *rest, keep_from, tiles_per_seq):
    h_ref = rest[-1]
    outs = rest[-4:-1] if len(rest) > 2 else rest[:1]
    z_ref = outs[0]
    i = pl.program_id(0)
    j = pl.program_id(1)

    tm, tn = z_ref.shape
    chunk = min(tm, FFN_EDGE_ROWS)

    @pl.when(j == 0)
    def _():
        for r in range(0, tm, chunk):
            rows = pl.ds(r, chunk)
            h = _rms(x_ref[rows, :], g_ref[...]).astype(BF16)
            h_ref[rows, :] = h
            z_ref[rows, :] = jnp.dot(h, w_ref[...], preferred_element_type=F32)

    @pl.when(j > 0)
    def _():
        _in_proj_tile(i, j, h_ref, w_ref, outs, keep_from, tiles_per_seq)


def _in_proj_tile(i, j, h_ref, w_ref, outs, keep_from, tiles_per_seq):
    z_ref = outs[0]
    res = jnp.dot(h_ref[...], w_ref[...], preferred_element_type=F32)
    z_ref[...] = res

    if len(outs) > 1:
        tm, tn = res.shape
        keep = (i % tiles_per_seq) >= keep_from
        by_tile = {}
        for buf_ref, col0 in ((outs[1], ATT_WIDTH), (outs[2], 2 * ATT_WIDTH)):
            for head in range(ATT_HEADS):
                col = col0 + head * ATT_HEAD_DIM
                by_tile.setdefault(col // tn, []).append((buf_ref, head, col % tn))
        for tile, heads in by_tile.items():
            @pl.when(keep & (j == tile))
            def _(heads=heads):
                for buf_ref, head, off in heads:
                    rows = pl.ds(head, tm, stride=ATT_HEADS)
                    buf_ref[0, 0, rows, :] = res[:, off:off + ATT_HEAD_DIM]


def in_proj(x, g, w, layer, tm, tn, kv=None):
    m, k = x.shape
    n = w.shape[1]
    assert tn % ATT_HEAD_DIM == 0
    in_specs = [pl.BlockSpec((tm, k), lambda i, j: (i, 0)),
                _layer_spec(g, layer),
                pl.BlockSpec((k, tn), lambda i, j: (0, j))]
    z_spec = pl.BlockSpec((tm, tn), lambda i, j: (i, j))
    z_shape = jax.ShapeDtypeStruct((m, n), F32)
    scratch = [pltpu.VMEM((tm, k), BF16)]
    if kv is None:
        kern = functools.partial(_in_proj_kernel, keep_from=0, tiles_per_seq=1)
        return pl.pallas_call(
            kern, grid=(m // tm, n // tn), in_specs=in_specs, out_specs=z_spec, out_shape=z_shape,
            scratch_shapes=scratch, compiler_params=_params("parallel", "arbitrary"),
            name="in_proj",
        )(x, g, w)
    seq, n_keep, k_buf, v_buf = kv
    tiles_per_seq = seq // tm
    keep_from = (seq - n_keep) // tm
    assert seq % tm == 0 and (seq - n_keep) % tm == 0
    assert tn <= ATT_WIDTH

    def buf_map(i, j):
        return (layer, i // tiles_per_seq, jnp.maximum(i % tiles_per_seq - keep_from, 0), 0)

    buf_spec = pl.BlockSpec((1, 1, tm * ATT_HEADS, ATT_HEAD_DIM), buf_map)
    any_spec = pl.BlockSpec(memory_space=pl.ANY)
    kern = functools.partial(_in_proj_kernel, keep_from=keep_from, tiles_per_seq=tiles_per_seq)
    return pl.pallas_call(
        kern, grid=(m // tm, n // tn),
        in_specs=in_specs + [any_spec, any_spec],
        out_specs=[z_spec, buf_spec, buf_spec],
        out_shape=[z_shape, jax.ShapeDtypeStruct(k_buf.shape, F32),
                   jax.ShapeDtypeStruct(v_buf.shape, F32)],
        input_output_aliases={3: 1, 4: 2},
        scratch_shapes=scratch, compiler_params=_params("arbitrary", "arbitrary"),
        name="in_proj_kv",
    )(x, g, w, k_buf, v_buf)


def _attn_prompt_kernel(*refs, seq):
    nc = ATT_CLASSES
    q_refs, k_refs, v_refs = refs[:nc], refs[nc:2 * nc], refs[2 * nc:3 * nc]
    b_ref, o_ref, acc_ref, m_ref, l_ref = refs[3 * nc:]
    qb = ATT_BLOCK
    per = seq // nc
    sub = qb // nc
    n_pat = len(DILATED_PATTERNS)
    nt = (((1,), (1,)), ((), ()))

    def load(parts):
        vals = [ref[0, rows, :] for ref, rows in parts]
        return (vals[0] if len(vals) == 1 else jnp.concatenate(vals, axis=0)).astype(BF16)

    def run_units(p, units):
        scores = [lax.dot_general(load(qp), load(kp), nt, preferred_element_type=F32)
                  for qp, kp, _, _, _ in units]
        probs = []
        for (_, _, _, bias, outs), s in zip(units, scores):
            s = s * ATT_SCALE + bias
            m_blk = jnp.max(s, axis=1, keepdims=True)
            pe = jnp.exp(s - m_blk)
            wide = (s.shape[0], ATT_HEAD_DIM)
            m_w = jnp.broadcast_to(m_blk, wide)
            l_w = jnp.broadcast_to(jnp.sum(pe, axis=1, keepdims=True), wide)
            for rows, part in outs:
                m_ref[p, rows, :] = m_w[part]
                l_ref[p, rows, :] = l_w[part]
            probs.append(pe.astype(BF16))
        for (_, _, vp, _, outs), pe in zip(units, probs):
            acc = jnp.dot(pe, load(vp), preferred_element_type=F32)
            for rows, part in outs:
                acc_ref[p, rows, :] = acc[part]

    def group0(g, carry):
        units = []
        for u in range(ATT_GROUP[0]):
            b = g * ATT_GROUP[0] + u
            q0 = pl.multiple_of(b * sub, sub)
            k0 = pl.multiple_of(jnp.maximum(b - 1, 0) * sub, sub)
            units.append(([(q_refs[c], pl.ds(q0, sub)) for c in range(nc)],
                          [(k_refs[c], pl.ds(k0, 2 * sub)) for c in range(nc)],
                          [(v_refs[c], pl.ds(k0, 2 * sub)) for c in range(nc)],
                          b_ref[0, (b == 0).astype(jnp.int32), 0],
                          [(pl.ds(pl.multiple_of(c * per + q0, sub), sub), slice(c * sub, (c + 1) * sub))
                           for c in range(nc)]))
        run_units(0, units)
        return carry

    lax.fori_loop(0, seq // qb // ATT_GROUP[0], group0, 0)

    for c in range(nc):
        def group1(g, carry, c=c):
            units = []
            for u in range(ATT_GROUP[1]):
                b = g * ATT_GROUP[1] + u
                q0 = pl.multiple_of(b * qb, qb)
                k_rows = pl.ds(pl.multiple_of(jnp.maximum(b - 1, 0) * qb, qb), 2 * qb)
                units.append(([(q_refs[c], pl.ds(q0, qb))], [(k_refs[c], k_rows)], [(v_refs[c], k_rows)],
                              b_ref[1, (b == 0).astype(jnp.int32), 0],
                              [(pl.ds(pl.multiple_of(c * per + q0, qb), qb), slice(None))]))
            run_units(1, units)
            return carry

        lax.fori_loop(0, per // qb // ATT_GROUP[1], group1, 0)

        def group2(g, carry, c=c):
            units = []
            for u in range(ATT_GROUP[2] // 2):
                e = g * (ATT_GROUP[2] // 2) + u
                rows = pl.ds(e, 2 * qb, stride=nc)
                units.append(([(q_refs[c], rows)], [(k_refs[c], rows)], [(v_refs[c], rows)],
                              jnp.concatenate([b_ref[2, 1, 0], b_ref[2, 0, 0]], axis=0),
                              [(pl.ds(c * per + e, 2 * qb, stride=nc), slice(None))]))
            run_units(2, units)
            return carry

        lax.fori_loop(0, per // (2 * qb) // (ATT_GROUP[2] // 2), group2, 0)

    rows_per = 2 * qb
    chunks_per_class = per // rows_per

    def merge(ci, carry):
        rows = pl.ds(pl.multiple_of(ci * rows_per, rows_per), rows_per)
        ms = [m_ref[p, rows, :] for p in range(n_pat)]
        m = functools.reduce(jnp.maximum, ms)
        num = den = None
        for p in range(n_pat):
            w = jnp.exp(ms[p] - m)
            num = w * acc_ref[p, rows, :] if num is None else num + w * acc_ref[p, rows, :]
            den = w * l_ref[p, rows, :] if den is None else den + w * l_ref[p, rows, :]
        cls = ci // chunks_per_class
        first_pos = (ci % chunks_per_class) * rows_per * nc + cls
        o_ref[0, pl.ds(first_pos, rows_per, stride=nc), :] = num / den
        return carry

    lax.fori_loop(0, seq // rows_per, merge, 0)


def attn_prompt(z3, bias):
    n, seq, width = z3.shape
    h, nc, qb = ATT_HEADS, ATT_CLASSES, ATT_BLOCK
    (_, d0), (_, d1), (w2, d2) = DILATED_PATTERNS
    assert d0 == 1 and d1 == nc and d2 == nc * nc and seq // d2 == 2 * qb and w2 // d2 == qb
    per = seq // nc
    assert (seq // qb) % ATT_GROUP[0] == 0 and (per // qb) % ATT_GROUP[1] == 0
    assert (per // (2 * qb)) % (ATT_GROUP[2] // 2) == 0
    z4 = z3.reshape(n, per, nc * width)
    blocks_per_pos = width // ATT_HEAD_DIM

    def class_spec(c, first_block):
        return pl.BlockSpec((1, per, ATT_HEAD_DIM),
                            lambda i, j: (i, 0, c * blocks_per_pos + first_block + j))

    in_specs = ([class_spec(c, 0) for c in range(nc)] + [class_spec(c, h) for c in range(nc)]
                + [class_spec(c, 2 * h) for c in range(nc)]
                + [pl.BlockSpec((3, 2, 1, qb, 2 * qb), lambda i, j: (0, 0, j, 0, 0))])
    kern = functools.partial(_attn_prompt_kernel, seq=seq)
    return pl.pallas_call(
        kern,
        grid=(n, h),
        in_specs=in_specs,
        out_specs=pl.BlockSpec((1, seq, ATT_HEAD_DIM), lambda i, j: (i, 0, j)),
        out_shape=jax.ShapeDtypeStruct((n, seq, ATT_WIDTH), F32),
        scratch_shapes=[pltpu.VMEM((len(DILATED_PATTERNS), seq, ATT_HEAD_DIM), F32)] * 3,
        compiler_params=_params("parallel", "parallel"),
        name="attn_prompt",
    )(*([z4] * (3 * nc)), bias)


def _attn_sample_kernel(q_ref, kn_ref, vn_ref, kc_ref, vc_ref, bc_ref, bn_ref, o_ref):
    t = q_ref.shape[1]
    e = ATT_HEAD_DIM
    slots = kc_ref.shape[2] // ATT_HEADS
    n_pat = len(DILATED_PATTERNS)
    pad = jnp.zeros((ATT_BLOCK - t, e), F32)
    nt = (((1,), (1,)), ((), ()))
    heads = range(ATT_HEADS)
    cols = [slice(h * e, (h + 1) * e) for h in heads]
    head_rows = [pl.ds(h, slots, stride=ATT_HEADS) for h in heads]

    sc, sn = [], []
    for h in heads:
        q = q_ref[0, :, cols[h]].astype(BF16)
        kn = jnp.concatenate([kn_ref[0, :, cols[h]], pad], axis=0).astype(BF16)
        kc = kc_ref[0, 0, head_rows[h], :].astype(BF16)
        sc.append(lax.dot_general(q, kc, nt, preferred_element_type=F32) * ATT_SCALE)
        sn.append(lax.dot_general(q, kn, nt, preferred_element_type=F32) * ATT_SCALE)
    pc, pn, ls = [], [], []
    for h in heads:
        scs = [sc[h] + bc_ref[p, h] for p in range(n_pat)]
        sns = [sn[h] + bn_ref[p, h] for p in range(n_pat)]
        m = None
        for s in scs + sns:
            mx = jnp.max(s, axis=1, keepdims=True)
            m = mx if m is None else jnp.maximum(m, mx)
        pch = sum(jnp.exp(s - m) for s in scs)
        pnh = sum(jnp.exp(s - m) for s in sns)
        ls.append(jnp.sum(pch, axis=1, keepdims=True) + jnp.sum(pnh, axis=1, keepdims=True))
        pc.append(pch.astype(BF16))
        pn.append(pnh.astype(BF16))
    for h in heads:
        vn = jnp.concatenate([vn_ref[0, :, cols[h]], pad], axis=0).astype(BF16)
        vc = vc_ref[0, 0, head_rows[h], :].astype(BF16)
        acc = (jnp.dot(pc[h], vc, preferred_element_type=F32)
               + jnp.dot(pn[h], vn, preferred_element_type=F32))
        o_ref[0, :, cols[h]] = acc / ls[h]


def attn_sample(zs3, cache_k, cache_v, layer, bias_c, bias_n):
    n, t, _ = zs3.shape
    h = ATT_HEADS
    e = ATT_HEAD_DIM
    w = cache_k.shape[2] // h
    return pl.pallas_call(
        _attn_sample_kernel,
        grid=(n,),
        in_specs=[pl.BlockSpec((1, t, ATT_WIDTH), lambda i: (i, 0, 0)),
                  pl.BlockSpec((1, t, ATT_WIDTH), lambda i: (i, 0, 1)),
                  pl.BlockSpec((1, t, ATT_WIDTH), lambda i: (i, 0, 2)),
                  pl.BlockSpec((1, 1, w * h, e), lambda i: (layer, i, 0, 0)),
                  pl.BlockSpec((1, 1, w * h, e), lambda i: (layer, i, 0, 0)),
                  pl.BlockSpec((3, h, t, w), lambda i: (0, 0, 0, 0)),
                  pl.BlockSpec((3, h, t, ATT_BLOCK), lambda i: (0, 0, 0, 0))],
        out_specs=pl.BlockSpec((1, t, ATT_WIDTH), lambda i: (i, 0, 0)),
        out_shape=jax.ShapeDtypeStruct((n, t, ATT_WIDTH), F32),
        compiler_params=_params("parallel"),
        name="attn_sample",
    )(zs3, zs3, zs3, cache_k, cache_v, bias_c, bias_n)


def _ssm_prompt_kernel(u_ref, bq_ref, a_ref, cq_ref, d_ref, wg_ref, bg_ref,
                       o_ref, hfin_ref, x_ref, h_ref, *, tb, pitch):
    step = pl.program_id(0)
    nq = SSM_QUARTERS
    qw = SSM_QW
    qc = SSM_QC

    @pl.when(step == 0)
    def _():
        h_ref[...] = jnp.zeros_like(h_ref)

    nl = 2 * qw // LANES
    hl = nl // 2

    for b in range(2):
        for k in range(nq):
            u = u_ref[b, :, k * qc:(k + 1) * qc].astype(BF16)
            x = jnp.dot(u, bq_ref[k], preferred_element_type=F32)
            for c in range(nl):
                x_ref[c, pl.ds((b * nq + k) * pitch, tb), :] = x[:, c * LANES:(c + 1) * LANES]

    a = [a_ref[:, c * LANES:(c + 1) * LANES] for c in range(nl)]

    def scan(t, h):
        rows = pl.ds(t, 2 * nq, stride=pitch)
        new = [None] * nl
        for c in range(hl):
            a_re, a_im, h_re, h_im = a[c], a[hl + c], h[c], h[hl + c]
            new[c] = a_re * h_re - a_im * h_im + x_ref[c, rows, :]
            new[hl + c] = a_re * h_im + a_im * h_re + x_ref[hl + c, rows, :]
        for c in range(nl):
            x_ref[c, rows, :] = new[c]
        return tuple(new)

    h0 = tuple(h_ref[:, c * LANES:(c + 1) * LANES] for c in range(nl))
    h_new = jnp.concatenate(lax.fori_loop(0, tb, scan, h0, unroll=8), axis=1)
    h_ref[...] = h_new

    for b in range(2):
        ys = []
        for k in range(nq):
            rows = pl.ds((b * nq + k) * pitch, tb)
            hs = jnp.concatenate([x_ref[c, rows, :] for c in range(nl)], axis=1).astype(BF16)
            ys.append(jnp.dot(hs, cq_ref[k], preferred_element_type=F32))
        y = jnp.concatenate(ys, axis=1) + d_ref[...] * u_ref[b]
        gate = jnp.dot(y.astype(BF16), wg_ref[...], preferred_element_type=F32) + bg_ref[...]
        o_ref[b] = y * jax.nn.sigmoid(gate)

    @pl.when(step == pl.num_programs(0) - 1)
    def _():
        hfin_ref[...] = h_new


def ssm_prompt(z3, layer, bq, a8, cq, d_skip, w_glu, b_glu, tb=512):
    n, seq, _ = z3.shape
    assert n == 2
    pitch = tb + 8
    ucol = 3 * ATT_WIDTH // SSM_WIDTH
    kern = functools.partial(_ssm_prompt_kernel, tb=tb, pitch=pitch)
    full = lambda *shape: pl.BlockSpec(shape, lambda s: (0,) * len(shape))
    return pl.pallas_call(
        kern,
        grid=(seq // tb,),
        in_specs=[pl.BlockSpec((n, tb, SSM_WIDTH), lambda s: (0, s, ucol))]
        + [_layer_spec(p, layer) for p in (bq, a8, cq, d_skip, w_glu, b_glu)],
        out_specs=[pl.BlockSpec((n, tb, SSM_WIDTH), lambda s: (0, s, 0)),
                   full(2 * SSM_QUARTERS, 2 * SSM_QW)],
        out_shape=[jax.ShapeDtypeStruct((n, seq, SSM_WIDTH), F32),
                   jax.ShapeDtypeStruct((2 * SSM_QUARTERS, 2 * SSM_QW), F32)],
        scratch_shapes=[pltpu.VMEM((2 * SSM_QW // LANES, 2 * SSM_QUARTERS * pitch, LANES), F32),
                        pltpu.VMEM((2 * SSM_QUARTERS, 2 * SSM_QW), F32)],
        compiler_params=_params("arbitrary"),
        name="ssm_prompt",
    )(z3, bq, a8, cq, d_skip, w_glu, b_glu)


def _ssm_sample_kernel(u_ref, hre_ref, him_ref, bq_ref, a_ref, cq_ref, d_ref, wg_ref, bg_ref,
                       o_ref, ore_ref, oim_ref, x_ref):
    n, t, _ = u_ref.shape
    nq = SSM_QUARTERS
    qw = SSM_QW
    qc = SSM_QC
    nl = 2 * qw // LANES
    hl = nl // 2
    u_all = u_ref[...].reshape(n * t, SSM_WIDTH)
    ys = []
    for k in range(nq):
        u = u_all[:, k * qc:(k + 1) * qc]
        x = jnp.dot(u, bq_ref[k], preferred_element_type=F32, precision=lax.Precision.HIGHEST)
        for c in range(nl):
            x_ref[c] = x[:, c * LANES:(c + 1) * LANES]
        h = ([hre_ref[:, k * qw + c * LANES:k * qw + (c + 1) * LANES] for c in range(hl)]
             + [him_ref[:, k * qw + c * LANES:k * qw + (c + 1) * LANES] for c in range(hl)])
        for s in range(t):
            rows = pl.ds(s, n, stride=t)
            new = [None] * nl
            for c in range(hl):
                a_re = a_ref[k:k + 1, c * LANES:(c + 1) * LANES]
                a_im = a_ref[k:k + 1, (hl + c) * LANES:(hl + c + 1) * LANES]
                new[c] = a_re * h[c] - a_im * h[hl + c] + x_ref[c, rows, :]
                new[hl + c] = a_re * h[hl + c] + a_im * h[c] + x_ref[hl + c, rows, :]
            for c in range(nl):
                x_ref[c, rows, :] = new[c]
            h = new
        for c in range(hl):
            ore_ref[:, k * qw + c * LANES:k * qw + (c + 1) * LANES] = h[c]
            oim_ref[:, k * qw + c * LANES:k * qw + (c + 1) * LANES] = h[hl + c]
        hs = jnp.concatenate([x_ref[c] for c in range(nl)], axis=1).astype(BF16)
        ys.append(jnp.dot(hs, cq_ref[k].astype(BF16), preferred_element_type=F32))
    y = jnp.concatenate(ys, axis=1) + d_ref[...] * u_all
    gate = jnp.dot(y.astype(BF16), wg_ref[...], preferred_element_type=F32) + bg_ref[...]
    o_ref[...] = (y * jax.nn.sigmoid(gate)).reshape(n, t, SSM_WIDTH)


def ssm_sample(zs3, layer, h_re, h_im, bq32, a8, cq32, d_skip, w_glu, b_glu):
    n, t, _ = zs3.shape
    ucol = 3 * ATT_WIDTH // SSM_WIDTH
    ns = SSM_GROUPS * SSM_STATE
    full = lambda *shape: pl.BlockSpec(shape, lambda s: (0,) * len(shape))
    return pl.pallas_call(
        _ssm_sample_kernel,
        grid=(1,),
        in_specs=[pl.BlockSpec((n, t, SSM_WIDTH), lambda s: (0, 0, ucol))]
        + [_layer_spec(p, layer) for p in (h_re, h_im, bq32, a8, cq32, d_skip, w_glu, b_glu)],
        out_specs=[full(n, t, SSM_WIDTH), full(n, ns), full(n, ns)],
        out_shape=[jax.ShapeDtypeStruct((n, t, SSM_WIDTH), F32),
                   jax.ShapeDtypeStruct((n, ns), F32),
                   jax.ShapeDtypeStruct((n, ns), F32)],
        scratch_shapes=[pltpu.VMEM((2 * SSM_QW // LANES, n * t, LANES), F32)],
        compiler_params=_params("arbitrary"),
        name="ssm_sample",
    )(zs3, h_re, h_im, bq32, a8, cq32, d_skip, w_glu, b_glu)


def _sgu_norm_v(v, g):
    gv = jax.nn.gelu(v)
    vc = gv - jnp.mean(gv, axis=-1, keepdims=True)
    return vc * lax.rsqrt(jnp.mean(vc * vc, axis=-1, keepdims=True) + EPS) * g


def _sgu_causal(w_ref):
    c = SGU_CHUNK
    keep = lax.broadcasted_iota(jnp.int32, (c, c), 0) >= lax.broadcasted_iota(jnp.int32, (c, c), 1)
    return [jnp.where(keep, w_ref[h], 0.0).astype(BF16) for h in range(SGU_HEADS)]


def _sgu_prompt_kernel(u_ref, v_ref, g_ref, w_ref, b_ref, o_ref):
    c = SGU_CHUNK
    e = SGU_HEAD_DIM
    rows = u_ref.shape[1]
    w = _sgu_causal(w_ref)
    gu = jax.nn.gelu(u_ref[0])
    gv = _sgu_norm_v(v_ref[0], g_ref[...]).astype(BF16)
    for ci in range(rows // c):
        for h in range(SGU_HEADS):
            mixed = jnp.dot(w[h], gv[ci * c:(ci + 1) * c, h * e:(h + 1) * e],
                            preferred_element_type=F32) + b_ref[h]
            o_ref[0, ci * c:(ci + 1) * c, h * e:(h + 1) * e] = (
                gu[ci * c:(ci + 1) * c, h * e:(h + 1) * e] * mixed)


def sgu_prompt(z3, layer, g, w_sp, b_rows, rows=1024):
    n, seq, _ = z3.shape
    ucol = (3 * ATT_WIDTH + SSM_WIDTH) // SGU_WIDTH
    return pl.pallas_call(
        _sgu_prompt_kernel,
        grid=(n, seq // rows),
        in_specs=[pl.BlockSpec((1, rows, SGU_WIDTH), lambda i, j: (i, j, ucol)),
                  pl.BlockSpec((1, rows, SGU_WIDTH), lambda i, j: (i, j, ucol + 1)),
                  _layer_spec(g, layer), _layer_spec(w_sp, layer), _layer_spec(b_rows, layer)],
        out_specs=pl.BlockSpec((1, rows, SGU_WIDTH), lambda i, j: (i, j, 0)),
        out_shape=jax.ShapeDtypeStruct((n, seq, SGU_WIDTH), F32),
        compiler_params=_params("parallel", "parallel"),
        name="sgu_prompt",
    )(z3, z3, g, w_sp, b_rows)


def _sgu_sample_kernel(u_ref, v_ref, g_ref, w_ref, b_ref, o_ref, gv_ref):
    c = SGU_CHUNK
    e = SGU_HEAD_DIM
    t = u_ref.shape[1]
    w = _sgu_causal(w_ref)
    gu = jax.nn.gelu(u_ref[0])
    gv = _sgu_norm_v(v_ref[0], g_ref[...])
    gv_ref[0] = gv
    gvp = jnp.concatenate([gv, jnp.zeros((c - t, SGU_WIDTH), F32)], axis=0).astype(BF16)
    for h in range(SGU_HEADS):
        mixed = jnp.dot(w[h], gvp[:, h * e:(h + 1) * e], preferred_element_type=F32) + b_ref[h]
        o_ref[0, :, h * e:(h + 1) * e] = gu[:, h * e:(h + 1) * e] * mixed[:t]


def sgu_sample(zs3, layer, g, w_sp, b_rows):
    n, t, _ = zs3.shape
    ucol = (3 * ATT_WIDTH + SSM_WIDTH) // SGU_WIDTH
    return pl.pallas_call(
        _sgu_sample_kernel,
        grid=(n,),
        in_specs=[pl.BlockSpec((1, t, SGU_WIDTH), lambda i: (i, 0, ucol)),
                  pl.BlockSpec((1, t, SGU_WIDTH), lambda i: (i, 0, ucol + 1)),
                  _layer_spec(g, layer), _layer_spec(w_sp, layer), _layer_spec(b_rows, layer)],
        out_specs=[pl.BlockSpec((1, t, SGU_WIDTH), lambda i: (i, 0, 0)),
                   pl.BlockSpec((1, t, SGU_WIDTH), lambda i: (i, 0, 0))],
        out_shape=[jax.ShapeDtypeStruct((n, t, SGU_WIDTH), F32),
                   jax.ShapeDtypeStruct((n, t, SGU_WIDTH), F32)],
        compiler_params=_params("parallel"),
        name="sgu_sample",
    )(zs3, zs3, g, w_sp, b_rows)


def _out_proj_kernel(att_ref, ssm_ref, sgu_ref, x_ref, w_ref, gm_ref, gp_ref, o_ref):
    a0 = ATT_WIDTH
    a1 = ATT_WIDTH + SSM_WIDTH
    a = _rms(att_ref[...], gm_ref[:, :a0]).astype(BF16)
    s = _rms(ssm_ref[...], gm_ref[:, a0:a1]).astype(BF16)
    c = _rms(sgu_ref[...], gm_ref[:, a1:]).astype(BF16)
    mixed = jnp.concatenate([a, s, c], axis=1)
    y = jnp.dot(mixed, w_ref[...], preferred_element_type=F32)
    o_ref[...] = x_ref[...] + _rms(y, gp_ref[...])


def out_proj(att, ssm, sgu, x, w, layer, gm, gp, tm):
    m, dm = x.shape
    row = lambda width: pl.BlockSpec((tm, width), lambda i: (i, 0))
    return pl.pallas_call(
        _out_proj_kernel,
        grid=(m // tm,),
        in_specs=[row(ATT_WIDTH), row(SSM_WIDTH), row(SGU_WIDTH), row(dm),
                  pl.BlockSpec(w.shape, lambda i: (0, 0)), _layer_spec(gm, layer), _layer_spec(gp, layer)],
        out_specs=row(dm),
        out_shape=jax.ShapeDtypeStruct((m, dm), F32),
        compiler_params=_params("parallel"),
        name="out_proj",
    )(att, ssm, sgu, x, w, gm, gp)


def _ffn_kernel(x_ref, gpre_ref, wg_ref, wu_ref, wd_ref, gpost_ref, *rest, n_cast):
    w32_refs, o_ref = rest[:n_cast], rest[n_cast]
    w16_refs, h_ref = rest[n_cast + 1:2 * n_cast + 1], rest[-1]
    j = pl.program_id(1)
    last = pl.num_programs(1) - 1
    tm = x_ref.shape[0]
    chunk = min(tm, FFN_EDGE_ROWS)

    def partial_out(h):
        gate = jnp.dot(h, wg_ref[...], preferred_element_type=F32)
        up = jnp.dot(h, wu_ref[...], preferred_element_type=F32)
        act = (jax.nn.silu(gate) * up).astype(BF16)
        return jnp.dot(act, wd_ref[...], preferred_element_type=F32)

    def cast_side_stream():
        for src_ref, dst_ref in zip(w32_refs, w16_refs):
            dst_ref[...] = src_ref[...].astype(BF16)

    @pl.when(j == 0)
    def _():
        for r in range(0, tm, chunk):
            rows = pl.ds(r, chunk)
            h = _rms(x_ref[rows, :], gpre_ref[...]).astype(BF16)
            h_ref[rows, :] = h
            o_ref[rows, :] = partial_out(h)
        cast_side_stream()

    @pl.when((j > 0) & (j < last))
    def _():
        o_ref[...] += partial_out(h_ref[...])
        cast_side_stream()

    @pl.when(j == last)
    def _():
        for r in range(0, tm, chunk):
            rows = pl.ds(r, chunk)
            f = o_ref[rows, :] + partial_out(h_ref[rows, :])
            o_ref[rows, :] = x_ref[rows, :] + _rms(f, gpost_ref[...])
        cast_side_stream()


def ffn(x, gpre, wg, wu, wd, layer, gpost, tm, tf, cast=()):
    m, dm = x.shape
    f = wg.shape[1]
    gi, gj = m // tm, f // tf
    assert gj >= 2
    cb = CAST_BLOCK
    cast_in, cast_out, cast_shape = [], [], []
    for w32, lyr, rows_follow_i in cast:
        _, r, c = w32.shape
        nr, nc = r // cb, c // cb
        assert r % cb == 0 and c % cb == 0
        if rows_follow_i:
            assert nr == gi and nc <= gj
            imap = lambda i, j, nc=nc: (i, jnp.minimum(j, nc - 1))
        else:
            assert nc == gi and nr <= gj
            imap = lambda i, j, nr=nr: (jnp.minimum(j, nr - 1), i)
        cast_in.append(pl.BlockSpec((None, cb, cb), lambda i, j, imap=imap, lyr=lyr: (lyr,) + imap(i, j)))
        cast_out.append(pl.BlockSpec((cb, cb), imap))
        cast_shape.append(jax.ShapeDtypeStruct((r, c), BF16))
    kern = functools.partial(_ffn_kernel, n_cast=len(cast))
    res = pl.pallas_call(
        kern,
        grid=(gi, gj),
        in_specs=[pl.BlockSpec((tm, dm), lambda i, j: (i, 0)),
                  _layer_spec(gpre, layer),
                  pl.BlockSpec((dm, tf), lambda i, j: (0, j)),
                  pl.BlockSpec((dm, tf), lambda i, j: (0, j)),
                  pl.BlockSpec((tf, dm), lambda i, j: (j, 0)),
                  _layer_spec(gpost, layer)] + cast_in,
        out_specs=[pl.BlockSpec((tm, dm), lambda i, j: (i, 0))] + cast_out,
        out_shape=[jax.ShapeDtypeStruct((m, dm), F32)] + cast_shape,
        scratch_shapes=[pltpu.VMEM((tm, dm), BF16)],
        compiler_params=_params("arbitrary", "arbitrary"),
        name="ffn_cast" if cast else "ffn",
    )(x, gpre, wg, wu, wd, gpost, *[w for w, _, _ in cast])
    return res[0], res[1:]


def _t5_bucket(dist):
    max_exact = N_REL_BUCKETS // 2
    df = jnp.maximum(dist, 1).astype(F32)
    large = max_exact + (jnp.log(df / max_exact) / math.log(REL_MAX_DIST / max_exact)
                         * (N_REL_BUCKETS - max_exact)).astype(jnp.int32)
    large = jnp.minimum(large, N_REL_BUCKETS - 1)
    return jnp.where(dist < max_exact, dist, large)


def _strided_bias(rel_bias, dilation, n_steps):
    dist = jnp.arange(n_steps + 1, dtype=jnp.int32) * dilation
    return rel_bias[_t5_bucket(dist)].astype(F32)


def _bias_lookup(sb, steps, ok):
    onehot = (steps[..., None] == jnp.arange(sb.shape[0])).astype(F32)
    vals = jnp.einsum('...j,jh->h...', onehot, sb, precision=lax.Precision.HIGHEST)
    return jnp.where(ok[None], vals, NEG_INF)


def _prompt_bias_tables(rel_bias):
    qb = ATT_BLOCK
    nc = ATT_CLASSES
    tables = []
    for window, d in DILATED_PATTERNS:
        nk = window // d
        sb = _strided_bias(rel_bias, d, nk)
        qi = jnp.arange(qb)[:, None]
        kj = jnp.arange(2 * qb)[None, :]
        if d == 1:
            qi = (qi % (qb // nc)) * nc + qi // (qb // nc)
            kj = (kj % (2 * qb // nc)) * nc + kj // (2 * qb // nc)
        delta = qb + qi - kj
        normal = _bias_lookup(sb, jnp.clip(delta, 0, nk), (delta >= 0) & (delta <= nk))
        delta_f = qi - kj
        first = _bias_lookup(sb, jnp.clip(delta_f, 0, nk), (delta_f >= 0) & (delta_f <= nk))
        tables.append(jnp.stack([normal, first]))
    return jnp.stack(tables).astype(F32)


def _sample_bias_tables(rel_bias, w, t):
    tq = jnp.arange(t)[:, None]
    cache_delta = w + tq - jnp.arange(w)[None, :]
    new_delta = tq - jnp.arange(ATT_BLOCK)[None, :]
    out_c, out_n = [], []
    for window, d in DILATED_PATTERNS:
        nk = window // d
        sb = _strided_bias(rel_bias, d, nk)

        def table(delta, extra_ok):
            ok = (delta >= 0) & (delta % d == 0) & (delta <= nk * d) & extra_ok
            return _bias_lookup(sb, jnp.clip(delta // d, 0, nk), ok)

        out_c.append(table(cache_delta, True))
        out_n.append(table(new_delta, jnp.arange(ATT_BLOCK)[None, :] < t))
    return jnp.stack(out_c).astype(F32), jnp.stack(out_n).astype(F32)


def _ssm_tables(lam_re, lam_im, log_step, b_re, b_im, c_re, c_im):
    g, p, cg = SSM_GROUPS, SSM_STATE, SSM_GROUP
    nq = SSM_QUARTERS
    gq = g // nq
    step = jnp.exp(log_step)[:, None]
    mag = jnp.exp(lam_re * step)
    a_re = mag * jnp.cos(lam_im * step)
    a_im = mag * jnp.sin(lam_im * step)
    den = lam_re * lam_re + lam_im * lam_im
    coef_re = ((a_re - 1.0) * lam_re + a_im * lam_im) / den
    coef_im = (a_im * lam_re - (a_re - 1.0) * lam_im) / den
    bb_re = coef_re[..., None] * b_re - coef_im[..., None] * b_im
    bb_im = coef_re[..., None] * b_im + coef_im[..., None] * b_re
    eye = jnp.eye(gq, dtype=F32)

    def in_mat(bb):
        bb = bb.reshape(nq, gq, p, cg)
        return jnp.einsum('kgpc,gh->kgchp', bb, eye).reshape(nq, gq * cg, gq * p)

    def out_mat(c):
        c = c.reshape(nq, gq, cg, p)
        return jnp.einsum('kgcp,gh->kgphc', c, eye).reshape(nq, gq * p, gq * cg)

    bq = jnp.concatenate([in_mat(bb_re), in_mat(bb_im)], axis=2)
    cq = jnp.concatenate([out_mat(c_re), -out_mat(c_im)], axis=1)
    aq = jnp.concatenate([a_re.reshape(nq, gq * p), a_im.reshape(nq, gq * p)], axis=1)
    a8 = jnp.concatenate([aq, aq], axis=0)
    return bq, cq, a8


def _rows(v):
    return v.reshape(v.shape[0], 1, -1)


def kernel(x_prompt, x_sample, cache_attn_k, cache_attn_v, state_ssm_re, state_ssm_im, rel_bias, w_in, w_out, g_pre_mix, g_post_mix, g_mix_out, ssm_lam_re, ssm_lam_im, ssm_log_step, ssm_b_re, ssm_b_im, ssm_c_re, ssm_c_im, ssm_d, ssm_w_glu, ssm_b_glu, sgu_g, sgu_w, sgu_b, g_pre_ffn, g_post_ffn, w_gate, w_up, w_down):
    nb, seq, dm = x_prompt.shape
    ns, ts, _ = x_sample.shape
    wbuf = cache_attn_k.shape[2]
    n_keep = min(DILATED_PATTERNS[-1][0], seq)
    h, e = ATT_HEADS, ATT_HEAD_DIM

    bias_p = _prompt_bias_tables(rel_bias)
    bias_c, bias_n = _sample_bias_tables(rel_bias, wbuf, ts)
    cache_k = cache_attn_k.reshape(DEPTH, ns, wbuf * h, e)
    cache_v = cache_attn_v.reshape(DEPTH, ns, wbuf * h, e)
    k_buf = jnp.zeros((DEPTH, nb, n_keep * h, e), F32)
    v_buf = jnp.zeros((DEPTH, nb, n_keep * h, e), F32)
    st_re = state_ssm_re.reshape(DEPTH, ns, SSM_GROUPS * SSM_STATE)
    st_im = state_ssm_im.reshape(DEPTH, ns, SSM_GROUPS * SSM_STATE)

    xp = x_prompt.reshape(nb * seq, dm)
    xs = x_sample.reshape(ns * ts, dm)
    outs = {k: [] for k in ('rp', 'ip', 'ks', 'vs', 'rs', 'is', 'us')}

    big = (w_in, w_out, w_gate, w_up, w_down)
    big16 = tuple(w[0].astype(BF16) for w in big)
    w_glu16 = ssm_w_glu.astype(BF16)
    bq, cq, a8 = jax.vmap(_ssm_tables)(ssm_lam_re, ssm_lam_im, ssm_log_step,
                                       ssm_b_re, ssm_b_im, ssm_c_re, ssm_c_im)
    bq16, cq16 = bq.astype(BF16), cq.astype(BF16)
    sgu_b_rows = jnp.broadcast_to(sgu_b[..., None], (DEPTH, SGU_HEADS, SGU_CHUNK, SGU_HEAD_DIM))
    g_pre, g_post, g_mix = _rows(g_pre_mix), _rows(g_post_mix), _rows(g_mix_out)
    g_pre_f, g_post_f = _rows(g_pre_ffn), _rows(g_post_ffn)
    d_skip, b_glu, g_sgu = _rows(ssm_d), _rows(ssm_b_glu), _rows(sgu_g)

    for l in range(DEPTH):
        w_in16, w_out16, w_gate16, w_up16, w_down16 = big16
        cast = [(w, l + 1, w is not w_down) for w in big] if l + 1 < DEPTH else []

        z, k_buf, v_buf = in_proj(xp, g_pre, w_in16, l, tm=1024, tn=768,
                                  kv=(seq, n_keep, k_buf, v_buf))
        z = z.reshape(nb, seq, IN_COLS)
        o_att = attn_prompt(z, bias_p)
        o_ssm, h_fin = ssm_prompt(z, l, bq16, a8, cq16, d_skip, w_glu16, b_glu)
        o_sgu = sgu_prompt(z, l, g_sgu, sgu_w, sgu_b_rows)
        xp = out_proj(o_att.reshape(nb * seq, ATT_WIDTH), o_ssm.reshape(nb * seq, SSM_WIDTH),
                      o_sgu.reshape(nb * seq, SGU_WIDTH), xp, w_out16, l, g_mix, g_post, tm=512)
        xp, next16 = ffn(xp, g_pre_f, w_gate16, w_up16, w_down16, l, g_post_f, tm=1024, tf=256,
                         cast=cast)
        hq = h_fin.reshape(nb, SSM_QUARTERS, 2, SSM_GROUPS // SSM_QUARTERS, SSM_STATE)
        outs['rp'].append(hq[:, :, 0].reshape(nb, SSM_GROUPS, SSM_STATE))
        outs['ip'].append(hq[:, :, 1].reshape(nb, SSM_GROUPS, SSM_STATE))

        zs = in_proj(xs, g_pre, w_in16, l, tm=ns * ts, tn=768).reshape(ns, ts, IN_COLS)
        s_att = attn_sample(zs, cache_k, cache_v, l, bias_c, bias_n)
        s_ssm, s_re, s_im = ssm_sample(zs, l, st_re, st_im, bq, a8, cq, d_skip, w_glu16, b_glu)
        s_sgu, s_gv = sgu_sample(zs, l, g_sgu, sgu_w, sgu_b_rows)
        xs = out_proj(s_att.reshape(ns * ts, ATT_WIDTH), s_ssm.reshape(ns * ts, SSM_WIDTH),
                      s_sgu.reshape(ns * ts, SGU_WIDTH), xs, w_out16, l, g_mix, g_post, tm=ns * ts)
        xs, _ = ffn(xs, g_pre_f, w_gate16, w_up16, w_down16, l, g_post_f, tm=ns * ts, tf=512)
        big16 = tuple(next16) or big16
        outs['ks'].append(zs[:, :, ATT_WIDTH:2 * ATT_WIDTH].reshape(ns, ts, h, e))
        outs['vs'].append(zs[:, :, 2 * ATT_WIDTH:3 * ATT_WIDTH].reshape(ns, ts, h, e))
        outs['rs'].append(s_re.reshape(ns, SSM_GROUPS, SSM_STATE))
        outs['is'].append(s_im.reshape(ns, SSM_GROUPS, SSM_STATE))
        outs['us'].append(s_gv)

    st = lambda key: jnp.stack(outs[key])
    kv_shape = (DEPTH, nb, n_keep, h, e)
    return (xp.reshape(nb, seq, dm), xs.reshape(ns, ts, dm),
            k_buf.reshape(kv_shape), v_buf.reshape(kv_shape), st('rp'), st('ip'),
            st('ks'), st('vs'), st('rs'), st('is'), st('us'))
```

```python
import functools
import math

import jax
import jax.numpy as jnp
from jax import lax
from jax.experimental import pallas as pl
from jax.experimental.pallas import tpu as pltpu

F32 = jnp.float32
BF16 = jnp.bfloat16

D_MODEL = 2048
DEPTH = 4
ATT_HEADS = 8
ATT_HEAD_DIM = 128
ATT_WIDTH = ATT_HEADS * ATT_HEAD_DIM
DILATED_PATTERNS = ((128, 1), (512, 4), (2048, 16))
ATT_BLOCK = 128
ATT_GROUP = (8, 8, 4)
ATT_SCALE = ATT_HEAD_DIM ** -0.5
N_REL_BUCKETS = 32
REL_MAX_DIST = 2048
SSM_WIDTH = 512
SSM_GROUP = 16
SSM_GROUPS = SSM_WIDTH // SSM_GROUP
SSM_STATE = 64
SSM_QUARTERS = 4
SSM_QW = SSM_GROUPS * SSM_STATE // SSM_QUARTERS
SSM_QC = SSM_WIDTH // SSM_QUARTERS
SGU_WIDTH = 512
SGU_HEADS = 4
SGU_HEAD_DIM = SGU_WIDTH // SGU_HEADS
SGU_CHUNK = 128
IN_COLS = 3 * ATT_WIDTH + SSM_WIDTH + 2 * SGU_WIDTH
FFN_HIDDEN = 5632
EPS = 1e-6
NEG_INF = -1e30
LANES = 128
CAST_BLOCK = 256
FFN_EDGE_ROWS = 256

VMEM_LIMIT_BYTES = 56 * 1024 * 1024


def _params(*sem):
    return pltpu.CompilerParams(dimension_semantics=sem, vmem_limit_bytes=VMEM_LIMIT_BYTES)


def _layer_spec(arr, layer):
    zeros = (0,) * (arr.ndim - 1)
    return pl.BlockSpec((None,) + arr.shape[1:], lambda *_: (layer,) + zeros)


def _rms(x, g):
    return x * lax.rsqrt(jnp.mean(x * x, axis=-1, keepdims=True) + EPS) * g


def _in_proj_kernel(x_ref, g_ref, w_ref, *rest, keep_from, tiles_per_seq):
    h_ref = rest[-1]
    outs = rest[-4:-1] if len(rest) > 2 else rest[:1]
    z_ref = outs[0]
    i = pl.program_id(0)
    j = pl.program_id(1)

    tm, tn = z_ref.shape
    chunk = min(tm, FFN_EDGE_ROWS)

    @pl.when(j == 0)
    def _():
        for r in range(0, tm, chunk):
            rows = pl.ds(r, chunk)
            h = _rms(x_ref[rows, :], g_ref[...]).astype(BF16)
            h_ref[rows, :] = h
            z_ref[rows, :] = jnp.dot(h, w_ref[...], preferred_element_type=F32)

    @pl.when(j > 0)
    def _():
        _in_proj_tile(i, j, h_ref, w_ref, outs, keep_from, tiles_per_seq)


def _in_proj_tile(i, j, h_ref, w_ref, outs, keep_from, tiles_per_seq):
    z_ref = outs[0]
    res = jnp.dot(h_ref[...], w_ref[...], preferred_element_type=F32)
    z_ref[...] = res

    if len(outs) > 1:
        tm, tn = res.shape
        keep = (i % tiles_per_seq) >= keep_from
        by_tile = {}
        for buf_ref, col0 in ((outs[1], ATT_WIDTH), (outs[2], 2 * ATT_WIDTH)):
            for head in range(ATT_HEADS):
                col = col0 + head * ATT_HEAD_DIM
                by_tile.setdefault(col // tn, []).append((buf_ref, head, col % tn))
        for tile, heads in by_tile.items():
            @pl.when(keep & (j == tile))
            def _(heads=heads):
                for buf_ref, head, off in heads:
                    rows = pl.ds(head, tm, stride=ATT_HEADS)
                    buf_ref[0, 0, rows, :] = res[:, off:off + ATT_HEAD_DIM]


def in_proj(x, g, w, layer, tm, tn, kv=None):
    m, k = x.shape
    n = w.shape[1]
    assert tn % ATT_HEAD_DIM == 0
    in_specs = [pl.BlockSpec((tm, k), lambda i, j: (i, 0)),
                _layer_spec(g, layer),
                pl.BlockSpec((k, tn), lambda i, j: (0, j))]
    z_spec = pl.BlockSpec((tm, tn), lambda i, j: (i, j))
    z_shape = jax.ShapeDtypeStruct((m, n), F32)
    scratch = [pltpu.VMEM((tm, k), BF16)]
    if kv is None:
        kern = functools.partial(_in_proj_kernel, keep_from=0, tiles_per_seq=1)
        return pl.pallas_call(
            kern, grid=(m // tm, n // tn), in_specs=in_specs, out_specs=z_spec, out_shape=z_shape,
            scratch_shapes=scratch, compiler_params=_params("parallel", "arbitrary"),
            name="in_proj",
        )(x, g, w)
    seq, n_keep, k_buf, v_buf = kv
    tiles_per_seq = seq // tm
    keep_from = (seq - n_keep) // tm
    assert seq % tm == 0 and (seq - n_keep) % tm == 0
    assert tn <= ATT_WIDTH

    def buf_map(i, j):
        return (layer, i // tiles_per_seq, jnp.maximum(i % tiles_per_seq - keep_from, 0), 0)

    buf_spec = pl.BlockSpec((1, 1, tm * ATT_HEADS, ATT_HEAD_DIM), buf_map)
    any_spec = pl.BlockSpec(memory_space=pl.ANY)
    kern = functools.partial(_in_proj_kernel, keep_from=keep_from, tiles_per_seq=tiles_per_seq)
    return pl.pallas_call(
        kern, grid=(m // tm, n // tn),
        in_specs=in_specs + [any_spec, any_spec],
        out_specs=[z_spec, buf_spec, buf_spec],
        out_shape=[z_shape, jax.ShapeDtypeStruct(k_buf.shape, F32),
                   jax.ShapeDtypeStruct(v_buf.shape, F32)],
        input_output_aliases={3: 1, 4: 2},
        scratch_shapes=scratch, compiler_params=_params("arbitrary", "arbitrary"),
        name="in_proj_kv",
    )(x, g, w, k_buf, v_buf)


def _attn_prompt_kernel(q_ref, k_ref, v_ref, b_ref, o_ref, acc_ref, m_ref, l_ref, *, seq):
    qb = ATT_BLOCK
    n_pat = len(DILATED_PATTERNS)

    for p, (_, d) in enumerate(DILATED_PATTERNS):
        shift = int(math.log2(d))
        whole = seq // d == 2 * qb
        blocks_per_trip = ATT_GROUP[p]
        units = blocks_per_trip // 2 if whole else blocks_per_trip

        def group(g, carry, p=p, d=d, shift=shift, whole=whole, units=units):
            rows, biases = [], []
            for u in range(units):
                i = g * units + u
                if whole:
                    rows.append((pl.ds(i, 2 * qb, stride=d), pl.ds(i, 2 * qb, stride=d)))
                    biases.append(None)
                    continue
                r = i & (d - 1)
                b = i >> shift
                q_start = r + b * (qb * d)
                k_start = r + jnp.maximum(b - 1, 0) * (qb * d)
                if d == 1:
                    rows.append((pl.ds(pl.multiple_of(q_start, qb), qb),
                                 pl.ds(pl.multiple_of(k_start, qb), 2 * qb)))
                else:
                    rows.append((pl.ds(q_start, qb, stride=d), pl.ds(k_start, 2 * qb, stride=d)))
                biases.append((b == 0).astype(jnp.int32))
            nt = (((1,), (1,)), ((), ()))
            scores = [lax.dot_general(q_ref[0, qr, :].astype(BF16), k_ref[0, kr, :].astype(BF16), nt,
                                      preferred_element_type=F32) for qr, kr in rows]
            probs = []
            for (qr, _), first, s in zip(rows, biases, scores):
                if whole:
                    bias = jnp.concatenate([b_ref[p, 1, 0], b_ref[p, 0, 0]], axis=0)
                else:
                    bias = b_ref[p, first, 0]
                s = s * ATT_SCALE + bias
                m_blk = jnp.max(s, axis=1, keepdims=True)
                pe = jnp.exp(s - m_blk)
                wide = (s.shape[0], ATT_HEAD_DIM)
                m_ref[p, qr, :] = jnp.broadcast_to(m_blk, wide)
                l_ref[p, qr, :] = jnp.broadcast_to(jnp.sum(pe, axis=1, keepdims=True), wide)
                probs.append(pe.astype(BF16))
            for (qr, kr), pe in zip(rows, probs):
                acc_ref[p, qr, :] = jnp.dot(pe, v_ref[0, kr, :].astype(BF16),
                                            preferred_element_type=F32)
            return carry

        lax.fori_loop(0, seq // qb // blocks_per_trip, group, 0)

    rows_per = 2 * qb

    def merge(c, carry):
        rows = pl.ds(pl.multiple_of(c * rows_per, rows_per), rows_per)
        ms = [m_ref[p, rows, :] for p in range(n_pat)]
        m = functools.reduce(jnp.maximum, ms)
        num = den = None
        for p in range(n_pat):
            w = jnp.exp(ms[p] - m)
            num = w * acc_ref[p, rows, :] if num is None else num + w * acc_ref[p, rows, :]
            den = w * l_ref[p, rows, :] if den is None else den + w * l_ref[p, rows, :]
        o_ref[0, rows, :] = num / den
        return carry

    lax.fori_loop(0, seq // rows_per, merge, 0)


def attn_prompt(z3, bias):
    n, seq, _ = z3.shape
    h = ATT_HEADS
    kern = functools.partial(_attn_prompt_kernel, seq=seq)
    return pl.pallas_call(
        kern,
        grid=(n, h),
        in_specs=[pl.BlockSpec((1, seq, ATT_HEAD_DIM), lambda i, j: (i, 0, j)),
                  pl.BlockSpec((1, seq, ATT_HEAD_DIM), lambda i, j: (i, 0, h + j)),
                  pl.BlockSpec((1, seq, ATT_HEAD_DIM), lambda i, j: (i, 0, 2 * h + j)),
                  pl.BlockSpec((3, 2, 1, ATT_BLOCK, 2 * ATT_BLOCK), lambda i, j: (0, 0, j, 0, 0))],
        out_specs=pl.BlockSpec((1, seq, ATT_HEAD_DIM), lambda i, j: (i, 0, j)),
        out_shape=jax.ShapeDtypeStruct((n, seq, ATT_WIDTH), F32),
        scratch_shapes=[pltpu.VMEM((len(DILATED_PATTERNS), seq, ATT_HEAD_DIM), F32)] * 3,
        compiler_params=_params("parallel", "parallel"),
        name="attn_prompt",
    )(z3, z3, z3, bias)


def _attn_sample_kernel(q_ref, kn_ref, vn_ref, ko_ref, vo_ref, kt_ref, vt_ref,
                        bo_ref, bt_ref, bn_ref, o_ref):
    t = q_ref.shape[1]
    e = ATT_HEAD_DIM
    nh = ATT_HEADS
    n_pat = len(DILATED_PATTERNS)
    groups, kept_rows, _ = ko_ref.shape
    kept = kept_rows // nh
    tail = kt_ref.shape[0] // nh
    pad = jnp.zeros((ATT_BLOCK - t, e), F32)
    nt = (((1,), (1,)), ((), ()))
    heads = range(nh)
    cols = [slice(h * e, (h + 1) * e) for h in heads]

    def old_rows(ref, h):
        return ref[:, pl.ds(h, kept, stride=nh), :].reshape(groups * kept, e).astype(BF16)

    def tail_rows(ref, h):
        return ref[pl.ds(h, tail, stride=nh), :].astype(BF16)

    so, st, sn = [], [], []
    for h in heads:
        q = q_ref[0, :, cols[h]].astype(BF16)
        kn = jnp.concatenate([kn_ref[0, :, cols[h]], pad], axis=0).astype(BF16)
        so.append(lax.dot_general(q, old_rows(ko_ref, h), nt, preferred_element_type=F32) * ATT_SCALE)
        st.append(lax.dot_general(q, tail_rows(kt_ref, h), nt, preferred_element_type=F32) * ATT_SCALE)
        sn.append(lax.dot_general(q, kn, nt, preferred_element_type=F32) * ATT_SCALE)
    po, pt, pn, ls = [], [], [], []
    for h in heads:
        s_old = so[h] + bo_ref[h]
        sts = [st[h] + bt_ref[p, h] for p in range(n_pat)]
        sns = [sn[h] + bn_ref[p, h] for p in range(n_pat)]
        m = None
        for s in [s_old] + sts + sns:
            mx = jnp.max(s, axis=1, keepdims=True)
            m = mx if m is None else jnp.maximum(m, mx)
        poh = jnp.exp(s_old - m)
        pth = sum(jnp.exp(s - m) for s in sts)
        pnh = sum(jnp.exp(s - m) for s in sns)
        ls.append(jnp.sum(poh, axis=1, keepdims=True) + jnp.sum(pth, axis=1, keepdims=True)
                  + jnp.sum(pnh, axis=1, keepdims=True))
        po.append(poh.astype(BF16))
        pt.append(pth.astype(BF16))
        pn.append(pnh.astype(BF16))
    for h in heads:
        vn = jnp.concatenate([vn_ref[0, :, cols[h]], pad], axis=0).astype(BF16)
        acc = (jnp.dot(po[h], old_rows(vo_ref, h), preferred_element_type=F32)
               + jnp.dot(pt[h], tail_rows(vt_ref, h), preferred_element_type=F32)
               + jnp.dot(pn[h], vn, preferred_element_type=F32))
        o_ref[0, :, cols[h]] = acc / ls[h]


def _sample_cache_split(w, t):
    d_far = DILATED_PATTERNS[-1][1]
    tail = max(window for window, _ in DILATED_PATTERNS[:-1])
    assert t <= d_far and w % d_far == 0 and tail % d_far == 0 and tail <= w
    return d_far, tail, (w - tail) // d_far


def attn_sample(zs3, cache_k, cache_v, layer, bias_o, bias_t, bias_n):
    n, t, _ = zs3.shape
    h = ATT_HEADS
    e = ATT_HEAD_DIM
    depth, _, w = cache_k.shape[:3]
    d_far, tail, groups = _sample_cache_split(w, t)
    by_group = lambda c: c.reshape(depth, n, w // d_far, d_far * h, e)
    by_tail = lambda c: c.reshape(depth, n, w // tail, tail * h, e)
    old_spec = pl.BlockSpec((None, None, groups, t * h, e), lambda i: (layer, i, 0, 0, 0))
    tail_spec = pl.BlockSpec((None, None, None, tail * h, e), lambda i: (layer, i, w // tail - 1, 0, 0))
    return pl.pallas_call(
        _attn_sample_kernel,
        grid=(n,),
        in_specs=[pl.BlockSpec((1, t, ATT_WIDTH), lambda i: (i, 0, 0)),
                  pl.BlockSpec((1, t, ATT_WIDTH), lambda i: (i, 0, 1)),
                  pl.BlockSpec((1, t, ATT_WIDTH), lambda i: (i, 0, 2)),
                  old_spec, old_spec, tail_spec, tail_spec,
                  pl.BlockSpec(bias_o.shape, lambda i: (0, 0, 0)),
                  pl.BlockSpec(bias_t.shape, lambda i: (0, 0, 0, 0)),
                  pl.BlockSpec(bias_n.shape, lambda i: (0, 0, 0, 0))],
        out_specs=pl.BlockSpec((1, t, ATT_WIDTH), lambda i: (i, 0, 0)),
        out_shape=jax.ShapeDtypeStruct((n, t, ATT_WIDTH), F32),
        compiler_params=_params("parallel"),
        name="attn_sample",
    )(zs3, zs3, zs3, by_group(cache_k), by_group(cache_v), by_tail(cache_k), by_tail(cache_v),
      bias_o, bias_t, bias_n)


def _ssm_prompt_kernel(u_ref, bq_ref, a_ref, cq_ref, d_ref, wg_ref, bg_ref,
                       o_ref, hfin_ref, x_ref, h_ref, *, tb, pitch):
    step = pl.program_id(0)
    nq = SSM_QUARTERS
    qw = SSM_QW
    qc = SSM_QC

    @pl.when(step == 0)
    def _():
        h_ref[...] = jnp.zeros_like(h_ref)

    nl = 2 * qw // LANES
    hl = nl // 2

    for b in range(2):
        for k in range(nq):
            u = u_ref[b, :, k * qc:(k + 1) * qc].astype(BF16)
            x = jnp.dot(u, bq_ref[k], preferred_element_type=F32)
            for c in range(nl):
                x_ref[c, pl.ds((b * nq + k) * pitch, tb), :] = x[:, c * LANES:(c + 1) * LANES]

    a = [a_ref[:, c * LANES:(c + 1) * LANES] for c in range(nl)]

    def scan(t, h):
        rows = pl.ds(t, 2 * nq, stride=pitch)
        new = [None] * nl
        for c in range(hl):
            a_re, a_im, h_re, h_im = a[c], a[hl + c], h[c], h[hl + c]
            new[c] = a_re * h_re - a_im * h_im + x_ref[c, rows, :]
            new[hl + c] = a_re * h_im + a_im * h_re + x_ref[hl + c, rows, :]
        for c in range(nl):
            x_ref[c, rows, :] = new[c]
        return tuple(new)

    h0 = tuple(h_ref[:, c * LANES:(c + 1) * LANES] for c in range(nl))
    h_new = jnp.concatenate(lax.fori_loop(0, tb, scan, h0, unroll=8), axis=1)
    h_ref[...] = h_new

    for b in range(2):
        ys = []
        for k in range(nq):
            rows = pl.ds((b * nq + k) * pitch, tb)
            hs = jnp.concatenate([x_ref[c, rows, :] for c in range(nl)], axis=1).astype(BF16)
            ys.append(jnp.dot(hs, cq_ref[k], preferred_element_type=F32))
        y = jnp.concatenate(ys, axis=1) + d_ref[...] * u_ref[b]
        gate = jnp.dot(y.astype(BF16), wg_ref[...], preferred_element_type=F32) + bg_ref[...]
        o_ref[b] = y * jax.nn.sigmoid(gate)

    @pl.when(step == pl.num_programs(0) - 1)
    def _():
        hfin_ref[...] = h_new


def ssm_prompt(z3, layer, bq, a8, cq, d_skip, w_glu, b_glu, tb=512):
    n, seq, _ = z3.shape
    assert n == 2
    pitch = tb + 8
    ucol = 3 * ATT_WIDTH // SSM_WIDTH
    kern = functools.partial(_ssm_prompt_kernel, tb=tb, pitch=pitch)
    full = lambda *shape: pl.BlockSpec(shape, lambda s: (0,) * len(shape))
    return pl.pallas_call(
        kern,
        grid=(seq // tb,),
        in_specs=[pl.BlockSpec((n, tb, SSM_WIDTH), lambda s: (0, s, ucol))]
        + [_layer_spec(p, layer) for p in (bq, a8, cq, d_skip, w_glu, b_glu)],
        out_specs=[pl.BlockSpec((n, tb, SSM_WIDTH), lambda s: (0, s, 0)),
                   full(2 * SSM_QUARTERS, 2 * SSM_QW)],
        out_shape=[jax.ShapeDtypeStruct((n, seq, SSM_WIDTH), F32),
                   jax.ShapeDtypeStruct((2 * SSM_QUARTERS, 2 * SSM_QW), F32)],
        scratch_shapes=[pltpu.VMEM((2 * SSM_QW // LANES, 2 * SSM_QUARTERS * pitch, LANES), F32),
                        pltpu.VMEM((2 * SSM_QUARTERS, 2 * SSM_QW), F32)],
        compiler_params=_params("arbitrary"),
        name="ssm_prompt",
    )(z3, bq, a8, cq, d_skip, w_glu, b_glu)


def _ssm_sample_kernel(u_ref, hre_ref, him_ref, bq_ref, a_ref, cq_ref, d_ref, wg_ref, bg_ref,
                       o_ref, ore_ref, oim_ref, x_ref):
    n, t, _ = u_ref.shape
    nq = SSM_QUARTERS
    qw = SSM_QW
    qc = SSM_QC
    nl = 2 * qw // LANES
    hl = nl // 2
    u_all = u_ref[...].reshape(n * t, SSM_WIDTH)
    ys = []
    for k in range(nq):
        u = u_all[:, k * qc:(k + 1) * qc]
        x = jnp.dot(u, bq_ref[k], preferred_element_type=F32, precision=lax.Precision.HIGHEST)
        for c in range(nl):
            x_ref[c] = x[:, c * LANES:(c + 1) * LANES]
        h = ([hre_ref[:, k * qw + c * LANES:k * qw + (c + 1) * LANES] for c in range(hl)]
             + [him_ref[:, k * qw + c * LANES:k * qw + (c + 1) * LANES] for c in range(hl)])
        for s in range(t):
            rows = pl.ds(s, n, stride=t)
            new = [None] * nl
            for c in range(hl):
                a_re = a_ref[k:k + 1, c * LANES:(c + 1) * LANES]
                a_im = a_ref[k:k + 1, (hl + c) * LANES:(hl + c + 1) * LANES]
                new[c] = a_re * h[c] - a_im * h[hl + c] + x_ref[c, rows, :]
                new[hl + c] = a_re * h[hl + c] + a_im * h[c] + x_ref[hl + c, rows, :]
            for c in range(nl):
                x_ref[c, rows, :] = new[c]
            h = new
        for c in range(hl):
            ore_ref[:, k * qw + c * LANES:k * qw + (c + 1) * LANES] = h[c]
            oim_ref[:, k * qw + c * LANES:k * qw + (c + 1) * LANES] = h[hl + c]
        hs = jnp.concatenate([x_ref[c] for c in range(nl)], axis=1).astype(BF16)
        ys.append(jnp.dot(hs, cq_ref[k].astype(BF16), preferred_element_type=F32))
    y = jnp.concatenate(ys, axis=1) + d_ref[...] * u_all
    gate = jnp.dot(y.astype(BF16), wg_ref[...], preferred_element_type=F32) + bg_ref[...]
    o_ref[...] = (y * jax.nn.sigmoid(gate)).reshape(n, t, SSM_WIDTH)


def ssm_sample(zs3, layer, h_re, h_im, bq32, a8, cq32, d_skip, w_glu, b_glu):
    n, t, _ = zs3.shape
    ucol = 3 * ATT_WIDTH // SSM_WIDTH
    ns = SSM_GROUPS * SSM_STATE
    full = lambda *shape: pl.BlockSpec(shape, lambda s: (0,) * len(shape))
    return pl.pallas_call(
        _ssm_sample_kernel,
        grid=(1,),
        in_specs=[pl.BlockSpec((n, t, SSM_WIDTH), lambda s: (0, 0, ucol))]
        + [_layer_spec(p, layer) for p in (h_re, h_im, bq32, a8, cq32, d_skip, w_glu, b_glu)],
        out_specs=[full(n, t, SSM_WIDTH), full(n, ns), full(n, ns)],
        out_shape=[jax.ShapeDtypeStruct((n, t, SSM_WIDTH), F32),
                   jax.ShapeDtypeStruct((n, ns), F32),
                   jax.ShapeDtypeStruct((n, ns), F32)],
        scratch_shapes=[pltpu.VMEM((2 * SSM_QW // LANES, n * t, LANES), F32)],
        compiler_params=_params("arbitrary"),
        name="ssm_sample",
    )(zs3, h_re, h_im, bq32, a8, cq32, d_skip, w_glu, b_glu)


def _sgu_norm_v(v, g):
    gv = jax.nn.gelu(v)
    vc = gv - jnp.mean(gv, axis=-1, keepdims=True)
    return vc * lax.rsqrt(jnp.mean(vc * vc, axis=-1, keepdims=True) + EPS) * g


def _sgu_causal(w_ref):
    c = SGU_CHUNK
    keep = lax.broadcasted_iota(jnp.int32, (c, c), 0) >= lax.broadcasted_iota(jnp.int32, (c, c), 1)
    return [jnp.where(keep, w_ref[h], 0.0).astype(BF16) for h in range(SGU_HEADS)]


def _sgu_prompt_kernel(u_ref, v_ref, g_ref, w_ref, b_ref, o_ref):
    c = SGU_CHUNK
    e = SGU_HEAD_DIM
    rows = u_ref.shape[1]
    w = _sgu_causal(w_ref)
    gu = jax.nn.gelu(u_ref[0])
    gv = _sgu_norm_v(v_ref[0], g_ref[...]).astype(BF16)
    for ci in range(rows // c):
        for h in range(SGU_HEADS):
            mixed = jnp.dot(w[h], gv[ci * c:(ci + 1) * c, h * e:(h + 1) * e],
                            preferred_element_type=F32) + b_ref[h]
            o_ref[0, ci * c:(ci + 1) * c, h * e:(h + 1) * e] = (
                gu[ci * c:(ci + 1) * c, h * e:(h + 1) * e] * mixed)


def sgu_prompt(z3, layer, g, w_sp, b_rows, rows=1024):
    n, seq, _ = z3.shape
    ucol = (3 * ATT_WIDTH + SSM_WIDTH) // SGU_WIDTH
    return pl.pallas_call(
        _sgu_prompt_kernel,
        grid=(n, seq // rows),
        in_specs=[pl.BlockSpec((1, rows, SGU_WIDTH), lambda i, j: (i, j, ucol)),
                  pl.BlockSpec((1, rows, SGU_WIDTH), lambda i, j: (i, j, ucol + 1)),
                  _layer_spec(g, layer), _layer_spec(w_sp, layer), _layer_spec(b_rows, layer)],
        out_specs=pl.BlockSpec((1, rows, SGU_WIDTH), lambda i, j: (i, j, 0)),
        out_shape=jax.ShapeDtypeStruct((n, seq, SGU_WIDTH), F32),
        compiler_params=_params("parallel", "parallel"),
        name="sgu_prompt",
    )(z3, z3, g, w_sp, b_rows)


def _sgu_sample_kernel(u_ref, v_ref, g_ref, w_ref, b_ref, o_ref, gv_ref):
    c = SGU_CHUNK
    e = SGU_HEAD_DIM
    t = u_ref.shape[1]
    w = _sgu_causal(w_ref)
    gu = jax.nn.gelu(u_ref[0])
    gv = _sgu_norm_v(v_ref[0], g_ref[...])
    gv_ref[0] = gv
    gvp = jnp.concatenate([gv, jnp.zeros((c - t, SGU_WIDTH), F32)], axis=0).astype(BF16)
    for h in range(SGU_HEADS):
        mixed = jnp.dot(w[h], gvp[:, h * e:(h + 1) * e], preferred_element_type=F32) + b_ref[h]
        o_ref[0, :, h * e:(h + 1) * e] = gu[:, h * e:(h + 1) * e] * mixed[:t]


def sgu_sample(zs3, layer, g, w_sp, b_rows):
    n, t, _ = zs3.shape
    ucol = (3 * ATT_WIDTH + SSM_WIDTH) // SGU_WIDTH
    return pl.pallas_call(
        _sgu_sample_kernel,
        grid=(n,),
        in_specs=[pl.BlockSpec((1, t, SGU_WIDTH), lambda i: (i, 0, ucol)),
                  pl.BlockSpec((1, t, SGU_WIDTH), lambda i: (i, 0, ucol + 1)),
                  _layer_spec(g, layer), _layer_spec(w_sp, layer), _layer_spec(b_rows, layer)],
        out_specs=[pl.BlockSpec((1, t, SGU_WIDTH), lambda i: (i, 0, 0)),
                   pl.BlockSpec((1, t, SGU_WIDTH), lambda i: (i, 0, 0))],
        out_shape=[jax.ShapeDtypeStruct((n, t, SGU_WIDTH), F32),
                   jax.ShapeDtypeStruct((n, t, SGU_WIDTH), F32)],
        compiler_params=_params("parallel"),
        name="sgu_sample",
    )(zs3, zs3, g, w_sp, b_rows)


def _out_proj_kernel(att_ref, ssm_ref, sgu_ref, x_ref, w_ref, gm_ref, gp_ref, o_ref):
    a0 = ATT_WIDTH
    a1 = ATT_WIDTH + SSM_WIDTH
    a = _rms(att_ref[...], gm_ref[:, :a0]).astype(BF16)
    s = _rms(ssm_ref[...], gm_ref[:, a0:a1]).astype(BF16)
    c = _rms(sgu_ref[...], gm_ref[:, a1:]).astype(BF16)
    mixed = jnp.concatenate([a, s, c], axis=1)
    y = jnp.dot(mixed, w_ref[...], preferred_element_type=F32)
    o_ref[...] = x_ref[...] + _rms(y, gp_ref[...])


def out_proj(att, ssm, sgu, x, w, layer, gm, gp, tm):
    m, dm = x.shape
    row = lambda width: pl.BlockSpec((tm, width), lambda i: (i, 0))
    return pl.pallas_call(
        _out_proj_kernel,
        grid=(m // tm,),
        in_specs=[row(ATT_WIDTH), row(SSM_WIDTH), row(SGU_WIDTH), row(dm),
                  pl.BlockSpec(w.shape, lambda i: (0, 0)), _layer_spec(gm, layer), _layer_spec(gp, layer)],
        out_specs=row(dm),
        out_shape=jax.ShapeDtypeStruct((m, dm), F32),
        compiler_params=_params("parallel"),
        name="out_proj",
    )(att, ssm, sgu, x, w, gm, gp)


def _ffn_kernel(x_ref, gpre_ref, wg_ref, wu_ref, wd_ref, gpost_ref, *rest, n_cast):
    w32_refs, o_ref = rest[:n_cast], rest[n_cast]
    w16_refs, h_ref = rest[n_cast + 1:2 * n_cast + 1], rest[-1]
    j = pl.program_id(1)
    last = pl.num_programs(1) - 1
    tm = x_ref.shape[0]
    chunk = min(tm, FFN_EDGE_ROWS)

    def partial_out(h):
        gate = jnp.dot(h, wg_ref[...], preferred_element_type=F32)
        up = jnp.dot(h, wu_ref[...], preferred_element_type=F32)
        act = (jax.nn.silu(gate) * up).astype(BF16)
        return jnp.dot(act, wd_ref[...], preferred_element_type=F32)

    def cast_side_stream():
        for src_ref, dst_ref in zip(w32_refs, w16_refs):
            dst_ref[...] = src_ref[...].astype(BF16)

    @pl.when(j == 0)
    def _():
        for r in range(0, tm, chunk):
            rows = pl.ds(r, chunk)
            h = _rms(x_ref[rows, :], gpre_ref[...]).astype(BF16)
            h_ref[rows, :] = h
            o_ref[rows, :] = partial_out(h)
        cast_side_stream()

    @pl.when((j > 0) & (j < last))
    def _():
        o_ref[...] += partial_out(h_ref[...])
        cast_side_stream()

    @pl.when(j == last)
    def _():
        for r in range(0, tm, chunk):
            rows = pl.ds(r, chunk)
            f = o_ref[rows, :] + partial_out(h_ref[rows, :])
            o_ref[rows, :] = x_ref[rows, :] + _rms(f, gpost_ref[...])
        cast_side_stream()


def ffn(x, gpre, wg, wu, wd, layer, gpost, tm, tf, cast=()):
    m, dm = x.shape
    f = wg.shape[1]
    gi, gj = m // tm, f // tf
    assert gj >= 2
    cb = CAST_BLOCK
    cast_in, cast_out, cast_shape = [], [], []
    for w32, lyr, rows_follow_i in cast:
        _, r, c = w32.shape
        nr, nc = r // cb, c // cb
        assert r % cb == 0 and c % cb == 0
        if rows_follow_i:
            assert nr == gi and nc <= gj
            imap = lambda i, j, nc=nc: (i, jnp.minimum(j, nc - 1))
        else:
            assert nc == gi and nr <= gj
            imap = lambda i, j, nr=nr: (jnp.minimum(j, nr - 1), i)
        cast_in.append(pl.BlockSpec((None, cb, cb), lambda i, j, imap=imap, lyr=lyr: (lyr,) + imap(i, j)))
        cast_out.append(pl.BlockSpec((cb, cb), imap))
        cast_shape.append(jax.ShapeDtypeStruct((r, c), BF16))
    kern = functools.partial(_ffn_kernel, n_cast=len(cast))
    res = pl.pallas_call(
        kern,
        grid=(gi, gj),
        in_specs=[pl.BlockSpec((tm, dm), lambda i, j: (i, 0)),
                  _layer_spec(gpre, layer),
                  pl.BlockSpec((dm, tf), lambda i, j: (0, j)),
                  pl.BlockSpec((dm, tf), lambda i, j: (0, j)),
                  pl.BlockSpec((tf, dm), lambda i, j: (j, 0)),
                  _layer_spec(gpost, layer)] + cast_in,
        out_specs=[pl.BlockSpec((tm, dm), lambda i, j: (i, 0))] + cast_out,
        out_shape=[jax.ShapeDtypeStruct((m, dm), F32)] + cast_shape,
        scratch_shapes=[pltpu.VMEM((tm, dm), BF16)],
        compiler_params=_params("arbitrary", "arbitrary"),
        name="ffn_cast" if cast else "ffn",
    )(x, gpre, wg, wu, wd, gpost, *[w for w, _, _ in cast])
    return res[0], res[1:]


def _t5_bucket(dist):
    max_exact = N_REL_BUCKETS // 2
    df = jnp.maximum(dist, 1).astype(F32)
    large = max_exact + (jnp.log(df / max_exact) / math.log(REL_MAX_DIST / max_exact)
                         * (N_REL_BUCKETS - max_exact)).astype(jnp.int32)
    large = jnp.minimum(large, N_REL_BUCKETS - 1)
    return jnp.where(dist < max_exact, dist, large)


def _strided_bias(rel_bias, dilation, n_steps):
    dist = jnp.arange(n_steps + 1, dtype=jnp.int32) * dilation
    return rel_bias[_t5_bucket(dist)].astype(F32)


def _bias_lookup(sb, steps, ok):
    onehot = (steps[..., None] == jnp.arange(sb.shape[0])).astype(F32)
    vals = jnp.einsum('...j,jh->h...', onehot, sb, precision=lax.Precision.HIGHEST)
    return jnp.where(ok[None], vals, NEG_INF)


def _prompt_bias_tables(rel_bias):
    qb = ATT_BLOCK
    qi = jnp.arange(qb)[:, None]
    kj = jnp.arange(2 * qb)[None, :]
    tables = []
    for window, d in DILATED_PATTERNS:
        nk = window // d
        sb = _strided_bias(rel_bias, d, nk)
        delta = qb + qi - kj
        normal = _bias_lookup(sb, jnp.clip(delta, 0, nk), (delta >= 0) & (delta <= nk))
        delta_f = qi - kj
        first = _bias_lookup(sb, jnp.clip(delta_f, 0, nk), (delta_f >= 0) & (delta_f <= nk))
        tables.append(jnp.stack([normal, first]))
    return jnp.stack(tables).astype(F32)


def _sample_bias_tables(rel_bias, w, t):
    d_far, tail, groups = _sample_cache_split(w, t)
    tq = jnp.arange(t)[:, None]
    old_slots = (jnp.arange(groups)[:, None] * d_far + jnp.arange(t)[None, :]).reshape(1, -1)
    old_delta = w + tq - old_slots
    tail_delta = w + tq - (w - tail + jnp.arange(tail))[None, :]
    new_delta = tq - jnp.arange(ATT_BLOCK)[None, :]
    out_t, out_n = [], []
    for window, d in DILATED_PATTERNS:
        nk = window // d
        sb = _strided_bias(rel_bias, d, nk)

        def table(delta, extra_ok):
            ok = (delta >= 0) & (delta % d == 0) & (delta <= nk * d) & extra_ok
            return _bias_lookup(sb, jnp.clip(delta // d, 0, nk), ok)

        out_t.append(table(tail_delta, True))
        out_n.append(table(new_delta, jnp.arange(ATT_BLOCK)[None, :] < t))
        if d == d_far:
            out_o = table(old_delta, True)
        else:
            assert window <= tail
    return out_o.astype(F32), jnp.stack(out_t).astype(F32), jnp.stack(out_n).astype(F32)


def _ssm_tables(lam_re, lam_im, log_step, b_re, b_im, c_re, c_im):
    g, p, cg = SSM_GROUPS, SSM_STATE, SSM_GROUP
    nq = SSM_QUARTERS
    gq = g // nq
    step = jnp.exp(log_step)[:, None]
    mag = jnp.exp(lam_re * step)
    a_re = mag * jnp.cos(lam_im * step)
    a_im = mag * jnp.sin(lam_im * step)
    den = lam_re * lam_re + lam_im * lam_im
    coef_re = ((a_re - 1.0) * lam_re + a_im * lam_im) / den
    coef_im = (a_im * lam_re - (a_re - 1.0) * lam_im) / den
    bb_re = coef_re[..., None] * b_re - coef_im[..., None] * b_im
    bb_im = coef_re[..., None] * b_im + coef_im[..., None] * b_re
    eye = jnp.eye(gq, dtype=F32)

    def in_mat(bb):
        bb = bb.reshape(nq, gq, p, cg)
        return jnp.einsum('kgpc,gh->kgchp', bb, eye).reshape(nq, gq * cg, gq * p)

    def out_mat(c):
        c = c.reshape(nq, gq, cg, p)
        return jnp.einsum('kgcp,gh->kgphc', c, eye).reshape(nq, gq * p, gq * cg)

    bq = jnp.concatenate([in_mat(bb_re), in_mat(bb_im)], axis=2)
    cq = jnp.concatenate([out_mat(c_re), -out_mat(c_im)], axis=1)
    aq = jnp.concatenate([a_re.reshape(nq, gq * p), a_im.reshape(nq, gq * p)], axis=1)
    a8 = jnp.concatenate([aq, aq], axis=0)
    return bq, cq, a8


def _rows(v):
    return v.reshape(v.shape[0], 1, -1)


def kernel(x_prompt, x_sample, cache_attn_k, cache_attn_v, state_ssm_re, state_ssm_im, rel_bias, w_in, w_out, g_pre_mix, g_post_mix, g_mix_out, ssm_lam_re, ssm_lam_im, ssm_log_step, ssm_b_re, ssm_b_im, ssm_c_re, ssm_c_im, ssm_d, ssm_w_glu, ssm_b_glu, sgu_g, sgu_w, sgu_b, g_pre_ffn, g_post_ffn, w_gate, w_up, w_down):
    nb, seq, dm = x_prompt.shape
    ns, ts, _ = x_sample.shape
    wbuf = cache_attn_k.shape[2]
    n_keep = min(DILATED_PATTERNS[-1][0], seq)
    h, e = ATT_HEADS, ATT_HEAD_DIM

    bias_p = _prompt_bias_tables(rel_bias)
    bias_o, bias_t, bias_n = _sample_bias_tables(rel_bias, wbuf, ts)
    k_buf = lax.empty((DEPTH, nb, n_keep * h, e), F32)
    v_buf = lax.empty((DEPTH, nb, n_keep * h, e), F32)
    st_re = state_ssm_re.reshape(DEPTH, ns, SSM_GROUPS * SSM_STATE)
    st_im = state_ssm_im.reshape(DEPTH, ns, SSM_GROUPS * SSM_STATE)

    xp = x_prompt.reshape(nb * seq, dm)
    xs = x_sample.reshape(ns * ts, dm)
    outs = {k: [] for k in ('rp', 'ip', 'ks', 'vs', 'rs', 'is', 'us')}

    big = (w_in, w_out, w_gate, w_up, w_down)
    big16 = tuple(w[0].astype(BF16) for w in big)
    w_glu16 = ssm_w_glu.astype(BF16)
    bq, cq, a8 = jax.vmap(_ssm_tables)(ssm_lam_re, ssm_lam_im, ssm_log_step,
                                       ssm_b_re, ssm_b_im, ssm_c_re, ssm_c_im)
    bq16, cq16 = bq.astype(BF16), cq.astype(BF16)
    sgu_b_rows = jnp.broadcast_to(sgu_b[..., None], (DEPTH, SGU_HEADS, SGU_CHUNK, SGU_HEAD_DIM))
    g_pre, g_post, g_mix = _rows(g_pre_mix), _rows(g_post_mix), _rows(g_mix_out)
    g_pre_f, g_post_f = _rows(g_pre_ffn), _rows(g_post_ffn)
    d_skip, b_glu, g_sgu = _rows(ssm_d), _rows(ssm_b_glu), _rows(sgu_g)

    for l in range(DEPTH):
        w_in16, w_out16, w_gate16, w_up16, w_down16 = big16
        cast = [(w, l + 1, w is not w_down) for w in big] if l + 1 < DEPTH else []

        z, k_buf, v_buf = in_proj(xp, g_pre, w_in16, l, tm=1024, tn=768,
                                  kv=(seq, n_keep, k_buf, v_buf))
        z = z.reshape(nb, seq, IN_COLS)
        o_att = attn_prompt(z, bias_p)
        o_ssm, h_fin = ssm_prompt(z, l, bq16, a8, cq16, d_skip, w_glu16, b_glu)
        o_sgu = sgu_prompt(z, l, g_sgu, sgu_w, sgu_b_rows)
        xp = out_proj(o_att.reshape(nb * seq, ATT_WIDTH), o_ssm.reshape(nb * seq, SSM_WIDTH),
                      o_sgu.reshape(nb * seq, SGU_WIDTH), xp, w_out16, l, g_mix, g_post, tm=512)
        xp, next16 = ffn(xp, g_pre_f, w_gate16, w_up16, w_down16, l, g_post_f, tm=1024, tf=256,
                         cast=cast)
        hq = h_fin.reshape(nb, SSM_QUARTERS, 2, SSM_GROUPS // SSM_QUARTERS, SSM_STATE)
        outs['rp'].append(hq[:, :, 0].reshape(nb, SSM_GROUPS, SSM_STATE))
        outs['ip'].append(hq[:, :, 1].reshape(nb, SSM_GROUPS, SSM_STATE))

        zs = in_proj(xs, g_pre, w_in16, l, tm=ns * ts, tn=768).reshape(ns, ts, IN_COLS)
        s_att = attn_sample(zs, cache_attn_k, cache_attn_v, l, bias_o, bias_t, bias_n)
        s_ssm, s_re, s_im = ssm_sample(zs, l, st_re, st_im, bq, a8, cq, d_skip, w_glu16, b_glu)
        s_sgu, s_gv = sgu_sample(zs, l, g_sgu, sgu_w, sgu_b_rows)
        xs = out_proj(s_att.reshape(ns * ts, ATT_WIDTH), s_ssm.reshape(ns * ts, SSM_WIDTH),
                      s_sgu.reshape(ns * ts, SGU_WIDTH), xs, w_out16, l, g_mix, g_post, tm=ns * ts)
        xs, _ = ffn(xs, g_pre_f, w_gate16, w_up16, w_down16, l, g_post_f, tm=ns * ts, tf=512)
        big16 = tuple(next16) or big16
        outs['ks'].append(zs[:, :, ATT_WIDTH:2 * ATT_WIDTH].reshape(ns, ts, h, e))
        outs['vs'].append(zs[:, :, 2 * ATT_WIDTH:3 * ATT_WIDTH].reshape(ns, ts, h, e))
        outs['rs'].append(s_re.reshape(ns, SSM_GROUPS, SSM_STATE))
        outs['is'].append(s_im.reshape(ns, SSM_GROUPS, SSM_STATE))
        outs['us'].append(s_gv)

    st = lambda key: jnp.stack(outs[key])
    kv_shape = (DEPTH, nb, n_keep, h, e)
    return (xp.reshape(nb, seq, dm), xs.reshape(ns, ts, dm),
            k_buf.reshape(kv_shape), v_buf.reshape(kv_shape), st('rp'), st('ip'),
            st('ks'), st('vs'), st('rs'), st('is'), st('us'))
```

```python
import functools
import math

import jax
import jax.numpy as jnp
from jax import lax
from jax.experimental import pallas as pl
from jax.experimental.pallas import tpu as pltpu

F32 = jnp.float32
BF16 = jnp.bfloat16

D_MODEL = 2048
DEPTH = 4
ATT_HEADS = 8
ATT_HEAD_DIM = 128
ATT_WIDTH = ATT_HEADS * ATT_HEAD_DIM
DILATED_PATTERNS = ((128, 1), (512, 4), (2048, 16))
ATT_BLOCK = 128
ATT_GROUP = (8, 8, 4)
ATT_SCALE = ATT_HEAD_DIM ** -0.5
N_REL_BUCKETS = 32
REL_MAX_DIST = 2048
SSM_WIDTH = 512
SSM_GROUP = 16
SSM_GROUPS = SSM_WIDTH // SSM_GROUP
SSM_STATE = 64
SSM_QUARTERS = 4
SSM_QW = SSM_GROUPS * SSM_STATE // SSM_QUARTERS
SSM_QC = SSM_WIDTH // SSM_QUARTERS
SGU_WIDTH = 512
SGU_HEADS = 4
SGU_HEAD_DIM = SGU_WIDTH // SGU_HEADS
SGU_CHUNK = 128
IN_COLS = 3 * ATT_WIDTH + SSM_WIDTH + 2 * SGU_WIDTH
FFN_HIDDEN = 5632
EPS = 1e-6
MASKED = -float("inf")
LANES = 128
CAST_BLOCK = 256
FFN_EDGE_ROWS = 256

VMEM_LIMIT_BYTES = 56 * 1024 * 1024


def _params(*sem):
    return pltpu.CompilerParams(dimension_semantics=sem, vmem_limit_bytes=VMEM_LIMIT_BYTES)


def _layer_spec(arr, layer):
    zeros = (0,) * (arr.ndim - 1)
    return pl.BlockSpec((None,) + arr.shape[1:], lambda *_: (layer,) + zeros)


def _rms(x, g):
    return x * lax.rsqrt(jnp.mean(x * x, axis=-1, keepdims=True) + EPS) * g


def _in_proj_kernel(x_ref, g_ref, w_ref, *rest, keep_from, tiles_per_seq):
    h_ref = rest[-1]
    outs = rest[-4:-1] if len(rest) > 2 else rest[:1]
    z_ref = outs[0]
    i = pl.program_id(0)
    j = pl.program_id(1)

    tm, tn = z_ref.shape
    chunk = min(tm, FFN_EDGE_ROWS)

    @pl.when(j == 0)
    def _():
        for r in range(0, tm, chunk):
            rows = pl.ds(r, chunk)
            h = _rms(x_ref[rows, :], g_ref[...]).astype(BF16)
            h_ref[rows, :] = h
            z_ref[rows, :] = jnp.dot(h, w_ref[...], preferred_element_type=F32)

    @pl.when(j > 0)
    def _():
        _in_proj_tile(i, j, h_ref, w_ref, outs, keep_from, tiles_per_seq)


def _in_proj_tile(i, j, h_ref, w_ref, outs, keep_from, tiles_per_seq):
    z_ref = outs[0]
    res = jnp.dot(h_ref[...], w_ref[...], preferred_element_type=F32)
    z_ref[...] = res

    if len(outs) > 1:
        tm, tn = res.shape
        keep = (i % tiles_per_seq) >= keep_from
        by_tile = {}
        for buf_ref, col0 in ((outs[1], ATT_WIDTH), (outs[2], 2 * ATT_WIDTH)):
            for head in range(ATT_HEADS):
                col = col0 + head * ATT_HEAD_DIM
                by_tile.setdefault(col // tn, []).append((buf_ref, head, col % tn))
        for tile, heads in by_tile.items():
            @pl.when(keep & (j == tile))
            def _(heads=heads):
                for buf_ref, head, off in heads:
                    rows = pl.ds(head, tm, stride=ATT_HEADS)
                    buf_ref[0, 0, rows, :] = res[:, off:off + ATT_HEAD_DIM]


def in_proj(x, g, w, layer, tm, tn, kv=None):
    m, k = x.shape
    n = w.shape[1]
    assert tn % ATT_HEAD_DIM == 0
    in_specs = [pl.BlockSpec((tm, k), lambda i, j: (i, 0)),
                _layer_spec(g, layer),
                pl.BlockSpec((k, tn), lambda i, j: (0, j))]
    z_spec = pl.BlockSpec((tm, tn), lambda i, j: (i, j))
    z_shape = jax.ShapeDtypeStruct((m, n), F32)
    scratch = [pltpu.VMEM((tm, k), BF16)]
    if kv is None:
        kern = functools.partial(_in_proj_kernel, keep_from=0, tiles_per_seq=1)
        return pl.pallas_call(
            kern, grid=(m // tm, n // tn), in_specs=in_specs, out_specs=z_spec, out_shape=z_shape,
            scratch_shapes=scratch, compiler_params=_params("parallel", "arbitrary"),
            name="in_proj",
        )(x, g, w)
    seq, n_keep, k_buf, v_buf = kv
    tiles_per_seq = seq // tm
    keep_from = (seq - n_keep) // tm
    assert seq % tm == 0 and (seq - n_keep) % tm == 0
    assert tn <= ATT_WIDTH

    def buf_map(i, j):
        return (layer, i // tiles_per_seq, jnp.maximum(i % tiles_per_seq - keep_from, 0), 0)

    buf_spec = pl.BlockSpec((1, 1, tm * ATT_HEADS, ATT_HEAD_DIM), buf_map)
    any_spec = pl.BlockSpec(memory_space=pl.ANY)
    kern = functools.partial(_in_proj_kernel, keep_from=keep_from, tiles_per_seq=tiles_per_seq)
    return pl.pallas_call(
        kern, grid=(m // tm, n // tn),
        in_specs=in_specs + [any_spec, any_spec],
        out_specs=[z_spec, buf_spec, buf_spec],
        out_shape=[z_shape, jax.ShapeDtypeStruct(k_buf.shape, F32),
                   jax.ShapeDtypeStruct(v_buf.shape, F32)],
        input_output_aliases={3: 1, 4: 2},
        scratch_shapes=scratch, compiler_params=_params("arbitrary", "arbitrary"),
        name="in_proj_kv",
    )(x, g, w, k_buf, v_buf)


def _attn_prompt_kernel(q_ref, k_ref, v_ref, b_ref, o_ref, acc_ref, m_ref, l_ref, *, seq):
    qb = ATT_BLOCK
    n_pat = len(DILATED_PATTERNS)

    for p, (_, d) in enumerate(DILATED_PATTERNS):
        shift = int(math.log2(d))
        whole = seq // d == 2 * qb
        blocks_per_trip = ATT_GROUP[p]
        units = blocks_per_trip // 2 if whole else blocks_per_trip

        def group(g, carry, p=p, d=d, shift=shift, whole=whole, units=units):
            rows, biases = [], []
            for u in range(units):
                i = g * units + u
                if whole:
                    rows.append((pl.ds(i, 2 * qb, stride=d), pl.ds(i, 2 * qb, stride=d)))
                    biases.append(None)
                    continue
                r = i & (d - 1)
                b = i >> shift
                q_start = r + b * (qb * d)
                k_start = r + jnp.maximum(b - 1, 0) * (qb * d)
                if d == 1:
                    rows.append((pl.ds(pl.multiple_of(q_start, qb), qb),
                                 pl.ds(pl.multiple_of(k_start, qb), 2 * qb)))
                else:
                    rows.append((pl.ds(q_start, qb, stride=d), pl.ds(k_start, 2 * qb, stride=d)))
                biases.append(jnp.where(b == 0, 1, 0))
            nt = (((1,), (1,)), ((), ()))
            scores = [lax.dot_general(q_ref[0, qr, :].astype(BF16), k_ref[0, kr, :].astype(BF16), nt,
                                      preferred_element_type=F32) for qr, kr in rows]
            probs = []
            for (qr, _), first, s in zip(rows, biases, scores):
                if whole:
                    bias = jnp.concatenate([b_ref[p, 1, 0], b_ref[p, 0, 0]], axis=0)
                else:
                    bias = b_ref[p, first, 0]
                s = s * ATT_SCALE + bias
                m_blk = jnp.max(s, axis=1, keepdims=True)
                pe = jnp.exp(s - m_blk)
                wide = (s.shape[0], ATT_HEAD_DIM)
                m_ref[p, qr, :] = jnp.broadcast_to(m_blk, wide)
                l_ref[p, qr, :] = jnp.broadcast_to(jnp.sum(pe, axis=1, keepdims=True), wide)
                probs.append(pe.astype(BF16))
            for (qr, kr), pe in zip(rows, probs):
                acc_ref[p, qr, :] = jnp.dot(pe, v_ref[0, kr, :].astype(BF16),
                                            preferred_element_type=F32)
            return carry

        lax.fori_loop(0, seq // qb // blocks_per_trip, group, 0)

    rows_per = 2 * qb

    def merge(c, carry):
        rows = pl.ds(pl.multiple_of(c * rows_per, rows_per), rows_per)
        ms = [m_ref[p, rows, :] for p in range(n_pat)]
        m = functools.reduce(jnp.maximum, ms)
        num = den = None
        for p in range(n_pat):
            w = jnp.exp(ms[p] - m)
            num = w * acc_ref[p, rows, :] if num is None else num + w * acc_ref[p, rows, :]
            den = w * l_ref[p, rows, :] if den is None else den + w * l_ref[p, rows, :]
        o_ref[0, rows, :] = num / den
        return carry

    lax.fori_loop(0, seq // rows_per, merge, 0)


def attn_prompt(z3, bias):
    n, seq, _ = z3.shape
    h = ATT_HEADS
    kern = functools.partial(_attn_prompt_kernel, seq=seq)
    return pl.pallas_call(
        kern,
        grid=(n, h),
        in_specs=[pl.BlockSpec((1, seq, ATT_HEAD_DIM), lambda i, j: (i, 0, j)),
                  pl.BlockSpec((1, seq, ATT_HEAD_DIM), lambda i, j: (i, 0, h + j)),
                  pl.BlockSpec((1, seq, ATT_HEAD_DIM), lambda i, j: (i, 0, 2 * h + j)),
                  pl.BlockSpec((3, 2, 1, ATT_BLOCK, 2 * ATT_BLOCK), lambda i, j: (0, 0, j, 0, 0))],
        out_specs=pl.BlockSpec((1, seq, ATT_HEAD_DIM), lambda i, j: (i, 0, j)),
        out_shape=jax.ShapeDtypeStruct((n, seq, ATT_WIDTH), F32),
        scratch_shapes=[pltpu.VMEM((len(DILATED_PATTERNS), seq, ATT_HEAD_DIM), F32)] * 3,
        compiler_params=_params("parallel", "parallel"),
        name="attn_prompt",
    )(z3, z3, z3, bias)


def _attn_sample_kernel(q_ref, kn_ref, vn_ref, ko_ref, vo_ref, kt_ref, vt_ref,
                        bo_ref, bt_ref, bn_ref, o_ref):
    t = q_ref.shape[1]
    e = ATT_HEAD_DIM
    nh = ATT_HEADS
    n_pat = len(DILATED_PATTERNS)
    groups, kept_rows, _ = ko_ref.shape
    kept = kept_rows // nh
    tail = kt_ref.shape[0] // nh
    pad = jnp.zeros((ATT_BLOCK - t, e), F32)
    nt = (((1,), (1,)), ((), ()))
    heads = range(nh)
    cols = [slice(h * e, (h + 1) * e) for h in heads]

    def old_rows(ref, h):
        return ref[:, pl.ds(h, kept, stride=nh), :].reshape(groups * kept, e).astype(BF16)

    def tail_rows(ref, h):
        return ref[pl.ds(h, tail, stride=nh), :].astype(BF16)

    so, st, sn = [], [], []
    for h in heads:
        q = q_ref[0, :, cols[h]].astype(BF16)
        kn = jnp.concatenate([kn_ref[0, :, cols[h]], pad], axis=0).astype(BF16)
        so.append(lax.dot_general(q, old_rows(ko_ref, h), nt, preferred_element_type=F32) * ATT_SCALE)
        st.append(lax.dot_general(q, tail_rows(kt_ref, h), nt, preferred_element_type=F32) * ATT_SCALE)
        sn.append(lax.dot_general(q, kn, nt, preferred_element_type=F32) * ATT_SCALE)
    po, pt, pn, ls = [], [], [], []
    for h in heads:
        s_old = so[h] + bo_ref[h]
        sts = [st[h] + bt_ref[p, h] for p in range(n_pat)]
        sns = [sn[h] + bn_ref[p, h] for p in range(n_pat)]
        m = None
        for s in [s_old] + sts + sns:
            mx = jnp.max(s, axis=1, keepdims=True)
            m = mx if m is None else jnp.maximum(m, mx)
        poh = jnp.exp(s_old - m)
        pth = sum(jnp.exp(s - m) for s in sts)
        pnh = sum(jnp.exp(s - m) for s in sns)
        ls.append(jnp.sum(poh, axis=1, keepdims=True) + jnp.sum(pth, axis=1, keepdims=True)
                  + jnp.sum(pnh, axis=1, keepdims=True))
        po.append(poh.astype(BF16))
        pt.append(pth.astype(BF16))
        pn.append(pnh.astype(BF16))
    for h in heads:
        vn = jnp.concatenate([vn_ref[0, :, cols[h]], pad], axis=0).astype(BF16)
        acc = (jnp.dot(po[h], old_rows(vo_ref, h), preferred_element_type=F32)
               + jnp.dot(pt[h], tail_rows(vt_ref, h), preferred_element_type=F32)
               + jnp.dot(pn[h], vn, preferred_element_type=F32))
        o_ref[0, :, cols[h]] = acc / ls[h]


def _sample_cache_split(w, t):
    d_far = DILATED_PATTERNS[-1][1]
    tail = max(window for window, _ in DILATED_PATTERNS[:-1])
    assert t <= d_far and w % d_far == 0 and tail % d_far == 0 and tail <= w
    return d_far, tail, (w - tail) // d_far


def attn_sample(zs3, cache_k, cache_v, layer, bias_o, bias_t, bias_n):
    n, t, _ = zs3.shape
    h = ATT_HEADS
    e = ATT_HEAD_DIM
    depth, _, w = cache_k.shape[:3]
    d_far, tail, groups = _sample_cache_split(w, t)
    by_group = lambda c: c.reshape(depth, n, w // d_far, d_far * h, e)
    by_tail = lambda c: c.reshape(depth, n, w // tail, tail * h, e)
    old_spec = pl.BlockSpec((None, None, groups, t * h, e), lambda i: (layer, i, 0, 0, 0))
    tail_spec = pl.BlockSpec((None, None, None, tail * h, e), lambda i: (layer, i, w // tail - 1, 0, 0))
    return pl.pallas_call(
        _attn_sample_kernel,
        grid=(n,),
        in_specs=[pl.BlockSpec((1, t, ATT_WIDTH), lambda i: (i, 0, 0)),
                  pl.BlockSpec((1, t, ATT_WIDTH), lambda i: (i, 0, 1)),
                  pl.BlockSpec((1, t, ATT_WIDTH), lambda i: (i, 0, 2)),
                  old_spec, old_spec, tail_spec, tail_spec,
                  pl.BlockSpec(bias_o.shape, lambda i: (0, 0, 0)),
                  pl.BlockSpec(bias_t.shape, lambda i: (0, 0, 0, 0)),
                  pl.BlockSpec(bias_n.shape, lambda i: (0, 0, 0, 0))],
        out_specs=pl.BlockSpec((1, t, ATT_WIDTH), lambda i: (i, 0, 0)),
        out_shape=jax.ShapeDtypeStruct((n, t, ATT_WIDTH), F32),
        compiler_params=_params("parallel"),
        name="attn_sample",
    )(zs3, zs3, zs3, by_group(cache_k), by_group(cache_v), by_tail(cache_k), by_tail(cache_v),
      bias_o, bias_t, bias_n)


def _ssm_prompt_kernel(u_ref, bq_ref, a_ref, cq_ref, d_ref, wg_ref, bg_ref,
                       o_ref, hfin_ref, x_ref, h_ref, *, tb, pitch):
    step = pl.program_id(0)
    nq = SSM_QUARTERS
    qw = SSM_QW
    qc = SSM_QC

    @pl.when(step == 0)
    def _():
        h_ref[...] = jnp.zeros_like(h_ref)

    nl = 2 * qw // LANES
    hl = nl // 2

    for b in range(2):
        for k in range(nq):
            u = u_ref[b, :, k * qc:(k + 1) * qc].astype(BF16)
            x = jnp.dot(u, bq_ref[k], preferred_element_type=F32)
            for c in range(nl):
                x_ref[c, pl.ds((b * nq + k) * pitch, tb), :] = x[:, c * LANES:(c + 1) * LANES]

    a = [a_ref[:, c * LANES:(c + 1) * LANES] for c in range(nl)]

    def scan(t, h):
        rows = pl.ds(t, 2 * nq, stride=pitch)
        new = [None] * nl
        for c in range(hl):
            a_re, a_im, h_re, h_im = a[c], a[hl + c], h[c], h[hl + c]
            new[c] = a_re * h_re - a_im * h_im + x_ref[c, rows, :]
            new[hl + c] = a_re * h_im + a_im * h_re + x_ref[hl + c, rows, :]
        for c in range(nl):
            x_ref[c, rows, :] = new[c]
        return tuple(new)

    h0 = tuple(h_ref[:, c * LANES:(c + 1) * LANES] for c in range(nl))
    h_new = jnp.concatenate(lax.fori_loop(0, tb, scan, h0, unroll=8), axis=1)
    h_ref[...] = h_new

    for b in range(2):
        ys = []
        for k in range(nq):
            rows = pl.ds((b * nq + k) * pitch, tb)
            hs = jnp.concatenate([x_ref[c, rows, :] for c in range(nl)], axis=1).astype(BF16)
            ys.append(jnp.dot(hs, cq_ref[k], preferred_element_type=F32))
        y = jnp.concatenate(ys, axis=1) + d_ref[...] * u_ref[b]
        gate = jnp.dot(y.astype(BF16), wg_ref[...], preferred_element_type=F32) + bg_ref[...]
        o_ref[b] = y * jax.nn.sigmoid(gate)

    @pl.when(step == pl.num_programs(0) - 1)
    def _():
        hfin_ref[...] = h_new


def ssm_prompt(z3, layer, bq, a8, cq, d_skip, w_glu, b_glu, tb=512):
    n, seq, _ = z3.shape
    assert n == 2
    pitch = tb + 8
    ucol = 3 * ATT_WIDTH // SSM_WIDTH
    kern = functools.partial(_ssm_prompt_kernel, tb=tb, pitch=pitch)
    full = lambda *shape: pl.BlockSpec(shape, lambda s: (0,) * len(shape))
    return pl.pallas_call(
        kern,
        grid=(seq // tb,),
        in_specs=[pl.BlockSpec((n, tb, SSM_WIDTH), lambda s: (0, s, ucol))]
        + [_layer_spec(p, layer) for p in (bq, a8, cq, d_skip, w_glu, b_glu)],
        out_specs=[pl.BlockSpec((n, tb, SSM_WIDTH), lambda s: (0, s, 0)),
                   full(2 * SSM_QUARTERS, 2 * SSM_QW)],
        out_shape=[jax.ShapeDtypeStruct((n, seq, SSM_WIDTH), F32),
                   jax.ShapeDtypeStruct((2 * SSM_QUARTERS, 2 * SSM_QW), F32)],
        scratch_shapes=[pltpu.VMEM((2 * SSM_QW // LANES, 2 * SSM_QUARTERS * pitch, LANES), F32),
                        pltpu.VMEM((2 * SSM_QUARTERS, 2 * SSM_QW), F32)],
        compiler_params=_params("arbitrary"),
        name="ssm_prompt",
    )(z3, bq, a8, cq, d_skip, w_glu, b_glu)


def _ssm_sample_kernel(u_ref, hre_ref, him_ref, bq_ref, a_ref, cq_ref, d_ref, wg_ref, bg_ref,
                       o_ref, ore_ref, oim_ref, x_ref):
    n, t, _ = u_ref.shape
    nq = SSM_QUARTERS
    qw = SSM_QW
    qc = SSM_QC
    nl = 2 * qw // LANES
    hl = nl // 2
    u_all = u_ref[...].reshape(n * t, SSM_WIDTH)
    ys = []
    for k in range(nq):
        u = u_all[:, k * qc:(k + 1) * qc]
        x = jnp.dot(u, bq_ref[k], preferred_element_type=F32, precision=lax.Precision.HIGHEST)
        for c in range(nl):
            x_ref[c] = x[:, c * LANES:(c + 1) * LANES]
        h = ([hre_ref[:, k * qw + c * LANES:k * qw + (c + 1) * LANES] for c in range(hl)]
             + [him_ref[:, k * qw + c * LANES:k * qw + (c + 1) * LANES] for c in range(hl)])
        for s in range(t):
            rows = pl.ds(s, n, stride=t)
            new = [None] * nl
            for c in range(hl):
                a_re = a_ref[k:k + 1, c * LANES:(c + 1) * LANES]
                a_im = a_ref[k:k + 1, (hl + c) * LANES:(hl + c + 1) * LANES]
                new[c] = a_re * h[c] - a_im * h[hl + c] + x_ref[c, rows, :]
                new[hl + c] = a_re * h[hl + c] + a_im * h[c] + x_ref[hl + c, rows, :]
            for c in range(nl):
                x_ref[c, rows, :] = new[c]
            h = new
        for c in range(hl):
            ore_ref[:, k * qw + c * LANES:k * qw + (c + 1) * LANES] = h[c]
            oim_ref[:, k * qw + c * LANES:k * qw + (c + 1) * LANES] = h[hl + c]
        hs = jnp.concatenate([x_ref[c] for c in range(nl)], axis=1).astype(BF16)
        ys.append(jnp.dot(hs, cq_ref[k].astype(BF16), preferred_element_type=F32))
    y = jnp.concatenate(ys, axis=1) + d_ref[...] * u_all
    gate = jnp.dot(y.astype(BF16), wg_ref[...], preferred_element_type=F32) + bg_ref[...]
    o_ref[...] = (y * jax.nn.sigmoid(gate)).reshape(n, t, SSM_WIDTH)


def ssm_sample(zs3, layer, h_re, h_im, bq32, a8, cq32, d_skip, w_glu, b_glu):
    n, t, _ = zs3.shape
    ucol = 3 * ATT_WIDTH // SSM_WIDTH
    ns = SSM_GROUPS * SSM_STATE
    full = lambda *shape: pl.BlockSpec(shape, lambda s: (0,) * len(shape))
    return pl.pallas_call(
        _ssm_sample_kernel,
        grid=(1,),
        in_specs=[pl.BlockSpec((n, t, SSM_WIDTH), lambda s: (0, 0, ucol))]
        + [_layer_spec(p, layer) for p in (h_re, h_im, bq32, a8, cq32, d_skip, w_glu, b_glu)],
        out_specs=[full(n, t, SSM_WIDTH), full(n, ns), full(n, ns)],
        out_shape=[jax.ShapeDtypeStruct((n, t, SSM_WIDTH), F32),
                   jax.ShapeDtypeStruct((n, ns), F32),
                   jax.ShapeDtypeStruct((n, ns), F32)],
        scratch_shapes=[pltpu.VMEM((2 * SSM_QW // LANES, n * t, LANES), F32)],
        compiler_params=_params("arbitrary"),
        name="ssm_sample",
    )(zs3, h_re, h_im, bq32, a8, cq32, d_skip, w_glu, b_glu)


def _sgu_norm_v(v, g):
    gv = jax.nn.gelu(v)
    vc = gv - jnp.mean(gv, axis=-1, keepdims=True)
    return vc * lax.rsqrt(jnp.mean(vc * vc, axis=-1, keepdims=True) + EPS) * g


def _sgu_causal(w_ref):
    c = SGU_CHUNK
    keep = lax.broadcasted_iota(jnp.int32, (c, c), 0) >= lax.broadcasted_iota(jnp.int32, (c, c), 1)
    return [jnp.where(keep, w_ref[h], 0.0).astype(BF16) for h in range(SGU_HEADS)]


def _sgu_prompt_kernel(u_ref, v_ref, g_ref, w_ref, b_ref, o_ref):
    c = SGU_CHUNK
    e = SGU_HEAD_DIM
    rows = u_ref.shape[1]
    w = _sgu_causal(w_ref)
    gu = jax.nn.gelu(u_ref[0])
    gv = _sgu_norm_v(v_ref[0], g_ref[...]).astype(BF16)
    for ci in range(rows // c):
        for h in range(SGU_HEADS):
            mixed = jnp.dot(w[h], gv[ci * c:(ci + 1) * c, h * e:(h + 1) * e],
                            preferred_element_type=F32) + b_ref[h]
            o_ref[0, ci * c:(ci + 1) * c, h * e:(h + 1) * e] = (
                gu[ci * c:(ci + 1) * c, h * e:(h + 1) * e] * mixed)


def sgu_prompt(z3, layer, g, w_sp, b_rows, rows=1024):
    n, seq, _ = z3.shape
    ucol = (3 * ATT_WIDTH + SSM_WIDTH) // SGU_WIDTH
    return pl.pallas_call(
        _sgu_prompt_kernel,
        grid=(n, seq // rows),
        in_specs=[pl.BlockSpec((1, rows, SGU_WIDTH), lambda i, j: (i, j, ucol)),
                  pl.BlockSpec((1, rows, SGU_WIDTH), lambda i, j: (i, j, ucol + 1)),
                  _layer_spec(g, layer), _layer_spec(w_sp, layer), _layer_spec(b_rows, layer)],
        out_specs=pl.BlockSpec((1, rows, SGU_WIDTH), lambda i, j: (i, j, 0)),
        out_shape=jax.ShapeDtypeStruct((n, seq, SGU_WIDTH), F32),
        compiler_params=_params("parallel", "parallel"),
        name="sgu_prompt",
    )(z3, z3, g, w_sp, b_rows)


def _sgu_sample_kernel(u_ref, v_ref, g_ref, w_ref, b_ref, o_ref, gv_ref):
    c = SGU_CHUNK
    e = SGU_HEAD_DIM
    t = u_ref.shape[1]
    w = _sgu_causal(w_ref)
    gu = jax.nn.gelu(u_ref[0])
    gv = _sgu_norm_v(v_ref[0], g_ref[...])
    gv_ref[0] = gv
    gvp = jnp.concatenate([gv, jnp.zeros((c - t, SGU_WIDTH), F32)], axis=0).astype(BF16)
    for h in range(SGU_HEADS):
        mixed = jnp.dot(w[h], gvp[:, h * e:(h + 1) * e], preferred_element_type=F32) + b_ref[h]
        o_ref[0, :, h * e:(h + 1) * e] = gu[:, h * e:(h + 1) * e] * mixed[:t]


def sgu_sample(zs3, layer, g, w_sp, b_rows):
    n, t, _ = zs3.shape
    ucol = (3 * ATT_WIDTH + SSM_WIDTH) // SGU_WIDTH
    return pl.pallas_call(
        _sgu_sample_kernel,
        grid=(n,),
        in_specs=[pl.BlockSpec((1, t, SGU_WIDTH), lambda i: (i, 0, ucol)),
                  pl.BlockSpec((1, t, SGU_WIDTH), lambda i: (i, 0, ucol + 1)),
                  _layer_spec(g, layer), _layer_spec(w_sp, layer), _layer_spec(b_rows, layer)],
        out_specs=[pl.BlockSpec((1, t, SGU_WIDTH), lambda i: (i, 0, 0)),
                   pl.BlockSpec((1, t, SGU_WIDTH), lambda i: (i, 0, 0))],
        out_shape=[jax.ShapeDtypeStruct((n, t, SGU_WIDTH), F32),
                   jax.ShapeDtypeStruct((n, t, SGU_WIDTH), F32)],
        compiler_params=_params("parallel"),
        name="sgu_sample",
    )(zs3, zs3, g, w_sp, b_rows)


def _out_proj_kernel(att_ref, ssm_ref, sgu_ref, x_ref, w_ref, gm_ref, gp_ref, o_ref):
    a0 = ATT_WIDTH
    a1 = ATT_WIDTH + SSM_WIDTH
    a = _rms(att_ref[...], gm_ref[:, :a0]).astype(BF16)
    s = _rms(ssm_ref[...], gm_ref[:, a0:a1]).astype(BF16)
    c = _rms(sgu_ref[...], gm_ref[:, a1:]).astype(BF16)
    mixed = jnp.concatenate([a, s, c], axis=1)
    y = jnp.dot(mixed, w_ref[...], preferred_element_type=F32)
    o_ref[...] = x_ref[...] + _rms(y, gp_ref[...])


def out_proj(att, ssm, sgu, x, w, layer, gm, gp, tm):
    m, dm = x.shape
    row = lambda width: pl.BlockSpec((tm, width), lambda i: (i, 0))
    return pl.pallas_call(
        _out_proj_kernel,
        grid=(m // tm,),
        in_specs=[row(ATT_WIDTH), row(SSM_WIDTH), row(SGU_WIDTH), row(dm),
                  pl.BlockSpec(w.shape, lambda i: (0, 0)), _layer_spec(gm, layer), _layer_spec(gp, layer)],
        out_specs=row(dm),
        out_shape=jax.ShapeDtypeStruct((m, dm), F32),
        compiler_params=_params("parallel"),
        name="out_proj",
    )(att, ssm, sgu, x, w, gm, gp)


def _ffn_kernel(x_ref, gpre_ref, wg_ref, wu_ref, wd_ref, gpost_ref, *rest, n_cast):
    w32_refs, o_ref = rest[:n_cast], rest[n_cast]
    w16_refs, h_ref = rest[n_cast + 1:2 * n_cast + 1], rest[-1]
    j = pl.program_id(1)
    last = pl.num_programs(1) - 1
    tm = x_ref.shape[0]
    chunk = min(tm, FFN_EDGE_ROWS)

    def partial_out(h):
        gate = jnp.dot(h, wg_ref[...], preferred_element_type=F32)
        up = jnp.dot(h, wu_ref[...], preferred_element_type=F32)
        act = (jax.nn.silu(gate) * up).astype(BF16)
        return jnp.dot(act, wd_ref[...], preferred_element_type=F32)

    def cast_side_stream():
        for src_ref, dst_ref in zip(w32_refs, w16_refs):
            dst_ref[...] = src_ref[...].astype(BF16)

    @pl.when(j == 0)
    def _():
        for r in range(0, tm, chunk):
            rows = pl.ds(r, chunk)
            h = _rms(x_ref[rows, :], gpre_ref[...]).astype(BF16)
            h_ref[rows, :] = h
            o_ref[rows, :] = partial_out(h)
        cast_side_stream()

    @pl.when((j > 0) & (j < last))
    def _():
        o_ref[...] += partial_out(h_ref[...])
        cast_side_stream()

    @pl.when(j == last)
    def _():
        for r in range(0, tm, chunk):
            rows = pl.ds(r, chunk)
            f = o_ref[rows, :] + partial_out(h_ref[rows, :])
            o_ref[rows, :] = x_ref[rows, :] + _rms(f, gpost_ref[...])
        cast_side_stream()


def ffn(x, gpre, wg, wu, wd, layer, gpost, tm, tf, cast=()):
    m, dm = x.shape
    f = wg.shape[1]
    gi, gj = m // tm, f // tf
    assert gj >= 2
    def spread(dim, steps):
        return next(b for b in range(CAST_BLOCK, dim + 1, CAST_BLOCK) if dim % b == 0 and dim // b <= steps)

    cast_in, cast_out, cast_shape = [], [], []
    for w32, lyr, rows_follow_i in cast:
        _, r, c = w32.shape
        if rows_follow_i:
            assert r % gi == 0
            blk = (r // gi, spread(c, gj))
            imap = lambda i, j, last=c // blk[1] - 1: (i, jnp.minimum(j, last))
        else:
            assert c % gi == 0
            blk = (spread(r, gj), c // gi)
            imap = lambda i, j, last=r // blk[0] - 1: (jnp.minimum(j, last), i)
        cast_in.append(pl.BlockSpec((None,) + blk, lambda i, j, imap=imap, lyr=lyr: (lyr,) + imap(i, j)))
        cast_out.append(pl.BlockSpec(blk, imap))
        cast_shape.append(jax.ShapeDtypeStruct((r, c), BF16))
    kern = functools.partial(_ffn_kernel, n_cast=len(cast))
    res = pl.pallas_call(
        kern,
        grid=(gi, gj),
        in_specs=[pl.BlockSpec((tm, dm), lambda i, j: (i, 0)),
                  _layer_spec(gpre, layer),
                  pl.BlockSpec((dm, tf), lambda i, j: (0, j)),
                  pl.BlockSpec((dm, tf), lambda i, j: (0, j)),
                  pl.BlockSpec((tf, dm), lambda i, j: (j, 0)),
                  _layer_spec(gpost, layer)] + cast_in,
        out_specs=[pl.BlockSpec((tm, dm), lambda i, j: (i, 0))] + cast_out,
        out_shape=[jax.ShapeDtypeStruct((m, dm), F32)] + cast_shape,
        scratch_shapes=[pltpu.VMEM((tm, dm), BF16)],
        compiler_params=_params("arbitrary", "arbitrary"),
        name="ffn_cast" if cast else "ffn",
    )(x, gpre, wg, wu, wd, gpost, *[w for w, _, _ in cast])
    return res[0], res[1:]


def _t5_bucket(dist):
    max_exact = N_REL_BUCKETS // 2
    df = jnp.maximum(dist, 1).astype(F32)
    large = max_exact + (jnp.log(df / max_exact) / math.log(REL_MAX_DIST / max_exact)
                         * (N_REL_BUCKETS - max_exact)).astype(jnp.int32)
    large = jnp.minimum(large, N_REL_BUCKETS - 1)
    return jnp.where(dist < max_exact, dist, large)


def _strided_bias(rel_bias, dilation, n_steps):
    dist = jnp.arange(n_steps + 1, dtype=jnp.int32) * dilation
    return rel_bias[_t5_bucket(dist)].astype(F32)


def _bias_lookup(sb, steps, ok):
    onehot = (steps[..., None] == jnp.arange(sb.shape[0])).astype(F32)
    vals = jnp.einsum('...j,jh->h...', onehot, sb, precision=lax.Precision.HIGHEST)
    return jnp.where(ok[None], vals, MASKED)


def _prompt_bias_tables(rel_bias):
    qb = ATT_BLOCK
    qi = jnp.arange(qb)[:, None]
    kj = jnp.arange(2 * qb)[None, :]
    tables = []
    for window, d in DILATED_PATTERNS:
        nk = window // d
        sb = _strided_bias(rel_bias, d, nk)
        delta = qb + qi - kj
        normal = _bias_lookup(sb, jnp.clip(delta, 0, nk), (delta >= 0) & (delta <= nk))
        delta_f = qi - kj
        first = _bias_lookup(sb, jnp.clip(delta_f, 0, nk), (delta_f >= 0) & (delta_f <= nk))
        tables.append(jnp.stack([normal, first]))
    return jnp.stack(tables).astype(F32)


def _sample_bias_tables(rel_bias, w, t):
    d_far, tail, groups = _sample_cache_split(w, t)
    tq = jnp.arange(t)[:, None]
    old_slots = (jnp.arange(groups)[:, None] * d_far + jnp.arange(t)[None, :]).reshape(1, -1)
    old_delta = w + tq - old_slots
    tail_delta = w + tq - (w - tail + jnp.arange(tail))[None, :]
    new_delta = tq - jnp.arange(ATT_BLOCK)[None, :]
    out_t, out_n = [], []
    for window, d in DILATED_PATTERNS:
        nk = window // d
        sb = _strided_bias(rel_bias, d, nk)

        def table(delta, extra_ok):
            ok = (delta >= 0) & (delta % d == 0) & (delta <= nk * d) & extra_ok
            return _bias_lookup(sb, jnp.clip(delta // d, 0, nk), ok)

        out_t.append(table(tail_delta, True))
        out_n.append(table(new_delta, jnp.arange(ATT_BLOCK)[None, :] < t))
        if d == d_far:
            out_o = table(old_delta, True)
        else:
            assert window <= tail
    return out_o.astype(F32), jnp.stack(out_t).astype(F32), jnp.stack(out_n).astype(F32)


def _ssm_tables(lam_re, lam_im, log_step, b_re, b_im, c_re, c_im):
    g, p, cg = SSM_GROUPS, SSM_STATE, SSM_GROUP
    nq = SSM_QUARTERS
    gq = g // nq
    step = jnp.exp(log_step)[:, None]
    mag = jnp.exp(lam_re * step)
    a_re = mag * jnp.cos(lam_im * step)
    a_im = mag * jnp.sin(lam_im * step)
    den = lam_re * lam_re + lam_im * lam_im
    coef_re = ((a_re - 1.0) * lam_re + a_im * lam_im) / den
    coef_im = (a_im * lam_re - (a_re - 1.0) * lam_im) / den
    bb_re = coef_re[..., None] * b_re - coef_im[..., None] * b_im
    bb_im = coef_re[..., None] * b_im + coef_im[..., None] * b_re
    eye = jnp.eye(gq, dtype=F32)

    def in_mat(bb):
        bb = bb.reshape(nq, gq, p, cg)
        return jnp.einsum('kgpc,gh->kgchp', bb, eye).reshape(nq, gq * cg, gq * p)

    def out_mat(c):
        c = c.reshape(nq, gq, cg, p)
        return jnp.einsum('kgcp,gh->kgphc', c, eye).reshape(nq, gq * p, gq * cg)

    bq = jnp.concatenate([in_mat(bb_re), in_mat(bb_im)], axis=2)
    cq = jnp.concatenate([out_mat(c_re), -out_mat(c_im)], axis=1)
    aq = jnp.concatenate([a_re.reshape(nq, gq * p), a_im.reshape(nq, gq * p)], axis=1)
    a8 = jnp.concatenate([aq, aq], axis=0)
    return bq, cq, a8


def _rows(v):
    return v.reshape(v.shape[0], 1, -1)


def kernel(x_prompt, x_sample, cache_attn_k, cache_attn_v, state_ssm_re, state_ssm_im, rel_bias, w_in, w_out, g_pre_mix, g_post_mix, g_mix_out, ssm_lam_re, ssm_lam_im, ssm_log_step, ssm_b_re, ssm_b_im, ssm_c_re, ssm_c_im, ssm_d, ssm_w_glu, ssm_b_glu, sgu_g, sgu_w, sgu_b, g_pre_ffn, g_post_ffn, w_gate, w_up, w_down):
    nb, seq, dm = x_prompt.shape
    ns, ts, _ = x_sample.shape
    wbuf = cache_attn_k.shape[2]
    n_keep = min(DILATED_PATTERNS[-1][0], seq)
    h, e = ATT_HEADS, ATT_HEAD_DIM

    bias_p = _prompt_bias_tables(rel_bias)
    bias_o, bias_t, bias_n = _sample_bias_tables(rel_bias, wbuf, ts)
    k_buf = lax.empty((DEPTH, nb, n_keep * h, e), F32)
    v_buf = lax.empty((DEPTH, nb, n_keep * h, e), F32)
    st_re = state_ssm_re.reshape(DEPTH, ns, SSM_GROUPS * SSM_STATE)
    st_im = state_ssm_im.reshape(DEPTH, ns, SSM_GROUPS * SSM_STATE)

    xp = x_prompt.reshape(nb * seq, dm)
    xs = x_sample.reshape(ns * ts, dm)
    outs = {k: [] for k in ('rp', 'ip', 'ks', 'vs', 'rs', 'is', 'us')}

    big = (w_in, w_out, w_gate, w_up, w_down)
    big16 = tuple(w[0].astype(BF16) for w in big)
    w_glu16 = ssm_w_glu.astype(BF16)
    bq, cq, a8 = jax.vmap(_ssm_tables)(ssm_lam_re, ssm_lam_im, ssm_log_step,
                                       ssm_b_re, ssm_b_im, ssm_c_re, ssm_c_im)
    bq16, cq16 = bq.astype(BF16), cq.astype(BF16)
    sgu_b_rows = jnp.broadcast_to(sgu_b[..., None], (DEPTH, SGU_HEADS, SGU_CHUNK, SGU_HEAD_DIM))
    g_pre, g_post, g_mix = _rows(g_pre_mix), _rows(g_post_mix), _rows(g_mix_out)
    g_pre_f, g_post_f = _rows(g_pre_ffn), _rows(g_post_ffn)
    d_skip, b_glu, g_sgu = _rows(ssm_d), _rows(ssm_b_glu), _rows(sgu_g)

    for l in range(DEPTH):
        w_in16, w_out16, w_gate16, w_up16, w_down16 = big16
        cast = [(w, l + 1, w is not w_down) for w in big] if l + 1 < DEPTH else []

        z, k_buf, v_buf = in_proj(xp, g_pre, w_in16, l, tm=1024, tn=768,
                                  kv=(seq, n_keep, k_buf, v_buf))
        z = z.reshape(nb, seq, IN_COLS)
        o_att = attn_prompt(z, bias_p)
        o_ssm, h_fin = ssm_prompt(z, l, bq16, a8, cq16, d_skip, w_glu16, b_glu)
        o_sgu = sgu_prompt(z, l, g_sgu, sgu_w, sgu_b_rows)
        xp = out_proj(o_att.reshape(nb * seq, ATT_WIDTH), o_ssm.reshape(nb * seq, SSM_WIDTH),
                      o_sgu.reshape(nb * seq, SGU_WIDTH), xp, w_out16, l, g_mix, g_post, tm=512)
        xp, next16 = ffn(xp, g_pre_f, w_gate16, w_up16, w_down16, l, g_post_f, tm=1024, tf=256,
                         cast=cast)
        hq = h_fin.reshape(nb, SSM_QUARTERS, 2, SSM_GROUPS // SSM_QUARTERS, SSM_STATE)
        outs['rp'].append(hq[:, :, 0].reshape(nb, SSM_GROUPS, SSM_STATE))
        outs['ip'].append(hq[:, :, 1].reshape(nb, SSM_GROUPS, SSM_STATE))

        zs = in_proj(xs, g_pre, w_in16, l, tm=ns * ts, tn=2304).reshape(ns, ts, IN_COLS)
        s_att = attn_sample(zs, cache_attn_k, cache_attn_v, l, bias_o, bias_t, bias_n)
        s_ssm, s_re, s_im = ssm_sample(zs, l, st_re, st_im, bq, a8, cq, d_skip, w_glu16, b_glu)
        s_sgu, s_gv = sgu_sample(zs, l, g_sgu, sgu_w, sgu_b_rows)
        xs = out_proj(s_att.reshape(ns * ts, ATT_WIDTH), s_ssm.reshape(ns * ts, SSM_WIDTH),
                      s_sgu.reshape(ns * ts, SGU_WIDTH), xs, w_out16, l, g_mix, g_post, tm=ns * ts)
        xs, _ = ffn(xs, g_pre_f, w_gate16, w_up16, w_down16, l, g_post_f, tm=ns * ts, tf=1408)
        big16 = tuple(next16) or big16
        outs['ks'].append(zs[:, :, ATT_WIDTH:2 * ATT_WIDTH].reshape(ns, ts, h, e))
        outs['vs'].append(zs[:, :, 2 * ATT_WIDTH:3 * ATT_WIDTH].reshape(ns, ts, h, e))
        outs['rs'].append(s_re.reshape(ns, SSM_GROUPS, SSM_STATE))
        outs['is'].append(s_im.reshape(ns, SSM_GROUPS, SSM_STATE))
        outs['us'].append(s_gv)

    st = lambda key: jnp.stack(outs[key])
    kv_shape = (DEPTH, nb, n_keep, h, e)
    return (xp.reshape(nb, seq, dm), xs.reshape(ns, ts, dm),
            k_buf.reshape(kv_shape), v_buf.reshape(kv_shape), st('rp'), st('ip'),
            st('ks'), st('vs'), st('rs'), st('is'), st('us'))
```

```python
import functools
import math

import jax
import jax.numpy as jnp
import numpy as np
from jax import lax
from jax.experimental import pallas as pl
from jax.experimental.pallas import tpu as pltpu

F32 = jnp.float32
BF16 = jnp.bfloat16

D_MODEL = 2048
DEPTH = 4
ATT_HEADS = 8
ATT_HEAD_DIM = 128
ATT_WIDTH = ATT_HEADS * ATT_HEAD_DIM
DILATED_PATTERNS = ((128, 1), (512, 4), (2048, 16))
ATT_BLOCK = 128
ATT_GROUP = (8, 8, 4)
ATT_SCALE = ATT_HEAD_DIM ** -0.5
N_REL_BUCKETS = 32
REL_MAX_DIST = 2048
SSM_WIDTH = 512
SSM_GROUP = 16
SSM_GROUPS = SSM_WIDTH // SSM_GROUP
SSM_STATE = 64
SSM_QUARTERS = 4
SSM_QW = SSM_GROUPS * SSM_STATE // SSM_QUARTERS
SSM_QC = SSM_WIDTH // SSM_QUARTERS
SGU_WIDTH = 512
SGU_HEADS = 4
SGU_HEAD_DIM = SGU_WIDTH // SGU_HEADS
SGU_CHUNK = 128
IN_COLS = 3 * ATT_WIDTH + SSM_WIDTH + 2 * SGU_WIDTH
FFN_HIDDEN = 5632
EPS = 1e-6
MASKED = -float("inf")
LANES = 128
CAST_BLOCK = 256
FFN_EDGE_ROWS = 256

VMEM_LIMIT_BYTES = 56 * 1024 * 1024


def _params(*sem):
    return pltpu.CompilerParams(dimension_semantics=sem, vmem_limit_bytes=VMEM_LIMIT_BYTES)


def _layer_spec(arr, layer):
    zeros = (0,) * (arr.ndim - 1)
    return pl.BlockSpec((None,) + arr.shape[1:], lambda *_: (layer,) + zeros)


def _rms(x, g):
    return x * lax.rsqrt(jnp.mean(x * x, axis=-1, keepdims=True) + EPS) * g


def _in_proj_kernel(x_ref, g_ref, w_ref, *rest, keep_from, tiles_per_seq):
    h_ref = rest[-1]
    outs = rest[-4:-1] if len(rest) > 2 else rest[:1]
    z_ref = outs[0]
    i = pl.program_id(0)
    j = pl.program_id(1)

    tm, tn = z_ref.shape
    chunk = min(tm, FFN_EDGE_ROWS)

    @pl.when(j == 0)
    def _():
        for r in range(0, tm, chunk):
            rows = pl.ds(r, chunk)
            h = _rms(x_ref[rows, :], g_ref[...]).astype(BF16)
            h_ref[rows, :] = h
            z_ref[rows, :] = jnp.dot(h, w_ref[...], preferred_element_type=F32)

    @pl.when(j > 0)
    def _():
        _in_proj_tile(i, j, h_ref, w_ref, outs, keep_from, tiles_per_seq)


def _in_proj_tile(i, j, h_ref, w_ref, outs, keep_from, tiles_per_seq):
    z_ref = outs[0]
    res = jnp.dot(h_ref[...], w_ref[...], preferred_element_type=F32)
    z_ref[...] = res

    if len(outs) > 1:
        tm, tn = res.shape
        keep = (i % tiles_per_seq) >= keep_from
        by_tile = {}
        for buf_ref, col0 in ((outs[1], ATT_WIDTH), (outs[2], 2 * ATT_WIDTH)):
            for head in range(ATT_HEADS):
                col = col0 + head * ATT_HEAD_DIM
                by_tile.setdefault(col // tn, []).append((buf_ref, head, col % tn))
        for tile, heads in by_tile.items():
            @pl.when(keep & (j == tile))
            def _(heads=heads):
                for buf_ref, head, off in heads:
                    rows = pl.ds(head, tm, stride=ATT_HEADS)
                    buf_ref[0, 0, rows, :] = res[:, off:off + ATT_HEAD_DIM]


def in_proj(x, g, w, layer, tm, tn, kv=None):
    m, k = x.shape
    n = w.shape[1]
    assert tn % ATT_HEAD_DIM == 0
    in_specs = [pl.BlockSpec((tm, k), lambda i, j: (i, 0)),
                _layer_spec(g, layer),
                pl.BlockSpec((k, tn), lambda i, j: (0, j))]
    z_spec = pl.BlockSpec((tm, tn), lambda i, j: (i, j))
    z_shape = jax.ShapeDtypeStruct((m, n), F32)
    scratch = [pltpu.VMEM((tm, k), BF16)]
    if kv is None:
        kern = functools.partial(_in_proj_kernel, keep_from=0, tiles_per_seq=1)
        return pl.pallas_call(
            kern, grid=(m // tm, n // tn), in_specs=in_specs, out_specs=z_spec, out_shape=z_shape,
            scratch_shapes=scratch, compiler_params=_params("parallel", "arbitrary"),
            name="in_proj",
        )(x, g, w)
    seq, n_keep, k_buf, v_buf = kv
    tiles_per_seq = seq // tm
    keep_from = (seq - n_keep) // tm
    assert seq % tm == 0 and (seq - n_keep) % tm == 0
    assert tn <= ATT_WIDTH

    def buf_map(i, j):
        return (layer, i // tiles_per_seq, jnp.maximum(i % tiles_per_seq - keep_from, 0), 0)

    buf_spec = pl.BlockSpec((1, 1, tm * ATT_HEADS, ATT_HEAD_DIM), buf_map)
    any_spec = pl.BlockSpec(memory_space=pl.ANY)
    kern = functools.partial(_in_proj_kernel, keep_from=keep_from, tiles_per_seq=tiles_per_seq)
    return pl.pallas_call(
        kern, grid=(m // tm, n // tn),
        in_specs=in_specs + [any_spec, any_spec],
        out_specs=[z_spec, buf_spec, buf_spec],
        out_shape=[z_shape, jax.ShapeDtypeStruct(k_buf.shape, F32),
                   jax.ShapeDtypeStruct(v_buf.shape, F32)],
        input_output_aliases={3: 1, 4: 2},
        scratch_shapes=scratch, compiler_params=_params("arbitrary", "arbitrary"),
        name="in_proj_kv",
    )(x, g, w, k_buf, v_buf)


def _attn_prompt_kernel(q_ref, k_ref, v_ref, b_ref, o_ref, acc_ref, m_ref, l_ref, *, seq):
    qb = ATT_BLOCK
    n_pat = len(DILATED_PATTERNS)

    for p, (_, d) in enumerate(DILATED_PATTERNS):
        shift = int(math.log2(d))
        whole = seq // d == 2 * qb
        blocks_per_trip = ATT_GROUP[p]
        units = blocks_per_trip // 2 if whole else blocks_per_trip

        def group(g, carry, p=p, d=d, shift=shift, whole=whole, units=units):
            rows, biases = [], []
            for u in range(units):
                i = g * units + u
                if whole:
                    rows.append((pl.ds(i, 2 * qb, stride=d), pl.ds(i, 2 * qb, stride=d)))
                    biases.append(None)
                    continue
                r = i & (d - 1)
                b = i >> shift
                q_start = r + b * (qb * d)
                k_start = r + jnp.maximum(b - 1, 0) * (qb * d)
                if d == 1:
                    rows.append((pl.ds(pl.multiple_of(q_start, qb), qb),
                                 pl.ds(pl.multiple_of(k_start, qb), 2 * qb)))
                else:
                    rows.append((pl.ds(q_start, qb, stride=d), pl.ds(k_start, 2 * qb, stride=d)))
                biases.append(jnp.where(b == 0, 1, 0))
            nt = (((1,), (1,)), ((), ()))
            scores = [lax.dot_general(q_ref[0, qr, :].astype(BF16), k_ref[0, kr, :].astype(BF16), nt,
                                      preferred_element_type=F32) for qr, kr in rows]
            probs = []
            for (qr, _), first, s in zip(rows, biases, scores):
                if whole:
                    bias = jnp.concatenate([b_ref[p, 1, 0], b_ref[p, 0, 0]], axis=0)
                else:
                    bias = b_ref[p, first, 0]
                s = s * ATT_SCALE + bias
                m_blk = jnp.max(s, axis=1, keepdims=True)
                pe = jnp.exp(s - m_blk)
                wide = (s.shape[0], ATT_HEAD_DIM)
                m_ref[p, qr, :] = jnp.broadcast_to(m_blk, wide)
                l_ref[p, qr, :] = jnp.broadcast_to(jnp.sum(pe, axis=1, keepdims=True), wide)
                probs.append(pe.astype(BF16))
            for (qr, kr), pe in zip(rows, probs):
                acc_ref[p, qr, :] = jnp.dot(pe, v_ref[0, kr, :].astype(BF16),
                                            preferred_element_type=F32)
            return carry

        lax.fori_loop(0, seq // qb // blocks_per_trip, group, 0)

    rows_per = 2 * qb

    def merge(c, carry):
        rows = pl.ds(pl.multiple_of(c * rows_per, rows_per), rows_per)
        ms = [m_ref[p, rows, :] for p in range(n_pat)]
        m = functools.reduce(jnp.maximum, ms)
        num = den = None
        for p in range(n_pat):
            w = jnp.exp(ms[p] - m)
            num = w * acc_ref[p, rows, :] if num is None else num + w * acc_ref[p, rows, :]
            den = w * l_ref[p, rows, :] if den is None else den + w * l_ref[p, rows, :]
        o_ref[0, rows, :] = num / den
        return carry

    lax.fori_loop(0, seq // rows_per, merge, 0)


def attn_prompt(z3, bias):
    n, seq, _ = z3.shape
    h = ATT_HEADS
    kern = functools.partial(_attn_prompt_kernel, seq=seq)
    return pl.pallas_call(
        kern,
        grid=(n, h),
        in_specs=[pl.BlockSpec((1, seq, ATT_HEAD_DIM), lambda i, j: (i, 0, j)),
                  pl.BlockSpec((1, seq, ATT_HEAD_DIM), lambda i, j: (i, 0, h + j)),
                  pl.BlockSpec((1, seq, ATT_HEAD_DIM), lambda i, j: (i, 0, 2 * h + j)),
                  pl.BlockSpec((3, 2, 1, ATT_BLOCK, 2 * ATT_BLOCK), lambda i, j: (0, 0, j, 0, 0))],
        out_specs=pl.BlockSpec((1, seq, ATT_HEAD_DIM), lambda i, j: (i, 0, j)),
        out_shape=jax.ShapeDtypeStruct((n, seq, ATT_WIDTH), F32),
        scratch_shapes=[pltpu.VMEM((len(DILATED_PATTERNS), seq, ATT_HEAD_DIM), F32)] * 3,
        compiler_params=_params("parallel", "parallel"),
        name="attn_prompt",
    )(z3, z3, z3, bias)


def _attn_sample_kernel(q_ref, kn_ref, vn_ref, ko_ref, vo_ref, kt_ref, vt_ref,
                        bo_ref, bt_ref, bn_ref, o_ref):
    t = q_ref.shape[1]
    e = ATT_HEAD_DIM
    nh = ATT_HEADS
    n_pat = len(DILATED_PATTERNS)
    groups, kept_rows, _ = ko_ref.shape
    kept = kept_rows // nh
    tail = kt_ref.shape[0] // nh
    pad = jnp.zeros((ATT_BLOCK - t, e), F32)
    nt = (((1,), (1,)), ((), ()))
    heads = range(nh)
    cols = [slice(h * e, (h + 1) * e) for h in heads]

    def old_rows(ref, h):
        return ref[:, pl.ds(h, kept, stride=nh), :].reshape(groups * kept, e).astype(BF16)

    def tail_rows(ref, h):
        return ref[pl.ds(h, tail, stride=nh), :].astype(BF16)

    so, st, sn = [], [], []
    for h in heads:
        q = q_ref[0, :, cols[h]].astype(BF16)
        kn = jnp.concatenate([kn_ref[0, :, cols[h]], pad], axis=0).astype(BF16)
        so.append(lax.dot_general(q, old_rows(ko_ref, h), nt, preferred_element_type=F32) * ATT_SCALE)
        st.append(lax.dot_general(q, tail_rows(kt_ref, h), nt, preferred_element_type=F32) * ATT_SCALE)
        sn.append(lax.dot_general(q, kn, nt, preferred_element_type=F32) * ATT_SCALE)
    po, pt, pn, ls = [], [], [], []
    for h in heads:
        s_old = so[h] + bo_ref[h]
        sts = [st[h] + bt_ref[p, h] for p in range(n_pat)]
        sns = [sn[h] + bn_ref[p, h] for p in range(n_pat)]
        m = None
        for s in [s_old] + sts + sns:
            mx = jnp.max(s, axis=1, keepdims=True)
            m = mx if m is None else jnp.maximum(m, mx)
        poh = jnp.exp(s_old - m)
        pth = sum(jnp.exp(s - m) for s in sts)
        pnh = sum(jnp.exp(s - m) for s in sns)
        ls.append(jnp.sum(poh, axis=1, keepdims=True) + jnp.sum(pth, axis=1, keepdims=True)
                  + jnp.sum(pnh, axis=1, keepdims=True))
        po.append(poh.astype(BF16))
        pt.append(pth.astype(BF16))
        pn.append(pnh.astype(BF16))
    for h in heads:
        vn = jnp.concatenate([vn_ref[0, :, cols[h]], pad], axis=0).astype(BF16)
        acc = (jnp.dot(po[h], old_rows(vo_ref, h), preferred_element_type=F32)
               + jnp.dot(pt[h], tail_rows(vt_ref, h), preferred_element_type=F32)
               + jnp.dot(pn[h], vn, preferred_element_type=F32))
        o_ref[0, :, cols[h]] = acc / ls[h]


def _sample_cache_split(w, t):
    d_far = DILATED_PATTERNS[-1][1]
    tail = max(window for window, _ in DILATED_PATTERNS[:-1])
    assert t <= d_far and w % d_far == 0 and tail % d_far == 0 and tail <= w
    return d_far, tail, (w - tail) // d_far


def attn_sample(zs3, cache_k, cache_v, layer, bias_o, bias_t, bias_n):
    n, t, _ = zs3.shape
    h = ATT_HEADS
    e = ATT_HEAD_DIM
    depth, _, w = cache_k.shape[:3]
    d_far, tail, groups = _sample_cache_split(w, t)
    by_group = lambda c: c.reshape(depth, n, w // d_far, d_far * h, e)
    by_tail = lambda c: c.reshape(depth, n, w // tail, tail * h, e)
    old_spec = pl.BlockSpec((None, None, groups, t * h, e), lambda i: (layer, i, 0, 0, 0))
    tail_spec = pl.BlockSpec((None, None, None, tail * h, e), lambda i: (layer, i, w // tail - 1, 0, 0))
    return pl.pallas_call(
        _attn_sample_kernel,
        grid=(n,),
        in_specs=[pl.BlockSpec((1, t, ATT_WIDTH), lambda i: (i, 0, 0)),
                  pl.BlockSpec((1, t, ATT_WIDTH), lambda i: (i, 0, 1)),
                  pl.BlockSpec((1, t, ATT_WIDTH), lambda i: (i, 0, 2)),
                  old_spec, old_spec, tail_spec, tail_spec,
                  pl.BlockSpec(bias_o.shape, lambda i: (0, 0, 0)),
                  pl.BlockSpec(bias_t.shape, lambda i: (0, 0, 0, 0)),
                  pl.BlockSpec(bias_n.shape, lambda i: (0, 0, 0, 0))],
        out_specs=pl.BlockSpec((1, t, ATT_WIDTH), lambda i: (i, 0, 0)),
        out_shape=jax.ShapeDtypeStruct((n, t, ATT_WIDTH), F32),
        compiler_params=_params("parallel"),
        name="attn_sample",
    )(zs3, zs3, zs3, by_group(cache_k), by_group(cache_v), by_tail(cache_k), by_tail(cache_v),
      bias_o, bias_t, bias_n)


def _ssm_prompt_kernel(u_ref, bq_ref, a_ref, cq_ref, d_ref, wg_ref, bg_ref,
                       o_ref, hre_ref, him_ref, x_ref, h_ref, *, tb, pitch):
    step = pl.program_id(0)
    nq = SSM_QUARTERS
    qw = SSM_QW
    qc = SSM_QC

    @pl.when(step == 0)
    def _():
        h_ref[...] = jnp.zeros_like(h_ref)

    nl = 2 * qw // LANES
    hl = nl // 2

    for b in range(2):
        for k in range(nq):
            u = u_ref[b, :, k * qc:(k + 1) * qc].astype(BF16)
            x = jnp.dot(u, bq_ref[k], preferred_element_type=F32)
            for c in range(nl):
                x_ref[c, pl.ds((b * nq + k) * pitch, tb), :] = x[:, c * LANES:(c + 1) * LANES]

    a = [a_ref[:, c * LANES:(c + 1) * LANES] for c in range(nl)]

    def scan(t, h):
        rows = pl.ds(t, 2 * nq, stride=pitch)
        new = [None] * nl
        for c in range(hl):
            a_re, a_im, h_re, h_im = a[c], a[hl + c], h[c], h[hl + c]
            new[c] = a_re * h_re - a_im * h_im + x_ref[c, rows, :]
            new[hl + c] = a_re * h_im + a_im * h_re + x_ref[hl + c, rows, :]
        for c in range(nl):
            x_ref[c, rows, :] = new[c]
        return tuple(new)

    h0 = tuple(h_ref[:, c * LANES:(c + 1) * LANES] for c in range(nl))
    h_new = jnp.concatenate(lax.fori_loop(0, tb, scan, h0, unroll=8), axis=1)
    h_ref[...] = h_new

    for b in range(2):
        ys = []
        for k in range(nq):
            rows = pl.ds((b * nq + k) * pitch, tb)
            hs = jnp.concatenate([x_ref[c, rows, :] for c in range(nl)], axis=1).astype(BF16)
            ys.append(jnp.dot(hs, cq_ref[k], preferred_element_type=F32))
        y = jnp.concatenate(ys, axis=1) + d_ref[...] * u_ref[b]
        gate = jnp.dot(y.astype(BF16), wg_ref[...], preferred_element_type=F32) + bg_ref[...]
        o_ref[b] = y * jax.nn.sigmoid(gate)

    @pl.when(step == pl.num_programs(0) - 1)
    def _():
        hre_ref[...] = h_new[:, :qw]
        him_ref[...] = h_new[:, qw:]


def ssm_prompt(z3, layer, bq, a8, cq, d_skip, w_glu, b_glu, tb=512):
    n, seq, _ = z3.shape
    assert n == 2
    pitch = tb + 8
    ucol = 3 * ATT_WIDTH // SSM_WIDTH
    kern = functools.partial(_ssm_prompt_kernel, tb=tb, pitch=pitch)
    full = lambda *shape: pl.BlockSpec(shape, lambda s: (0,) * len(shape))
    return pl.pallas_call(
        kern,
        grid=(seq // tb,),
        in_specs=[pl.BlockSpec((n, tb, SSM_WIDTH), lambda s: (0, s, ucol))]
        + [_layer_spec(p, layer) for p in (bq, a8, cq, d_skip, w_glu, b_glu)],
        out_specs=[pl.BlockSpec((n, tb, SSM_WIDTH), lambda s: (0, s, 0)),
                   full(2 * SSM_QUARTERS, SSM_QW), full(2 * SSM_QUARTERS, SSM_QW)],
        out_shape=[jax.ShapeDtypeStruct((n, seq, SSM_WIDTH), F32),
                   jax.ShapeDtypeStruct((2 * SSM_QUARTERS, SSM_QW), F32),
                   jax.ShapeDtypeStruct((2 * SSM_QUARTERS, SSM_QW), F32)],
        scratch_shapes=[pltpu.VMEM((2 * SSM_QW // LANES, 2 * SSM_QUARTERS * pitch, LANES), F32),
                        pltpu.VMEM((2 * SSM_QUARTERS, 2 * SSM_QW), F32)],
        compiler_params=_params("arbitrary"),
        name="ssm_prompt",
    )(z3, bq, a8, cq, d_skip, w_glu, b_glu)


def _ssm_sample_kernel(u_ref, hre_ref, him_ref, bq_ref, a_ref, cq_ref, d_ref, wg_ref, bg_ref,
                       o_ref, ore_ref, oim_ref, x_ref):
    n, t, _ = u_ref.shape
    nq = SSM_QUARTERS
    qw = SSM_QW
    qc = SSM_QC
    nl = 2 * qw // LANES
    hl = nl // 2
    u_all = u_ref[...].reshape(n * t, SSM_WIDTH)
    ys = []
    for k in range(nq):
        u = u_all[:, k * qc:(k + 1) * qc]
        x = jnp.dot(u, bq_ref[k], preferred_element_type=F32, precision=lax.Precision.HIGHEST)
        for c in range(nl):
            x_ref[c] = x[:, c * LANES:(c + 1) * LANES]
        h = ([hre_ref[:, k * qw + c * LANES:k * qw + (c + 1) * LANES] for c in range(hl)]
             + [him_ref[:, k * qw + c * LANES:k * qw + (c + 1) * LANES] for c in range(hl)])
        for s in range(t):
            rows = pl.ds(s, n, stride=t)
            new = [None] * nl
            for c in range(hl):
                a_re = a_ref[k:k + 1, c * LANES:(c + 1) * LANES]
                a_im = a_ref[k:k + 1, (hl + c) * LANES:(hl + c + 1) * LANES]
                new[c] = a_re * h[c] - a_im * h[hl + c] + x_ref[c, rows, :]
                new[hl + c] = a_re * h[hl + c] + a_im * h[c] + x_ref[hl + c, rows, :]
            for c in range(nl):
                x_ref[c, rows, :] = new[c]
            h = new
        for c in range(hl):
            ore_ref[:, k * qw + c * LANES:k * qw + (c + 1) * LANES] = h[c]
            oim_ref[:, k * qw + c * LANES:k * qw + (c + 1) * LANES] = h[hl + c]
        hs = jnp.concatenate([x_ref[c] for c in range(nl)], axis=1).astype(BF16)
        ys.append(jnp.dot(hs, cq_ref[k].astype(BF16), preferred_element_type=F32))
    y = jnp.concatenate(ys, axis=1) + d_ref[...] * u_all
    gate = jnp.dot(y.astype(BF16), wg_ref[...], preferred_element_type=F32) + bg_ref[...]
    o_ref[...] = (y * jax.nn.sigmoid(gate)).reshape(n, t, SSM_WIDTH)


def ssm_sample(zs3, layer, h_re, h_im, bq32, a8, cq32, d_skip, w_glu, b_glu):
    n, t, _ = zs3.shape
    ucol = 3 * ATT_WIDTH // SSM_WIDTH
    ns = SSM_GROUPS * SSM_STATE
    full = lambda *shape: pl.BlockSpec(shape, lambda s: (0,) * len(shape))
    return pl.pallas_call(
        _ssm_sample_kernel,
        grid=(1,),
        in_specs=[pl.BlockSpec((n, t, SSM_WIDTH), lambda s: (0, 0, ucol))]
        + [_layer_spec(p, layer) for p in (h_re, h_im, bq32, a8, cq32, d_skip, w_glu, b_glu)],
        out_specs=[full(n, t, SSM_WIDTH), full(n, ns), full(n, ns)],
        out_shape=[jax.ShapeDtypeStruct((n, t, SSM_WIDTH), F32),
                   jax.ShapeDtypeStruct((n, ns), F32),
                   jax.ShapeDtypeStruct((n, ns), F32)],
        scratch_shapes=[pltpu.VMEM((2 * SSM_QW // LANES, n * t, LANES), F32)],
        compiler_params=_params("arbitrary"),
        name="ssm_sample",
    )(zs3, h_re, h_im, bq32, a8, cq32, d_skip, w_glu, b_glu)


def _sgu_norm_v(v, g):
    gv = jax.nn.gelu(v)
    vc = gv - jnp.mean(gv, axis=-1, keepdims=True)
    return vc * lax.rsqrt(jnp.mean(vc * vc, axis=-1, keepdims=True) + EPS) * g


def _sgu_causal(w_ref):
    c = SGU_CHUNK
    keep = lax.broadcasted_iota(jnp.int32, (c, c), 0) >= lax.broadcasted_iota(jnp.int32, (c, c), 1)
    return [jnp.where(keep, w_ref[h], 0.0).astype(BF16) for h in range(SGU_HEADS)]


def _sgu_prompt_kernel(u_ref, v_ref, g_ref, w_ref, b_ref, o_ref):
    c = SGU_CHUNK
    e = SGU_HEAD_DIM
    rows = u_ref.shape[1]
    w = _sgu_causal(w_ref)
    gu = jax.nn.gelu(u_ref[0])
    gv = _sgu_norm_v(v_ref[0], g_ref[...]).astype(BF16)
    for ci in range(rows // c):
        for h in range(SGU_HEADS):
            mixed = jnp.dot(w[h], gv[ci * c:(ci + 1) * c, h * e:(h + 1) * e],
                            preferred_element_type=F32) + b_ref[h]
            o_ref[0, ci * c:(ci + 1) * c, h * e:(h + 1) * e] = (
                gu[ci * c:(ci + 1) * c, h * e:(h + 1) * e] * mixed)


def sgu_prompt(z3, layer, g, w_sp, b_rows, rows=1024):
    n, seq, _ = z3.shape
    ucol = (3 * ATT_WIDTH + SSM_WIDTH) // SGU_WIDTH
    return pl.pallas_call(
        _sgu_prompt_kernel,
        grid=(n, seq // rows),
        in_specs=[pl.BlockSpec((1, rows, SGU_WIDTH), lambda i, j: (i, j, ucol)),
                  pl.BlockSpec((1, rows, SGU_WIDTH), lambda i, j: (i, j, ucol + 1)),
                  _layer_spec(g, layer), _layer_spec(w_sp, layer), _layer_spec(b_rows, layer)],
        out_specs=pl.BlockSpec((1, rows, SGU_WIDTH), lambda i, j: (i, j, 0)),
        out_shape=jax.ShapeDtypeStruct((n, seq, SGU_WIDTH), F32),
        compiler_params=_params("parallel", "parallel"),
        name="sgu_prompt",
    )(z3, z3, g, w_sp, b_rows)


def _sgu_sample_kernel(u_ref, v_ref, g_ref, w_ref, b_ref, o_ref, gv_ref):
    c = SGU_CHUNK
    e = SGU_HEAD_DIM
    n, t, _ = u_ref.shape
    w = _sgu_causal(w_ref)
    for i in range(n):
        gu = jax.nn.gelu(u_ref[i])
        gv = _sgu_norm_v(v_ref[i], g_ref[...])
        gv_ref[i] = gv
        gvp = jnp.concatenate([gv, jnp.zeros((c - t, SGU_WIDTH), F32)], axis=0).astype(BF16)
        for h in range(SGU_HEADS):
            mixed = jnp.dot(w[h], gvp[:, h * e:(h + 1) * e], preferred_element_type=F32) + b_ref[h]
            o_ref[i, :, h * e:(h + 1) * e] = gu[:, h * e:(h + 1) * e] * mixed[:t]


def sgu_sample(zs3, layer, g, w_sp, b_rows):
    n, t, _ = zs3.shape
    ucol = (3 * ATT_WIDTH + SSM_WIDTH) // SGU_WIDTH
    return pl.pallas_call(
        _sgu_sample_kernel,
        grid=(1,),
        in_specs=[pl.BlockSpec((n, t, SGU_WIDTH), lambda i: (0, 0, ucol)),
                  pl.BlockSpec((n, t, SGU_WIDTH), lambda i: (0, 0, ucol + 1)),
                  _layer_spec(g, layer), _layer_spec(w_sp, layer), _layer_spec(b_rows, layer)],
        out_specs=[pl.BlockSpec((n, t, SGU_WIDTH), lambda i: (0, 0, 0)),
                   pl.BlockSpec((n, t, SGU_WIDTH), lambda i: (0, 0, 0))],
        out_shape=[jax.ShapeDtypeStruct((n, t, SGU_WIDTH), F32),
                   jax.ShapeDtypeStruct((n, t, SGU_WIDTH), F32)],
        compiler_params=_params("arbitrary"),
        name="sgu_sample",
    )(zs3, zs3, g, w_sp, b_rows)


def _out_proj_kernel(att_ref, ssm_ref, sgu_ref, x_ref, w_ref, gm_ref, gp_ref, o_ref):
    a0 = ATT_WIDTH
    a1 = ATT_WIDTH + SSM_WIDTH
    a = _rms(att_ref[...], gm_ref[:, :a0]).astype(BF16)
    s = _rms(ssm_ref[...], gm_ref[:, a0:a1]).astype(BF16)
    c = _rms(sgu_ref[...], gm_ref[:, a1:]).astype(BF16)
    mixed = jnp.concatenate([a, s, c], axis=1)
    y = jnp.dot(mixed, w_ref[...], preferred_element_type=F32)
    o_ref[...] = x_ref[...] + _rms(y, gp_ref[...])


def out_proj(att, ssm, sgu, x, w, layer, gm, gp, tm):
    m, dm = x.shape
    row = lambda width: pl.BlockSpec((tm, width), lambda i: (i, 0))
    return pl.pallas_call(
        _out_proj_kernel,
        grid=(m // tm,),
        in_specs=[row(ATT_WIDTH), row(SSM_WIDTH), row(SGU_WIDTH), row(dm),
                  pl.BlockSpec(w.shape, lambda i: (0, 0)), _layer_spec(gm, layer), _layer_spec(gp, layer)],
        out_specs=row(dm),
        out_shape=jax.ShapeDtypeStruct((m, dm), F32),
        compiler_params=_params("parallel"),
        name="out_proj",
    )(att, ssm, sgu, x, w, gm, gp)


def _ffn_kernel(x_ref, gpre_ref, wg_ref, wu_ref, wd_ref, gpost_ref, *rest, n_cast):
    w32_refs, o_ref = rest[:n_cast], rest[n_cast]
    w16_refs, h_ref = rest[n_cast + 1:2 * n_cast + 1], rest[-1]
    j = pl.program_id(1)
    last = pl.num_programs(1) - 1
    tm = x_ref.shape[0]
    chunk = min(tm, FFN_EDGE_ROWS)

    def partial_out(h):
        gate = jnp.dot(h, wg_ref[...], preferred_element_type=F32)
        up = jnp.dot(h, wu_ref[...], preferred_element_type=F32)
        act = (jax.nn.silu(gate) * up).astype(BF16)
        return jnp.dot(act, wd_ref[...], preferred_element_type=F32)

    def cast_side_stream():
        for src_ref, dst_ref in zip(w32_refs, w16_refs):
            dst_ref[...] = src_ref[...].astype(BF16)

    @pl.when(j == 0)
    def _():
        for r in range(0, tm, chunk):
            rows = pl.ds(r, chunk)
            h = _rms(x_ref[rows, :], gpre_ref[...]).astype(BF16)
            h_ref[rows, :] = h
            o_ref[rows, :] = partial_out(h)
        cast_side_stream()

    @pl.when((j > 0) & (j < last))
    def _():
        o_ref[...] += partial_out(h_ref[...])
        cast_side_stream()

    @pl.when(j == last)
    def _():
        for r in range(0, tm, chunk):
            rows = pl.ds(r, chunk)
            f = o_ref[rows, :] + partial_out(h_ref[rows, :])
            o_ref[rows, :] = x_ref[rows, :] + _rms(f, gpost_ref[...])
        cast_side_stream()


def ffn(x, gpre, wg, wu, wd, layer, gpost, tm, tf, cast=()):
    m, dm = x.shape
    f = wg.shape[1]
    gi, gj = m // tm, f // tf
    assert gj >= 2
    def spread(dim, steps):
        return next(b for b in range(CAST_BLOCK, dim + 1, CAST_BLOCK) if dim % b == 0 and dim // b <= steps)

    cast_in, cast_out, cast_shape = [], [], []
    for w32, lyr, rows_follow_i in cast:
        _, r, c = w32.shape
        if rows_follow_i:
            assert r % gi == 0
            blk = (r // gi, spread(c, gj))
            imap = lambda i, j, last=c // blk[1] - 1: (i, jnp.minimum(j, last))
        else:
            assert c % gi == 0
            blk = (spread(r, gj), c // gi)
            imap = lambda i, j, last=r // blk[0] - 1: (jnp.minimum(j, last), i)
        cast_in.append(pl.BlockSpec((None,) + blk, lambda i, j, imap=imap, lyr=lyr: (lyr,) + imap(i, j)))
        cast_out.append(pl.BlockSpec(blk, imap))
        cast_shape.append(jax.ShapeDtypeStruct((r, c), BF16))
    kern = functools.partial(_ffn_kernel, n_cast=len(cast))
    res = pl.pallas_call(
        kern,
        grid=(gi, gj),
        in_specs=[pl.BlockSpec((tm, dm), lambda i, j: (i, 0)),
                  _layer_spec(gpre, layer),
                  pl.BlockSpec((dm, tf), lambda i, j: (0, j)),
                  pl.BlockSpec((dm, tf), lambda i, j: (0, j)),
                  pl.BlockSpec((tf, dm), lambda i, j: (j, 0)),
                  _layer_spec(gpost, layer)] + cast_in,
        out_specs=[pl.BlockSpec((tm, dm), lambda i, j: (i, 0))] + cast_out,
        out_shape=[jax.ShapeDtypeStruct((m, dm), F32)] + cast_shape,
        scratch_shapes=[pltpu.VMEM((tm, dm), BF16)],
        compiler_params=_params("arbitrary", "arbitrary"),
        name="ffn_cast" if cast else "ffn",
    )(x, gpre, wg, wu, wd, gpost, *[w for w, _, _ in cast])
    return res[0], res[1:]


def _t5_bucket(dist):
    max_exact = N_REL_BUCKETS // 2
    df = jnp.maximum(dist, 1).astype(F32)
    large = max_exact + (jnp.log(df / max_exact) / math.log(REL_MAX_DIST / max_exact)
                         * (N_REL_BUCKETS - max_exact)).astype(jnp.int32)
    large = jnp.minimum(large, N_REL_BUCKETS - 1)
    return jnp.where(dist < max_exact, dist, large)


def _strided_bias(rel_bias, dilation, n_steps):
    dist = jnp.arange(n_steps + 1, dtype=jnp.int32) * dilation
    return rel_bias[_t5_bucket(dist)].astype(F32)


def _bias_lookup(sb, steps, ok):
    onehot = (steps[..., None] == jnp.arange(sb.shape[0])).astype(F32)
    vals = jnp.einsum('...j,jh->h...', onehot, sb, precision=lax.Precision.HIGHEST)
    return jnp.where(ok[None], vals, MASKED)


def _prompt_bias_tables(rel_bias):
    qb = ATT_BLOCK
    span = jnp.arange(3 * qb - 1)
    tables = []
    for window, d in DILATED_PATTERNS:
        nk = window // d
        sb = _strided_bias(rel_bias, d, nk)

        def band(delta):
            line = _bias_lookup(sb, jnp.clip(delta, 0, nk), (delta >= 0) & (delta <= nk))
            return _toeplitz(line, qb, 2 * qb)

        tables.append(jnp.stack([band(span - (qb - 1)), band(span - (2 * qb - 1))]))
    return jnp.stack(tables).astype(F32)


def _toeplitz(line, n_rows, n_cols):
    lh = n_rows + n_cols - 1
    w = jnp.concatenate([line[..., ::-1], jnp.zeros(line.shape[:-1] + (1,), line.dtype)], axis=-1)
    tiled = jnp.tile(w, (1,) * (w.ndim - 1) + (n_rows,))[..., :n_rows * lh]
    return tiled.reshape(line.shape[:-1] + (n_rows, lh))[..., n_rows - 1:n_rows - 1 + n_cols]


def _sample_bias_tables(rel_bias, w, t):
    d_far, tail, groups = _sample_cache_split(w, t)
    tq = jnp.arange(t)[:, None]
    old_slots = (jnp.arange(groups)[:, None] * d_far + jnp.arange(t)[None, :]).reshape(1, -1)
    old_delta = w + tq - old_slots
    tail_delta = w + tq - (w - tail + jnp.arange(tail))[None, :]
    new_delta = tq - jnp.arange(ATT_BLOCK)[None, :]
    out_t, out_n = [], []
    for window, d in DILATED_PATTERNS:
        nk = window // d
        sb = _strided_bias(rel_bias, d, nk)

        def table(delta, extra_ok):
            ok = (delta >= 0) & (delta % d == 0) & (delta <= nk * d) & extra_ok
            return _bias_lookup(sb, jnp.clip(delta // d, 0, nk), ok)

        out_t.append(table(tail_delta, True))
        out_n.append(table(new_delta, jnp.arange(ATT_BLOCK)[None, :] < t))
        if d == d_far:
            out_o = table(old_delta, True)
        else:
            assert window <= tail
    return out_o.astype(F32), jnp.stack(out_t).astype(F32), jnp.stack(out_n).astype(F32)


def _ssm_tables(lam_re, lam_im, log_step, b_re, b_im, c_re, c_im):
    g, p, cg = SSM_GROUPS, SSM_STATE, SSM_GROUP
    nq = SSM_QUARTERS
    gq = g // nq
    step = jnp.exp(log_step)[:, None]
    mag = jnp.exp(lam_re * step)
    a_re = mag * jnp.cos(lam_im * step)
    a_im = mag * jnp.sin(lam_im * step)
    den = lam_re * lam_re + lam_im * lam_im
    coef_re = ((a_re - 1.0) * lam_re + a_im * lam_im) / den
    coef_im = (a_im * lam_re - (a_re - 1.0) * lam_im) / den
    bb_re = coef_re[..., None] * b_re - coef_im[..., None] * b_im
    bb_im = coef_re[..., None] * b_im + coef_im[..., None] * b_re
    same_group = np.arange(gq * cg)[:, None] // cg == np.arange(gq * p)[None, :] // p

    def in_mat(bb):
        t = jnp.transpose(bb, (0, 2, 1)).reshape(nq, gq * cg, 1, p)
        t = jnp.broadcast_to(t, (nq, gq * cg, gq, p)).reshape(nq, gq * cg, gq * p)
        return jnp.where(same_group, t, 0.0)

    def out_mat(c):
        t = jnp.transpose(c, (0, 2, 1)).reshape(nq, gq * p, 1, cg)
        t = jnp.broadcast_to(t, (nq, gq * p, gq, cg)).reshape(nq, gq * p, gq * cg)
        return jnp.where(same_group.T, t, 0.0)

    bq = jnp.concatenate([in_mat(bb_re), in_mat(bb_im)], axis=2)
    cq = jnp.concatenate([out_mat(c_re), -out_mat(c_im)], axis=1)
    aq = jnp.concatenate([a_re.reshape(nq, gq * p), a_im.reshape(nq, gq * p)], axis=1)
    a8 = jnp.concatenate([aq, aq], axis=0)
    return bq, cq, a8


def _rows(v):
    return v.reshape(v.shape[0], 1, -1)


def kernel(x_prompt, x_sample, cache_attn_k, cache_attn_v, state_ssm_re, state_ssm_im, rel_bias, w_in, w_out, g_pre_mix, g_post_mix, g_mix_out, ssm_lam_re, ssm_lam_im, ssm_log_step, ssm_b_re, ssm_b_im, ssm_c_re, ssm_c_im, ssm_d, ssm_w_glu, ssm_b_glu, sgu_g, sgu_w, sgu_b, g_pre_ffn, g_post_ffn, w_gate, w_up, w_down):
    nb, seq, dm = x_prompt.shape
    ns, ts, _ = x_sample.shape
    wbuf = cache_attn_k.shape[2]
    n_keep = min(DILATED_PATTERNS[-1][0], seq)
    h, e = ATT_HEADS, ATT_HEAD_DIM

    bias_p = _prompt_bias_tables(rel_bias)
    bias_o, bias_t, bias_n = _sample_bias_tables(rel_bias, wbuf, ts)
    k_buf = lax.empty((DEPTH, nb, n_keep * h, e), F32)
    v_buf = lax.empty((DEPTH, nb, n_keep * h, e), F32)
    st_re = state_ssm_re.reshape(DEPTH, ns, SSM_GROUPS * SSM_STATE)
    st_im = state_ssm_im.reshape(DEPTH, ns, SSM_GROUPS * SSM_STATE)

    xp = x_prompt.reshape(nb * seq, dm)
    xs = x_sample.reshape(ns * ts, dm)
    outs = {k: [] for k in ('rp', 'ip', 'ks', 'vs', 'rs', 'is', 'us')}

    big = (w_in, w_out, w_gate, w_up, w_down)
    big16 = tuple(w[0].astype(BF16) for w in big)
    w_glu16 = ssm_w_glu.astype(BF16)
    bq, cq, a8 = jax.vmap(_ssm_tables)(ssm_lam_re, ssm_lam_im, ssm_log_step,
                                       ssm_b_re, ssm_b_im, ssm_c_re, ssm_c_im)
    bq16, cq16 = bq.astype(BF16), cq.astype(BF16)
    sgu_b_rows = jnp.broadcast_to(sgu_b[..., None], (DEPTH, SGU_HEADS, SGU_CHUNK, SGU_HEAD_DIM))
    g_pre, g_post, g_mix = _rows(g_pre_mix), _rows(g_post_mix), _rows(g_mix_out)
    g_pre_f, g_post_f = _rows(g_pre_ffn), _rows(g_post_ffn)
    d_skip, b_glu, g_sgu = _rows(ssm_d), _rows(ssm_b_glu), _rows(sgu_g)

    for l in range(DEPTH):
        w_in16, w_out16, w_gate16, w_up16, w_down16 = big16
        cast = [(w, l + 1, w is not w_down) for w in big] if l + 1 < DEPTH else []

        z, k_buf, v_buf = in_proj(xp, g_pre, w_in16, l, tm=1024, tn=768,
                                  kv=(seq, n_keep, k_buf, v_buf))
        z = z.reshape(nb, seq, IN_COLS)
        o_att = attn_prompt(z, bias_p)
        o_ssm, p_re, p_im = ssm_prompt(z, l, bq16, a8, cq16, d_skip, w_glu16, b_glu)
        o_sgu = sgu_prompt(z, l, g_sgu, sgu_w, sgu_b_rows)
        xp = out_proj(o_att.reshape(nb * seq, ATT_WIDTH), o_ssm.reshape(nb * seq, SSM_WIDTH),
                      o_sgu.reshape(nb * seq, SGU_WIDTH), xp, w_out16, l, g_mix, g_post, tm=512)
        xp, next16 = ffn(xp, g_pre_f, w_gate16, w_up16, w_down16, l, g_post_f, tm=1024, tf=256,
                         cast=cast)
        outs['rp'].append(p_re.reshape(nb, SSM_GROUPS, SSM_STATE))
        outs['ip'].append(p_im.reshape(nb, SSM_GROUPS, SSM_STATE))

        zs = in_proj(xs, g_pre, w_in16, l, tm=ns * ts, tn=2304).reshape(ns, ts, IN_COLS)
        s_att = attn_sample(zs, cache_attn_k, cache_attn_v, l, bias_o, bias_t, bias_n)
        s_ssm, s_re, s_im = ssm_sample(zs, l, st_re, st_im, bq, a8, cq, d_skip, w_glu16, b_glu)
        s_sgu, s_gv = sgu_sample(zs, l, g_sgu, sgu_w, sgu_b_rows)
        xs = out_proj(s_att.reshape(ns * ts, ATT_WIDTH), s_ssm.reshape(ns * ts, SSM_WIDTH),
                      s_sgu.reshape(ns * ts, SGU_WIDTH), xs, w_out16, l, g_mix, g_post, tm=ns * ts)
        xs, _ = ffn(xs, g_pre_f, w_gate16, w_up16, w_down16, l, g_post_f, tm=ns * ts, tf=1408)
        big16 = tuple(next16) or big16
        outs['ks'].append(zs[:, :, ATT_WIDTH:2 * ATT_WIDTH].reshape(ns, ts, h, e))
        outs['vs'].append(zs[:, :, 2 * ATT_WIDTH:3 * ATT_WIDTH].reshape(ns, ts, h, e))
        outs['rs'].append(s_re.reshape(ns, SSM_GROUPS, SSM_STATE))
        outs['is'].append(s_im.reshape(ns, SSM_GROUPS, SSM_STATE))
        outs['us'].append(s_gv)

    st = lambda key: jnp.stack(outs[key])
    kv_shape = (DEPTH, nb, n_keep, h, e)
    return (xp.reshape(nb, seq, dm), xs.reshape(ns, ts, dm),
            k_buf.reshape(kv_shape), v_buf.reshape(kv_shape), st('rp'), st('ip'),
            st('ks'), st('vs'), st('rs'), st('is'), st('us'))
```

```python
import functools
import math

import jax
import jax.numpy as jnp
import numpy as np
from jax import lax
from jax.experimental import pallas as pl
from jax.experimental.pallas import tpu as pltpu

F32 = jnp.float32
BF16 = jnp.bfloat16

D_MODEL = 2048
DEPTH = 4
ATT_HEADS = 8
ATT_HEAD_DIM = 128
ATT_WIDTH = ATT_HEADS * ATT_HEAD_DIM
DILATED_PATTERNS = ((128, 1), (512, 4), (2048, 16))
ATT_BLOCK = 128
ATT_GROUP = (8, 8, 4)
ATT_SCALE = ATT_HEAD_DIM ** -0.5
N_REL_BUCKETS = 32
REL_MAX_DIST = 2048
SSM_WIDTH = 512
SSM_GROUP = 16
SSM_GROUPS = SSM_WIDTH // SSM_GROUP
SSM_STATE = 64
SSM_QUARTERS = 4
SSM_QW = SSM_GROUPS * SSM_STATE // SSM_QUARTERS
SSM_QC = SSM_WIDTH // SSM_QUARTERS
SGU_WIDTH = 512
SGU_HEADS = 4
SGU_HEAD_DIM = SGU_WIDTH // SGU_HEADS
SGU_CHUNK = 128
IN_COLS = 3 * ATT_WIDTH + SSM_WIDTH + 2 * SGU_WIDTH
FFN_HIDDEN = 5632
EPS = 1e-6
MASKED = -float("inf")
LANES = 128
CAST_BLOCK = 256
FFN_EDGE_ROWS = 256

VMEM_LIMIT_BYTES = 56 * 1024 * 1024


def _params(*sem):
    return pltpu.CompilerParams(dimension_semantics=sem, vmem_limit_bytes=VMEM_LIMIT_BYTES)


def _layer_spec(arr, layer):
    zeros = (0,) * (arr.ndim - 1)
    return pl.BlockSpec((None,) + arr.shape[1:], lambda *_: (layer,) + zeros)


def _rms(x, g):
    return x * lax.rsqrt(jnp.mean(x * x, axis=-1, keepdims=True) + EPS) * g


def _in_proj_kernel(x_ref, g_ref, w_ref, *rest, keep_from, tiles_per_seq):
    h_ref = rest[-1]
    outs = rest[-4:-1] if len(rest) > 2 else rest[:1]
    z_ref = outs[0]
    i = pl.program_id(0)
    j = pl.program_id(1)

    tm, tn = z_ref.shape
    chunk = min(tm, FFN_EDGE_ROWS)

    @pl.when(j == 0)
    def _():
        for r in range(0, tm, chunk):
            rows = pl.ds(r, chunk)
            h = _rms(x_ref[rows, :], g_ref[...]).astype(BF16)
            h_ref[rows, :] = h
            z_ref[rows, :] = jnp.dot(h, w_ref[...], preferred_element_type=F32)

    @pl.when(j > 0)
    def _():
        _in_proj_tile(i, j, h_ref, w_ref, outs, keep_from, tiles_per_seq)


def _in_proj_tile(i, j, h_ref, w_ref, outs, keep_from, tiles_per_seq):
    z_ref = outs[0]
    res = jnp.dot(h_ref[...], w_ref[...], preferred_element_type=F32)
    z_ref[...] = res

    if len(outs) > 1:
        tm, tn = res.shape
        keep = (i % tiles_per_seq) >= keep_from
        by_tile = {}
        for buf_ref, col0 in ((outs[1], ATT_WIDTH), (outs[2], 2 * ATT_WIDTH)):
            for head in range(ATT_HEADS):
                col = col0 + head * ATT_HEAD_DIM
                by_tile.setdefault(col // tn, []).append((buf_ref, head, col % tn))
        for tile, heads in by_tile.items():
            @pl.when(keep & (j == tile))
            def _(heads=heads):
                for buf_ref, head, off in heads:
                    rows = pl.ds(head, tm, stride=ATT_HEADS)
                    buf_ref[0, 0, rows, :] = res[:, off:off + ATT_HEAD_DIM]


def in_proj(x, g, w, layer, tm, tn, kv=None):
    m, k = x.shape
    n = w.shape[1]
    assert tn % ATT_HEAD_DIM == 0
    in_specs = [pl.BlockSpec((tm, k), lambda i, j: (i, 0)),
                _layer_spec(g, layer),
                pl.BlockSpec((k, tn), lambda i, j: (0, j))]
    z_spec = pl.BlockSpec((tm, tn), lambda i, j: (i, j))
    z_shape = jax.ShapeDtypeStruct((m, n), F32)
    scratch = [pltpu.VMEM((tm, k), BF16)]
    if kv is None:
        kern = functools.partial(_in_proj_kernel, keep_from=0, tiles_per_seq=1)
        return pl.pallas_call(
            kern, grid=(m // tm, n // tn), in_specs=in_specs, out_specs=z_spec, out_shape=z_shape,
            scratch_shapes=scratch, compiler_params=_params("parallel", "arbitrary"),
            name="in_proj",
        )(x, g, w)
    seq, n_keep, k_buf, v_buf = kv
    tiles_per_seq = seq // tm
    keep_from = (seq - n_keep) // tm
    assert seq % tm == 0 and (seq - n_keep) % tm == 0
    assert tn <= ATT_WIDTH

    def buf_map(i, j):
        return (layer, i // tiles_per_seq, jnp.maximum(i % tiles_per_seq - keep_from, 0), 0)

    buf_spec = pl.BlockSpec((1, 1, tm * ATT_HEADS, ATT_HEAD_DIM), buf_map)
    any_spec = pl.BlockSpec(memory_space=pl.ANY)
    kern = functools.partial(_in_proj_kernel, keep_from=keep_from, tiles_per_seq=tiles_per_seq)
    return pl.pallas_call(
        kern, grid=(m // tm, n // tn),
        in_specs=in_specs + [any_spec, any_spec],
        out_specs=[z_spec, buf_spec, buf_spec],
        out_shape=[z_shape, jax.ShapeDtypeStruct(k_buf.shape, F32),
                   jax.ShapeDtypeStruct(v_buf.shape, F32)],
        input_output_aliases={3: 1, 4: 2},
        scratch_shapes=scratch, compiler_params=_params("arbitrary", "arbitrary"),
        name="in_proj_kv",
    )(x, g, w, k_buf, v_buf)


def _attn_prompt_kernel(q_ref, k_ref, v_ref, b_ref, o_ref, acc_ref, m_ref, l_ref, *, seq):
    qb = ATT_BLOCK
    n_pat = len(DILATED_PATTERNS)

    for p, (_, d) in enumerate(DILATED_PATTERNS):
        shift = int(math.log2(d))
        whole = seq // d == 2 * qb
        blocks_per_trip = ATT_GROUP[p]
        units = blocks_per_trip // 2 if whole else blocks_per_trip

        def group(g, carry, p=p, d=d, shift=shift, whole=whole, units=units):
            rows, biases = [], []
            for u in range(units):
                i = g * units + u
                if whole:
                    rows.append((pl.ds(i, 2 * qb, stride=d), pl.ds(i, 2 * qb, stride=d)))
                    biases.append(None)
                    continue
                r = i & (d - 1)
                b = i >> shift
                q_start = r + b * (qb * d)
                k_start = r + jnp.maximum(b - 1, 0) * (qb * d)
                if d == 1:
                    rows.append((pl.ds(pl.multiple_of(q_start, qb), qb),
                                 pl.ds(pl.multiple_of(k_start, qb), 2 * qb)))
                else:
                    rows.append((pl.ds(q_start, qb, stride=d), pl.ds(k_start, 2 * qb, stride=d)))
                biases.append(jnp.where(b == 0, 1, 0))
            nt = (((1,), (1,)), ((), ()))
            scores = [lax.dot_general(q_ref[0, qr, :].astype(BF16), k_ref[0, kr, :].astype(BF16), nt,
                                      preferred_element_type=F32) for qr, kr in rows]
            probs = []
            for (qr, _), first, s in zip(rows, biases, scores):
                if whole:
                    bias = jnp.concatenate([b_ref[p, 1, 0], b_ref[p, 0, 0]], axis=0)
                else:
                    bias = b_ref[p, first, 0]
                s = s * ATT_SCALE + bias
                m_blk = jnp.max(s, axis=1, keepdims=True)
                pe = jnp.exp(s - m_blk)
                wide = (s.shape[0], ATT_HEAD_DIM)
                m_ref[p, qr, :] = jnp.broadcast_to(m_blk, wide)
                l_ref[p, qr, :] = jnp.broadcast_to(jnp.sum(pe, axis=1, keepdims=True), wide)
                probs.append(pe.astype(BF16))
            for (qr, kr), pe in zip(rows, probs):
                acc_ref[p, qr, :] = jnp.dot(pe, v_ref[0, kr, :].astype(BF16),
                                            preferred_element_type=F32)
            return carry

        lax.fori_loop(0, seq // qb // blocks_per_trip, group, 0)

    rows_per = 2 * qb

    def merge(c, carry):
        rows = pl.ds(pl.multiple_of(c * rows_per, rows_per), rows_per)
        ms = [m_ref[p, rows, :] for p in range(n_pat)]
        m = functools.reduce(jnp.maximum, ms)
        num = den = None
        for p in range(n_pat):
            w = jnp.exp(ms[p] - m)
            num = w * acc_ref[p, rows, :] if num is None else num + w * acc_ref[p, rows, :]
            den = w * l_ref[p, rows, :] if den is None else den + w * l_ref[p, rows, :]
        o_ref[0, rows, :] = num / den
        return carry

    lax.fori_loop(0, seq // rows_per, merge, 0)


def attn_prompt(z3, bias):
    n, seq, _ = z3.shape
    h = ATT_HEADS
    kern = functools.partial(_attn_prompt_kernel, seq=seq)
    return pl.pallas_call(
        kern,
        grid=(n, h),
        in_specs=[pl.BlockSpec((1, seq, ATT_HEAD_DIM), lambda i, j: (i, 0, j)),
                  pl.BlockSpec((1, seq, ATT_HEAD_DIM), lambda i, j: (i, 0, h + j)),
                  pl.BlockSpec((1, seq, ATT_HEAD_DIM), lambda i, j: (i, 0, 2 * h + j)),
                  pl.BlockSpec((3, 2, 1, ATT_BLOCK, 2 * ATT_BLOCK), lambda i, j: (0, 0, j, 0, 0))],
        out_specs=pl.BlockSpec((1, seq, ATT_HEAD_DIM), lambda i, j: (i, 0, j)),
        out_shape=jax.ShapeDtypeStruct((n, seq, ATT_WIDTH), F32),
        scratch_shapes=[pltpu.VMEM((len(DILATED_PATTERNS), seq, ATT_HEAD_DIM), F32)] * 3,
        compiler_params=_params("parallel", "parallel"),
        name="attn_prompt",
    )(z3, z3, z3, bias)


def _attn_sample_kernel(q_ref, kn_ref, vn_ref, ko_ref, vo_ref, kt_ref, vt_ref,
                        bo_ref, bt_ref, bn_ref, o_ref):
    t = q_ref.shape[1]
    e = ATT_HEAD_DIM
    nh = ATT_HEADS
    n_pat = len(DILATED_PATTERNS)
    groups, kept_rows, _ = ko_ref.shape
    kept = kept_rows // nh
    tail = kt_ref.shape[0] // nh
    pad = jnp.zeros((ATT_BLOCK - t, e), F32)
    nt = (((1,), (1,)), ((), ()))
    heads = range(nh)
    cols = [slice(h * e, (h + 1) * e) for h in heads]

    def old_rows(ref, h):
        return ref[:, pl.ds(h, kept, stride=nh), :].reshape(groups * kept, e).astype(BF16)

    def tail_rows(ref, h):
        return ref[pl.ds(h, tail, stride=nh), :].astype(BF16)

    so, st, sn = [], [], []
    for h in heads:
        q = q_ref[0, :, cols[h]].astype(BF16)
        kn = jnp.concatenate([kn_ref[0, :, cols[h]], pad], axis=0).astype(BF16)
        so.append(lax.dot_general(q, old_rows(ko_ref, h), nt, preferred_element_type=F32) * ATT_SCALE)
        st.append(lax.dot_general(q, tail_rows(kt_ref, h), nt, preferred_element_type=F32) * ATT_SCALE)
        sn.append(lax.dot_general(q, kn, nt, preferred_element_type=F32) * ATT_SCALE)
    po, pt, pn, ls = [], [], [], []
    for h in heads:
        s_old = so[h] + bo_ref[h]
        sts = [st[h] + bt_ref[p, h] for p in range(n_pat)]
        sns = [sn[h] + bn_ref[p, h] for p in range(n_pat)]
        m = None
        for s in [s_old] + sts + sns:
            mx = jnp.max(s, axis=1, keepdims=True)
            m = mx if m is None else jnp.maximum(m, mx)
        poh = jnp.exp(s_old - m)
        pth = sum(jnp.exp(s - m) for s in sts)
        pnh = sum(jnp.exp(s - m) for s in sns)
        ls.append(jnp.sum(poh, axis=1, keepdims=True) + jnp.sum(pth, axis=1, keepdims=True)
                  + jnp.sum(pnh, axis=1, keepdims=True))
        po.append(poh.astype(BF16))
        pt.append(pth.astype(BF16))
        pn.append(pnh.astype(BF16))
    for h in heads:
        vn = jnp.concatenate([vn_ref[0, :, cols[h]], pad], axis=0).astype(BF16)
        acc = (jnp.dot(po[h], old_rows(vo_ref, h), preferred_element_type=F32)
               + jnp.dot(pt[h], tail_rows(vt_ref, h), preferred_element_type=F32)
               + jnp.dot(pn[h], vn, preferred_element_type=F32))
        o_ref[0, :, cols[h]] = acc / ls[h]


def _sample_cache_split(w, t):
    d_far = DILATED_PATTERNS[-1][1]
    tail = max(window for window, _ in DILATED_PATTERNS[:-1])
    assert t <= d_far and w % d_far == 0 and tail % d_far == 0 and tail <= w
    return d_far, tail, (w - tail) // d_far


def attn_sample(zs3, cache_k, cache_v, layer, bias_o, bias_t, bias_n):
    n, t, _ = zs3.shape
    h = ATT_HEADS
    e = ATT_HEAD_DIM
    depth, _, w = cache_k.shape[:3]
    d_far, tail, groups = _sample_cache_split(w, t)
    by_group = lambda c: c.reshape(depth, n, w // d_far, d_far * h, e)
    by_tail = lambda c: c.reshape(depth, n, w // tail, tail * h, e)
    old_spec = pl.BlockSpec((None, None, groups, t * h, e), lambda i: (layer, i, 0, 0, 0))
    tail_spec = pl.BlockSpec((None, None, None, tail * h, e), lambda i: (layer, i, w // tail - 1, 0, 0))
    return pl.pallas_call(
        _attn_sample_kernel,
        grid=(n,),
        in_specs=[pl.BlockSpec((1, t, ATT_WIDTH), lambda i: (i, 0, 0)),
                  pl.BlockSpec((1, t, ATT_WIDTH), lambda i: (i, 0, 1)),
                  pl.BlockSpec((1, t, ATT_WIDTH), lambda i: (i, 0, 2)),
                  old_spec, old_spec, tail_spec, tail_spec,
                  pl.BlockSpec(bias_o.shape, lambda i: (0, 0, 0)),
                  pl.BlockSpec(bias_t.shape, lambda i: (0, 0, 0, 0)),
                  pl.BlockSpec(bias_n.shape, lambda i: (0, 0, 0, 0))],
        out_specs=pl.BlockSpec((1, t, ATT_WIDTH), lambda i: (i, 0, 0)),
        out_shape=jax.ShapeDtypeStruct((n, t, ATT_WIDTH), F32),
        compiler_params=_params("parallel"),
        name="attn_sample",
    )(zs3, zs3, zs3, by_group(cache_k), by_group(cache_v), by_tail(cache_k), by_tail(cache_v),
      bias_o, bias_t, bias_n)


def _ssm_prompt_kernel(u_ref, bq_ref, a_ref, cq_ref, d_ref, wg_ref, bg_ref,
                       o_ref, hre_ref, him_ref, x_ref, h_ref, *, tb, pitch):
    step = pl.program_id(0)
    nq = SSM_QUARTERS
    qw = SSM_QW
    qc = SSM_QC

    @pl.when(step == 0)
    def _():
        h_ref[...] = jnp.zeros_like(h_ref)

    nl = 2 * qw // LANES
    hl = nl // 2

    for b in range(2):
        for k in range(nq):
            u = u_ref[b, :, k * qc:(k + 1) * qc].astype(BF16)
            x = jnp.dot(u, bq_ref[k], preferred_element_type=F32)
            for c in range(nl):
                x_ref[c, pl.ds((b * nq + k) * pitch, tb), :] = x[:, c * LANES:(c + 1) * LANES]

    a = [a_ref[:, c * LANES:(c + 1) * LANES] for c in range(nl)]

    def scan(t, h):
        rows = pl.ds(t, 2 * nq, stride=pitch)
        new = [None] * nl
        for c in range(hl):
            a_re, a_im, h_re, h_im = a[c], a[hl + c], h[c], h[hl + c]
            new[c] = a_re * h_re - a_im * h_im + x_ref[c, rows, :]
            new[hl + c] = a_re * h_im + a_im * h_re + x_ref[hl + c, rows, :]
        for c in range(nl):
            x_ref[c, rows, :] = new[c]
        return tuple(new)

    h0 = tuple(h_ref[:, c * LANES:(c + 1) * LANES] for c in range(nl))
    h_new = jnp.concatenate(lax.fori_loop(0, tb, scan, h0, unroll=8), axis=1)
    h_ref[...] = h_new

    for b in range(2):
        ys = []
        for k in range(nq):
            rows = pl.ds((b * nq + k) * pitch, tb)
            hs = jnp.concatenate([x_ref[c, rows, :] for c in range(nl)], axis=1).astype(BF16)
            ys.append(jnp.dot(hs, cq_ref[k], preferred_element_type=F32))
        y = jnp.concatenate(ys, axis=1) + d_ref[...] * u_ref[b]
        gate = jnp.dot(y.astype(BF16), wg_ref[...], preferred_element_type=F32) + bg_ref[...]
        o_ref[b] = y * jax.nn.sigmoid(gate)

    @pl.when(step == pl.num_programs(0) - 1)
    def _():
        hre_ref[...] = h_new[:, :qw]
        him_ref[...] = h_new[:, qw:]


def ssm_prompt(z3, layer, bq, a8, cq, d_skip, w_glu, b_glu, tb=512):
    n, seq, _ = z3.shape
    assert n == 2
    pitch = tb + 4
    ucol = 3 * ATT_WIDTH // SSM_WIDTH
    kern = functools.partial(_ssm_prompt_kernel, tb=tb, pitch=pitch)
    full = lambda *shape: pl.BlockSpec(shape, lambda s: (0,) * len(shape))
    return pl.pallas_call(
        kern,
        grid=(seq // tb,),
        in_specs=[pl.BlockSpec((n, tb, SSM_WIDTH), lambda s: (0, s, ucol))]
        + [_layer_spec(p, layer) for p in (bq, a8, cq, d_skip, w_glu, b_glu)],
        out_specs=[pl.BlockSpec((n, tb, SSM_WIDTH), lambda s: (0, s, 0)),
                   full(2 * SSM_QUARTERS, SSM_QW), full(2 * SSM_QUARTERS, SSM_QW)],
        out_shape=[jax.ShapeDtypeStruct((n, seq, SSM_WIDTH), F32),
                   jax.ShapeDtypeStruct((2 * SSM_QUARTERS, SSM_QW), F32),
                   jax.ShapeDtypeStruct((2 * SSM_QUARTERS, SSM_QW), F32)],
        scratch_shapes=[pltpu.VMEM((2 * SSM_QW // LANES, 2 * SSM_QUARTERS * pitch, LANES), F32),
                        pltpu.VMEM((2 * SSM_QUARTERS, 2 * SSM_QW), F32)],
        compiler_params=_params("arbitrary"),
        name="ssm_prompt",
    )(z3, bq, a8, cq, d_skip, w_glu, b_glu)


def _ssm_sample_kernel(u_ref, hre_ref, him_ref, bq_ref, a_ref, cq_ref, d_ref, wg_ref, bg_ref,
                       o_ref, ore_ref, oim_ref, x_ref):
    n, t, _ = u_ref.shape
    nq = SSM_QUARTERS
    qw = SSM_QW
    qc = SSM_QC
    nl = 2 * qw // LANES
    hl = nl // 2
    u_all = u_ref[...].reshape(n * t, SSM_WIDTH)
    ys = []
    for k in range(nq):
        u = u_all[:, k * qc:(k + 1) * qc]
        x = jnp.dot(u, bq_ref[k], preferred_element_type=F32, precision=lax.Precision.HIGHEST)
        for c in range(nl):
            x_ref[c] = x[:, c * LANES:(c + 1) * LANES]
        h = ([hre_ref[:, k * qw + c * LANES:k * qw + (c + 1) * LANES] for c in range(hl)]
             + [him_ref[:, k * qw + c * LANES:k * qw + (c + 1) * LANES] for c in range(hl)])
        for s in range(t):
            rows = pl.ds(s, n, stride=t)
            new = [None] * nl
            for c in range(hl):
                a_re = a_ref[k:k + 1, c * LANES:(c + 1) * LANES]
                a_im = a_ref[k:k + 1, (hl + c) * LANES:(hl + c + 1) * LANES]
                new[c] = a_re * h[c] - a_im * h[hl + c] + x_ref[c, rows, :]
                new[hl + c] = a_re * h[hl + c] + a_im * h[c] + x_ref[hl + c, rows, :]
            for c in range(nl):
                x_ref[c, rows, :] = new[c]
            h = new
        for c in range(hl):
            ore_ref[:, k * qw + c * LANES:k * qw + (c + 1) * LANES] = h[c]
            oim_ref[:, k * qw + c * LANES:k * qw + (c + 1) * LANES] = h[hl + c]
        hs = jnp.concatenate([x_ref[c] for c in range(nl)], axis=1).astype(BF16)
        ys.append(jnp.dot(hs, cq_ref[k].astype(BF16), preferred_element_type=F32))
    y = jnp.concatenate(ys, axis=1) + d_ref[...] * u_all
    gate = jnp.dot(y.astype(BF16), wg_ref[...], preferred_element_type=F32) + bg_ref[...]
    o_ref[...] = (y * jax.nn.sigmoid(gate)).reshape(n, t, SSM_WIDTH)


def ssm_sample(zs3, layer, h_re, h_im, bq32, a8, cq32, d_skip, w_glu, b_glu):
    n, t, _ = zs3.shape
    ucol = 3 * ATT_WIDTH // SSM_WIDTH
    ns = SSM_GROUPS * SSM_STATE
    full = lambda *shape: pl.BlockSpec(shape, lambda s: (0,) * len(shape))
    return pl.pallas_call(
        _ssm_sample_kernel,
        grid=(1,),
        in_specs=[pl.BlockSpec((n, t, SSM_WIDTH), lambda s: (0, 0, ucol))]
        + [_layer_spec(p, layer) for p in (h_re, h_im, bq32, a8, cq32, d_skip, w_glu, b_glu)],
        out_specs=[full(n, t, SSM_WIDTH), full(n, ns), full(n, ns)],
        out_shape=[jax.ShapeDtypeStruct((n, t, SSM_WIDTH), F32),
                   jax.ShapeDtypeStruct((n, ns), F32),
                   jax.ShapeDtypeStruct((n, ns), F32)],
        scratch_shapes=[pltpu.VMEM((2 * SSM_QW // LANES, n * t, LANES), F32)],
        compiler_params=_params("arbitrary"),
        name="ssm_sample",
    )(zs3, h_re, h_im, bq32, a8, cq32, d_skip, w_glu, b_glu)


def _sgu_norm_v(v, g):
    gv = jax.nn.gelu(v)
    vc = gv - jnp.mean(gv, axis=-1, keepdims=True)
    return vc * lax.rsqrt(jnp.mean(vc * vc, axis=-1, keepdims=True) + EPS) * g


def _sgu_causal(w_ref):
    c = SGU_CHUNK
    keep = lax.broadcasted_iota(jnp.int32, (c, c), 0) >= lax.broadcasted_iota(jnp.int32, (c, c), 1)
    return [jnp.where(keep, w_ref[h], 0.0).astype(BF16) for h in range(SGU_HEADS)]


def _sgu_prompt_kernel(u_ref, v_ref, g_ref, w_ref, b_ref, o_ref):
    c = SGU_CHUNK
    e = SGU_HEAD_DIM
    rows = u_ref.shape[1]
    w = _sgu_causal(w_ref)
    gu = jax.nn.gelu(u_ref[0])
    gv = _sgu_norm_v(v_ref[0], g_ref[...]).astype(BF16)
    for ci in range(rows // c):
        for h in range(SGU_HEADS):
            mixed = jnp.dot(w[h], gv[ci * c:(ci + 1) * c, h * e:(h + 1) * e],
                            preferred_element_type=F32) + b_ref[h]
            o_ref[0, ci * c:(ci + 1) * c, h * e:(h + 1) * e] = (
                gu[ci * c:(ci + 1) * c, h * e:(h + 1) * e] * mixed)


def sgu_prompt(z3, layer, g, w_sp, b_rows, rows=1024):
    n, seq, _ = z3.shape
    ucol = (3 * ATT_WIDTH + SSM_WIDTH) // SGU_WIDTH
    return pl.pallas_call(
        _sgu_prompt_kernel,
        grid=(n, seq // rows),
        in_specs=[pl.BlockSpec((1, rows, SGU_WIDTH), lambda i, j: (i, j, ucol)),
                  pl.BlockSpec((1, rows, SGU_WIDTH), lambda i, j: (i, j, ucol + 1)),
                  _layer_spec(g, layer), _layer_spec(w_sp, layer), _layer_spec(b_rows, layer)],
        out_specs=pl.BlockSpec((1, rows, SGU_WIDTH), lambda i, j: (i, j, 0)),
        out_shape=jax.ShapeDtypeStruct((n, seq, SGU_WIDTH), F32),
        compiler_params=_params("parallel", "parallel"),
        name="sgu_prompt",
    )(z3, z3, g, w_sp, b_rows)


def _sgu_sample_kernel(u_ref, v_ref, g_ref, w_ref, b_ref, o_ref, gv_ref):
    c = SGU_CHUNK
    e = SGU_HEAD_DIM
    n, t, _ = u_ref.shape
    w = _sgu_causal(w_ref)
    for i in range(n):
        gu = jax.nn.gelu(u_ref[i])
        gv = _sgu_norm_v(v_ref[i], g_ref[...])
        gv_ref[i] = gv
        gvp = jnp.concatenate([gv, jnp.zeros((c - t, SGU_WIDTH), F32)], axis=0).astype(BF16)
        for h in range(SGU_HEADS):
            mixed = jnp.dot(w[h], gvp[:, h * e:(h + 1) * e], preferred_element_type=F32) + b_ref[h]
            o_ref[i, :, h * e:(h + 1) * e] = gu[:, h * e:(h + 1) * e] * mixed[:t]


def sgu_sample(zs3, layer, g, w_sp, b_rows):
    n, t, _ = zs3.shape
    ucol = (3 * ATT_WIDTH + SSM_WIDTH) // SGU_WIDTH
    return pl.pallas_call(
        _sgu_sample_kernel,
        grid=(1,),
        in_specs=[pl.BlockSpec((n, t, SGU_WIDTH), lambda i: (0, 0, ucol)),
                  pl.BlockSpec((n, t, SGU_WIDTH), lambda i: (0, 0, ucol + 1)),
                  _layer_spec(g, layer), _layer_spec(w_sp, layer), _layer_spec(b_rows, layer)],
        out_specs=[pl.BlockSpec((n, t, SGU_WIDTH), lambda i: (0, 0, 0)),
                   pl.BlockSpec((n, t, SGU_WIDTH), lambda i: (0, 0, 0))],
        out_shape=[jax.ShapeDtypeStruct((n, t, SGU_WIDTH), F32),
                   jax.ShapeDtypeStruct((n, t, SGU_WIDTH), F32)],
        compiler_params=_params("arbitrary"),
        name="sgu_sample",
    )(zs3, zs3, g, w_sp, b_rows)


def _out_proj_kernel(att_ref, ssm_ref, sgu_ref, x_ref, w_ref, gm_ref, gp_ref, o_ref):
    a0 = ATT_WIDTH
    a1 = ATT_WIDTH + SSM_WIDTH
    a = _rms(att_ref[...], gm_ref[:, :a0]).astype(BF16)
    s = _rms(ssm_ref[...], gm_ref[:, a0:a1]).astype(BF16)
    c = _rms(sgu_ref[...], gm_ref[:, a1:]).astype(BF16)
    mixed = jnp.concatenate([a, s, c], axis=1)
    y = jnp.dot(mixed, w_ref[...], preferred_element_type=F32)
    o_ref[...] = x_ref[...] + _rms(y, gp_ref[...])


def out_proj(att, ssm, sgu, x, w, layer, gm, gp, tm):
    m, dm = x.shape
    row = lambda width: pl.BlockSpec((tm, width), lambda i: (i, 0))
    return pl.pallas_call(
        _out_proj_kernel,
        grid=(m // tm,),
        in_specs=[row(ATT_WIDTH), row(SSM_WIDTH), row(SGU_WIDTH), row(dm),
                  pl.BlockSpec(w.shape, lambda i: (0, 0)), _layer_spec(gm, layer), _layer_spec(gp, layer)],
        out_specs=row(dm),
        out_shape=jax.ShapeDtypeStruct((m, dm), F32),
        compiler_params=_params("parallel"),
        name="out_proj",
    )(att, ssm, sgu, x, w, gm, gp)


def _ffn_kernel(x_ref, gpre_ref, wg_ref, wu_ref, wd_ref, gpost_ref, *rest, n_cast, has_side):
    n_side = int(has_side)
    xs_ref = rest[0] if has_side else None
    w32_refs = rest[n_side:n_side + n_cast]
    o_ref = rest[n_side + n_cast]
    os_ref = rest[n_side + n_cast + 1] if has_side else None
    w16_refs = rest[2 * n_side + n_cast + 1:2 * n_side + 2 * n_cast + 1]
    h_ref = rest[2 * n_side + 2 * n_cast + 1]
    hs_ref, accs_ref = rest[-2:] if has_side else (None, None)
    i = pl.program_id(0)
    j = pl.program_id(1)
    last = pl.num_programs(1) - 1
    tm = x_ref.shape[0]
    chunk = min(tm, FFN_EDGE_ROWS)

    def partial_out(h):
        gate = jnp.dot(h, wg_ref[...], preferred_element_type=F32)
        up = jnp.dot(h, wu_ref[...], preferred_element_type=F32)
        act = (jax.nn.silu(gate) * up).astype(BF16)
        return jnp.dot(act, wd_ref[...], preferred_element_type=F32)

    def cast_side_stream():
        for src_ref, dst_ref in zip(w32_refs, w16_refs):
            dst_ref[...] = src_ref[...].astype(BF16)

    def side_rows(phase):
        if not has_side:
            return

        @pl.when(i == 0)
        def _():
            if phase == "first":
                hs_ref[...] = _rms(xs_ref[...], gpre_ref[...]).astype(BF16)
                accs_ref[...] = partial_out(hs_ref[...])
            elif phase == "middle":
                accs_ref[...] += partial_out(hs_ref[...])
            else:
                f = accs_ref[...] + partial_out(hs_ref[...])
                os_ref[...] = xs_ref[...] + _rms(f, gpost_ref[...])

    @pl.when(j == 0)
    def _():
        for r in range(0, tm, chunk):
            rows = pl.ds(r, chunk)
            h = _rms(x_ref[rows, :], gpre_ref[...]).astype(BF16)
            h_ref[rows, :] = h
            o_ref[rows, :] = partial_out(h)
        cast_side_stream()
        side_rows("first")

    @pl.when((j > 0) & (j < last))
    def _():
        o_ref[...] += partial_out(h_ref[...])
        cast_side_stream()
        side_rows("middle")

    @pl.when(j == last)
    def _():
        for r in range(0, tm, chunk):
            rows = pl.ds(r, chunk)
            f = o_ref[rows, :] + partial_out(h_ref[rows, :])
            o_ref[rows, :] = x_ref[rows, :] + _rms(f, gpost_ref[...])
        cast_side_stream()
        side_rows("last")


def ffn(x, gpre, wg, wu, wd, layer, gpost, tm, tf, cast=(), side=None):
    m, dm = x.shape
    f = wg.shape[1]
    gi, gj = m // tm, f // tf
    assert gj >= 2

    def spread(dim, steps):
        return next(b for b in range(CAST_BLOCK, dim + 1, CAST_BLOCK) if dim % b == 0 and dim // b <= steps)

    cast_in, cast_out, cast_shape = [], [], []
    for w32, lyr, rows_follow_i in cast:
        _, r, c = w32.shape
        if rows_follow_i:
            assert r % gi == 0
            blk = (r // gi, spread(c, gj))
            imap = lambda i, j, last=c // blk[1] - 1: (i, jnp.minimum(j, last))
        else:
            assert c % gi == 0
            blk = (spread(r, gj), c // gi)
            imap = lambda i, j, last=r // blk[0] - 1: (jnp.minimum(j, last), i)
        cast_in.append(pl.BlockSpec((None,) + blk, lambda i, j, imap=imap, lyr=lyr: (lyr,) + imap(i, j)))
        cast_out.append(pl.BlockSpec(blk, imap))
        cast_shape.append(jax.ShapeDtypeStruct((r, c), BF16))
    has_side = side is not None
    side_spec = [pl.BlockSpec(side.shape, lambda i, j: (0, 0))] if has_side else []
    side_shape = [jax.ShapeDtypeStruct(side.shape, F32)] if has_side else []
    side_scratch = [pltpu.VMEM(side.shape, BF16), pltpu.VMEM(side.shape, F32)] if has_side else []
    kern = functools.partial(_ffn_kernel, n_cast=len(cast), has_side=has_side)
    res = pl.pallas_call(
        kern,
        grid=(gi, gj),
        in_specs=[pl.BlockSpec((tm, dm), lambda i, j: (i, 0)),
                  _layer_spec(gpre, layer),
                  pl.BlockSpec((dm, tf), lambda i, j: (0, j)),
                  pl.BlockSpec((dm, tf), lambda i, j: (0, j)),
                  pl.BlockSpec((tf, dm), lambda i, j: (j, 0)),
                  _layer_spec(gpost, layer)] + side_spec + cast_in,
        out_specs=[pl.BlockSpec((tm, dm), lambda i, j: (i, 0))] + side_spec + cast_out,
        out_shape=[jax.ShapeDtypeStruct((m, dm), F32)] + side_shape + cast_shape,
        scratch_shapes=[pltpu.VMEM((tm, dm), BF16)] + side_scratch,
        compiler_params=_params("arbitrary", "arbitrary"),
        name="ffn_cast" if cast else "ffn",
    )(x, gpre, wg, wu, wd, gpost, *([side] if has_side else []), *[w for w, _, _ in cast])
    n_side = int(has_side)
    return res[0], res[1 + n_side:], (res[1] if has_side else None)


def _t5_bucket(dist):
    max_exact = N_REL_BUCKETS // 2
    df = jnp.maximum(dist, 1).astype(F32)
    large = max_exact + (jnp.log(df / max_exact) / math.log(REL_MAX_DIST / max_exact)
                         * (N_REL_BUCKETS - max_exact)).astype(jnp.int32)
    large = jnp.minimum(large, N_REL_BUCKETS - 1)
    return jnp.where(dist < max_exact, dist, large)


def _strided_bias(rel_bias, dilation, n_steps):
    dist = jnp.arange(n_steps + 1, dtype=jnp.int32) * dilation
    return rel_bias[_t5_bucket(dist)].astype(F32)


def _bias_lookup(sb, steps, ok):
    onehot = (steps[..., None] == jnp.arange(sb.shape[0])).astype(F32)
    vals = jnp.einsum('...j,jh->h...', onehot, sb, precision=lax.Precision.HIGHEST)
    return jnp.where(ok[None], vals, MASKED)


def _prompt_bias_tables(rel_bias):
    qb = ATT_BLOCK
    span = jnp.arange(3 * qb - 1)
    tables = []
    for window, d in DILATED_PATTERNS:
        nk = window // d
        sb = _strided_bias(rel_bias, d, nk)

        def band(delta):
            line = _bias_lookup(sb, jnp.clip(delta, 0, nk), (delta >= 0) & (delta <= nk))
            return _toeplitz(line, qb, 2 * qb)

        tables.append(jnp.stack([band(span - (qb - 1)), band(span - (2 * qb - 1))]))
    return jnp.stack(tables).astype(F32)


def _toeplitz(line, n_rows, n_cols):
    lh = n_rows + n_cols - 1
    w = jnp.concatenate([line[..., ::-1], jnp.zeros(line.shape[:-1] + (1,), line.dtype)], axis=-1)
    tiled = jnp.tile(w, (1,) * (w.ndim - 1) + (n_rows,))[..., :n_rows * lh]
    return tiled.reshape(line.shape[:-1] + (n_rows, lh))[..., n_rows - 1:n_rows - 1 + n_cols]


def _sample_bias_tables(rel_bias, w, t):
    d_far, tail, groups = _sample_cache_split(w, t)
    tq = jnp.arange(t)[:, None]
    old_slots = (jnp.arange(groups)[:, None] * d_far + jnp.arange(t)[None, :]).reshape(1, -1)
    old_delta = w + tq - old_slots
    tail_delta = w + tq - (w - tail + jnp.arange(tail))[None, :]
    new_delta = tq - jnp.arange(ATT_BLOCK)[None, :]
    out_t, out_n = [], []
    for window, d in DILATED_PATTERNS:
        nk = window // d
        sb = _strided_bias(rel_bias, d, nk)

        def table(delta, extra_ok):
            ok = (delta >= 0) & (delta % d == 0) & (delta <= nk * d) & extra_ok
            return _bias_lookup(sb, jnp.clip(delta // d, 0, nk), ok)

        out_t.append(table(tail_delta, True))
        out_n.append(table(new_delta, jnp.arange(ATT_BLOCK)[None, :] < t))
        if d == d_far:
            out_o = table(old_delta, True)
        else:
            assert window <= tail
    return out_o.astype(F32), jnp.stack(out_t).astype(F32), jnp.stack(out_n).astype(F32)


def _ssm_tables(lam_re, lam_im, log_step, b_re, b_im, c_re, c_im):
    g, p, cg = SSM_GROUPS, SSM_STATE, SSM_GROUP
    nq = SSM_QUARTERS
    gq = g // nq
    step = jnp.exp(log_step)[:, None]
    mag = jnp.exp(lam_re * step)
    a_re = mag * jnp.cos(lam_im * step)
    a_im = mag * jnp.sin(lam_im * step)
    den = lam_re * lam_re + lam_im * lam_im
    coef_re = ((a_re - 1.0) * lam_re + a_im * lam_im) / den
    coef_im = (a_im * lam_re - (a_re - 1.0) * lam_im) / den
    bb_re = coef_re[..., None] * b_re - coef_im[..., None] * b_im
    bb_im = coef_re[..., None] * b_im + coef_im[..., None] * b_re
    same_group = np.arange(gq * cg)[:, None] // cg == np.arange(gq * p)[None, :] // p

    def in_mat(bb):
        t = jnp.transpose(bb, (0, 2, 1)).reshape(nq, gq * cg, 1, p)
        t = jnp.broadcast_to(t, (nq, gq * cg, gq, p)).reshape(nq, gq * cg, gq * p)
        return jnp.where(same_group, t, 0.0)

    def out_mat(c):
        t = jnp.transpose(c, (0, 2, 1)).reshape(nq, gq * p, 1, cg)
        t = jnp.broadcast_to(t, (nq, gq * p, gq, cg)).reshape(nq, gq * p, gq * cg)
        return jnp.where(same_group.T, t, 0.0)

    bq = jnp.concatenate([in_mat(bb_re), in_mat(bb_im)], axis=2)
    cq = jnp.concatenate([out_mat(c_re), -out_mat(c_im)], axis=1)
    aq = jnp.concatenate([a_re.reshape(nq, gq * p), a_im.reshape(nq, gq * p)], axis=1)
    a8 = jnp.concatenate([aq, aq], axis=0)
    return bq, cq, a8


def _rows(v):
    return v.reshape(v.shape[0], 1, -1)


def kernel(x_prompt, x_sample, cache_attn_k, cache_attn_v, state_ssm_re, state_ssm_im, rel_bias, w_in, w_out, g_pre_mix, g_post_mix, g_mix_out, ssm_lam_re, ssm_lam_im, ssm_log_step, ssm_b_re, ssm_b_im, ssm_c_re, ssm_c_im, ssm_d, ssm_w_glu, ssm_b_glu, sgu_g, sgu_w, sgu_b, g_pre_ffn, g_post_ffn, w_gate, w_up, w_down):
    nb, seq, dm = x_prompt.shape
    ns, ts, _ = x_sample.shape
    wbuf = cache_attn_k.shape[2]
    n_keep = min(DILATED_PATTERNS[-1][0], seq)
    h, e = ATT_HEADS, ATT_HEAD_DIM

    bias_p = _prompt_bias_tables(rel_bias)
    bias_o, bias_t, bias_n = _sample_bias_tables(rel_bias, wbuf, ts)
    k_buf = lax.empty((DEPTH, nb, n_keep * h, e), F32)
    v_buf = lax.empty((DEPTH, nb, n_keep * h, e), F32)
    st_re = state_ssm_re.reshape(DEPTH, ns, SSM_GROUPS * SSM_STATE)
    st_im = state_ssm_im.reshape(DEPTH, ns, SSM_GROUPS * SSM_STATE)

    xp = x_prompt.reshape(nb * seq, dm)
    xs = x_sample.reshape(ns * ts, dm)
    outs = {k: [] for k in ('rp', 'ip', 'ks', 'vs', 'rs', 'is', 'us')}

    big = (w_in, w_out, w_gate, w_up, w_down)
    big16 = tuple(w[0].astype(BF16) for w in big)
    w_glu16 = ssm_w_glu.astype(BF16)
    bq, cq, a8 = jax.vmap(_ssm_tables)(ssm_lam_re, ssm_lam_im, ssm_log_step,
                                       ssm_b_re, ssm_b_im, ssm_c_re, ssm_c_im)
    bq16, cq16 = bq.astype(BF16), cq.astype(BF16)
    sgu_b_rows = jnp.broadcast_to(sgu_b[..., None], (DEPTH, SGU_HEADS, SGU_CHUNK, SGU_HEAD_DIM))
    g_pre, g_post, g_mix = _rows(g_pre_mix), _rows(g_post_mix), _rows(g_mix_out)
    g_pre_f, g_post_f = _rows(g_pre_ffn), _rows(g_post_ffn)
    d_skip, b_glu, g_sgu = _rows(ssm_d), _rows(ssm_b_glu), _rows(sgu_g)

    for l in range(DEPTH):
        w_in16, w_out16, w_gate16, w_up16, w_down16 = big16
        cast = [(w, l + 1, w is not w_down) for w in big] if l + 1 < DEPTH else []

        z, k_buf, v_buf = in_proj(xp, g_pre, w_in16, l, tm=1024, tn=768,
                                  kv=(seq, n_keep, k_buf, v_buf))
        z = z.reshape(nb, seq, IN_COLS)
        o_att = attn_prompt(z, bias_p)
        o_ssm, p_re, p_im = ssm_prompt(z, l, bq16, a8, cq16, d_skip, w_glu16, b_glu)
        o_sgu = sgu_prompt(z, l, g_sgu, sgu_w, sgu_b_rows)
        xp = out_proj(o_att.reshape(nb * seq, ATT_WIDTH), o_ssm.reshape(nb * seq, SSM_WIDTH),
                      o_sgu.reshape(nb * seq, SGU_WIDTH), xp, w_out16, l, g_mix, g_post, tm=512)
        outs['rp'].append(p_re.reshape(nb, SSM_GROUPS, SSM_STATE))
        outs['ip'].append(p_im.reshape(nb, SSM_GROUPS, SSM_STATE))

        zs = in_proj(xs, g_pre, w_in16, l, tm=ns * ts, tn=2304).reshape(ns, ts, IN_COLS)
        s_att = attn_sample(zs, cache_attn_k, cache_attn_v, l, bias_o, bias_t, bias_n)
        s_ssm, s_re, s_im = ssm_sample(zs, l, st_re, st_im, bq, a8, cq, d_skip, w_glu16, b_glu)
        s_sgu, s_gv = sgu_sample(zs, l, g_sgu, sgu_w, sgu_b_rows)
        xs = out_proj(s_att.reshape(ns * ts, ATT_WIDTH), s_ssm.reshape(ns * ts, SSM_WIDTH),
                      s_sgu.reshape(ns * ts, SGU_WIDTH), xs, w_out16, l, g_mix, g_post, tm=ns * ts)
        xp, next16, xs = ffn(xp, g_pre_f, w_gate16, w_up16, w_down16, l, g_post_f, tm=1024, tf=256,
                             cast=cast, side=xs)
        big16 = tuple(next16) or big16
        outs['ks'].append(zs[:, :, ATT_WIDTH:2 * ATT_WIDTH].reshape(ns, ts, h, e))
        outs['vs'].append(zs[:, :, 2 * ATT_WIDTH:3 * ATT_WIDTH].reshape(ns, ts, h, e))
        outs['rs'].append(s_re.reshape(ns, SSM_GROUPS, SSM_STATE))
        outs['is'].append(s_im.reshape(ns, SSM_GROUPS, SSM_STATE))
        outs['us'].append(s_gv)

    st = lambda key: jnp.stack(outs[key])
    kv_shape = (DEPTH, nb, n_keep, h, e)
    return (xp.reshape(nb, seq, dm), xs.reshape(ns, ts, dm),
            k_buf.reshape(kv_shape), v_buf.reshape(kv_shape), st('rp'), st('ip'),
            st('ks'), st('vs'), st('rs'), st('is'), st('us'))
```

```python
import functools
import math

import jax
import jax.numpy as jnp
import numpy as np
from jax import lax
from jax.experimental import pallas as pl
from jax.experimental.pallas import tpu as pltpu

F32 = jnp.float32
BF16 = jnp.bfloat16

D_MODEL = 2048
DEPTH = 4
ATT_HEADS = 8
ATT_HEAD_DIM = 128
ATT_WIDTH = ATT_HEADS * ATT_HEAD_DIM
DILATED_PATTERNS = ((128, 1), (512, 4), (2048, 16))
ATT_BLOCK = 128
ATT_GROUP = (16, 8, 4)
ATT_SCALE = ATT_HEAD_DIM ** -0.5
N_REL_BUCKETS = 32
REL_MAX_DIST = 2048
SSM_WIDTH = 512
SSM_GROUP = 16
SSM_GROUPS = SSM_WIDTH // SSM_GROUP
SSM_STATE = 64
SSM_QUARTERS = 4
SSM_QW = SSM_GROUPS * SSM_STATE // SSM_QUARTERS
SSM_QC = SSM_WIDTH // SSM_QUARTERS
SGU_WIDTH = 512
SGU_HEADS = 4
SGU_HEAD_DIM = SGU_WIDTH // SGU_HEADS
SGU_CHUNK = 128
IN_COLS = 3 * ATT_WIDTH + SSM_WIDTH + 2 * SGU_WIDTH
FFN_HIDDEN = 5632
EPS = 1e-6
MASKED = -float("inf")
LANES = 128
CAST_BLOCK = 256
FFN_EDGE_ROWS = 256

VMEM_LIMIT_BYTES = 56 * 1024 * 1024


def _params(*sem):
    return pltpu.CompilerParams(dimension_semantics=sem, vmem_limit_bytes=VMEM_LIMIT_BYTES)


def _layer_spec(arr, layer):
    zeros = (0,) * (arr.ndim - 1)
    return pl.BlockSpec((None,) + arr.shape[1:], lambda *_: (layer,) + zeros)


def _rms(x, g):
    return x * lax.rsqrt(jnp.mean(x * x, axis=-1, keepdims=True) + EPS) * g


def _in_proj_kernel(x_ref, g_ref, w_ref, *rest, keep_from, tiles_per_seq):
    h_ref = rest[-1]
    outs = rest[-4:-1] if len(rest) > 2 else rest[:1]
    z_ref = outs[0]
    i = pl.program_id(0)
    j = pl.program_id(1)

    tm, tn = z_ref.shape
    chunk = min(tm, FFN_EDGE_ROWS)

    @pl.when(j == 0)
    def _():
        for r in range(0, tm, chunk):
            rows = pl.ds(r, chunk)
            h = _rms(x_ref[rows, :], g_ref[...]).astype(BF16)
            h_ref[rows, :] = h
            z_ref[rows, :] = jnp.dot(h, w_ref[...], preferred_element_type=F32)

    @pl.when(j > 0)
    def _():
        _in_proj_tile(i, j, h_ref, w_ref, outs, keep_from, tiles_per_seq)


def _in_proj_tile(i, j, h_ref, w_ref, outs, keep_from, tiles_per_seq):
    z_ref = outs[0]
    res = jnp.dot(h_ref[...], w_ref[...], preferred_element_type=F32)
    z_ref[...] = res

    if len(outs) > 1:
        tm, tn = res.shape
        keep = (i % tiles_per_seq) >= keep_from
        by_tile = {}
        for buf_ref, col0 in ((outs[1], ATT_WIDTH), (outs[2], 2 * ATT_WIDTH)):
            for head in range(ATT_HEADS):
                col = col0 + head * ATT_HEAD_DIM
                by_tile.setdefault(col // tn, []).append((buf_ref, head, col % tn))
        for tile, heads in by_tile.items():
            @pl.when(keep & (j == tile))
            def _(heads=heads):
                for buf_ref, head, off in heads:
                    rows = pl.ds(head, tm, stride=ATT_HEADS)
                    buf_ref[0, 0, rows, :] = res[:, off:off + ATT_HEAD_DIM]


def in_proj(x, g, w, layer, tm, tn, kv=None):
    m, k = x.shape
    n = w.shape[1]
    assert tn % ATT_HEAD_DIM == 0
    in_specs = [pl.BlockSpec((tm, k), lambda i, j: (i, 0)),
                _layer_spec(g, layer),
                pl.BlockSpec((k, tn), lambda i, j: (0, j))]
    z_spec = pl.BlockSpec((tm, tn), lambda i, j: (i, j))
    z_shape = jax.ShapeDtypeStruct((m, n), F32)
    scratch = [pltpu.VMEM((tm, k), BF16)]
    if kv is None:
        kern = functools.partial(_in_proj_kernel, keep_from=0, tiles_per_seq=1)
        return pl.pallas_call(
            kern, grid=(m // tm, n // tn), in_specs=in_specs, out_specs=z_spec, out_shape=z_shape,
            scratch_shapes=scratch, compiler_params=_params("parallel", "arbitrary"),
            name="in_proj",
        )(x, g, w)
    seq, n_keep, k_buf, v_buf = kv
    tiles_per_seq = seq // tm
    keep_from = (seq - n_keep) // tm
    assert seq % tm == 0 and (seq - n_keep) % tm == 0
    assert tn <= ATT_WIDTH

    def buf_map(i, j):
        return (layer, i // tiles_per_seq, jnp.maximum(i % tiles_per_seq - keep_from, 0), 0)

    buf_spec = pl.BlockSpec((1, 1, tm * ATT_HEADS, ATT_HEAD_DIM), buf_map)
    any_spec = pl.BlockSpec(memory_space=pl.ANY)
    kern = functools.partial(_in_proj_kernel, keep_from=keep_from, tiles_per_seq=tiles_per_seq)
    return pl.pallas_call(
        kern, grid=(m // tm, n // tn),
        in_specs=in_specs + [any_spec, any_spec],
        out_specs=[z_spec, buf_spec, buf_spec],
        out_shape=[z_shape, jax.ShapeDtypeStruct(k_buf.shape, F32),
                   jax.ShapeDtypeStruct(v_buf.shape, F32)],
        input_output_aliases={3: 1, 4: 2},
        scratch_shapes=scratch, compiler_params=_params("arbitrary", "arbitrary"),
        name="in_proj_kv",
    )(x, g, w, k_buf, v_buf)


def _attn_prompt_kernel(q_ref, k_ref, v_ref, b_ref, o_ref, acc_ref, m_ref, l_ref, *, seq):
    qb = ATT_BLOCK
    n_pat = len(DILATED_PATTERNS)

    for p, (_, d) in enumerate(DILATED_PATTERNS):
        shift = int(math.log2(d))
        whole = seq // d == 2 * qb
        blocks_per_trip = ATT_GROUP[p]
        units = blocks_per_trip // 2 if whole else blocks_per_trip

        def group(g, carry, p=p, d=d, shift=shift, whole=whole, units=units):
            rows, biases = [], []
            for u in range(units):
                i = g * units + u
                if whole:
                    rows.append((pl.ds(i, 2 * qb, stride=d), pl.ds(i, 2 * qb, stride=d)))
                    biases.append(None)
                    continue
                r = i & (d - 1)
                b = i >> shift
                q_start = r + b * (qb * d)
                k_start = r + jnp.maximum(b - 1, 0) * (qb * d)
                if d == 1:
                    rows.append((pl.ds(pl.multiple_of(q_start, qb), qb),
                                 pl.ds(pl.multiple_of(k_start, qb), 2 * qb)))
                else:
                    rows.append((pl.ds(q_start, qb, stride=d), pl.ds(k_start, 2 * qb, stride=d)))
                biases.append(jnp.where(b == 0, 1, 0))
            nt = (((1,), (1,)), ((), ()))
            scores = [lax.dot_general(q_ref[0, qr, :].astype(BF16), k_ref[0, kr, :].astype(BF16), nt,
                                      preferred_element_type=F32) for qr, kr in rows]
            probs = []
            for (qr, _), first, s in zip(rows, biases, scores):
                if whole:
                    bias = jnp.concatenate([b_ref[p, 1, 0], b_ref[p, 0, 0]], axis=0)
                else:
                    bias = b_ref[p, first, 0]
                s = s * ATT_SCALE + bias
                m_blk = jnp.max(s, axis=1, keepdims=True)
                pe = jnp.exp(s - m_blk)
                wide = (s.shape[0], ATT_HEAD_DIM)
                m_ref[p, qr, :] = jnp.broadcast_to(m_blk, wide)
                l_ref[p, qr, :] = jnp.broadcast_to(jnp.sum(pe, axis=1, keepdims=True), wide)
                probs.append(pe.astype(BF16))
            for (qr, kr), pe in zip(rows, probs):
                acc_ref[p, qr, :] = jnp.dot(pe, v_ref[0, kr, :].astype(BF16),
                                            preferred_element_type=F32)
            return carry

        lax.fori_loop(0, seq // qb // blocks_per_trip, group, 0)

    rows_per = 2 * qb

    def merge(c, carry):
        rows = pl.ds(pl.multiple_of(c * rows_per, rows_per), rows_per)
        ms = [m_ref[p, rows, :] for p in range(n_pat)]
        m = functools.reduce(jnp.maximum, ms)
        num = den = None
        for p in range(n_pat):
            w = jnp.exp(ms[p] - m)
            num = w * acc_ref[p, rows, :] if num is None else num + w * acc_ref[p, rows, :]
            den = w * l_ref[p, rows, :] if den is None else den + w * l_ref[p, rows, :]
        o_ref[0, rows, :] = num / den
        return carry

    lax.fori_loop(0, seq // rows_per, merge, 0)


def attn_prompt(z3, bias):
    n, seq, _ = z3.shape
    h = ATT_HEADS
    kern = functools.partial(_attn_prompt_kernel, seq=seq)
    return pl.pallas_call(
        kern,
        grid=(n, h),
        in_specs=[pl.BlockSpec((1, seq, ATT_HEAD_DIM), lambda i, j: (i, 0, j)),
                  pl.BlockSpec((1, seq, ATT_HEAD_DIM), lambda i, j: (i, 0, h + j)),
                  pl.BlockSpec((1, seq, ATT_HEAD_DIM), lambda i, j: (i, 0, 2 * h + j)),
                  pl.BlockSpec((3, 2, 1, ATT_BLOCK, 2 * ATT_BLOCK), lambda i, j: (0, 0, j, 0, 0))],
        out_specs=pl.BlockSpec((1, seq, ATT_HEAD_DIM), lambda i, j: (i, 0, j)),
        out_shape=jax.ShapeDtypeStruct((n, seq, ATT_WIDTH), F32),
        scratch_shapes=[pltpu.VMEM((len(DILATED_PATTERNS), seq, ATT_HEAD_DIM), F32)] * 3,
        compiler_params=_params("parallel", "parallel"),
        name="attn_prompt",
    )(z3, z3, z3, bias)


def _attn_sample_kernel(q_ref, kn_ref, vn_ref, ko_ref, vo_ref, kt_ref, vt_ref,
                        bo_ref, bt_ref, bn_ref, o_ref):
    t = q_ref.shape[1]
    e = ATT_HEAD_DIM
    nh = ATT_HEADS
    n_pat = len(DILATED_PATTERNS)
    groups, kept_rows, _ = ko_ref.shape
    kept = kept_rows // nh
    tail = kt_ref.shape[0] // nh
    pad = jnp.zeros((ATT_BLOCK - t, e), F32)
    nt = (((1,), (1,)), ((), ()))
    heads = range(nh)
    cols = [slice(h * e, (h + 1) * e) for h in heads]

    def old_rows(ref, h):
        return ref[:, pl.ds(h, kept, stride=nh), :].reshape(groups * kept, e).astype(BF16)

    def tail_rows(ref, h):
        return ref[pl.ds(h, tail, stride=nh), :].astype(BF16)

    so, st, sn = [], [], []
    for h in heads:
        q = q_ref[0, :, cols[h]].astype(BF16)
        kn = jnp.concatenate([kn_ref[0, :, cols[h]], pad], axis=0).astype(BF16)
        so.append(lax.dot_general(q, old_rows(ko_ref, h), nt, preferred_element_type=F32) * ATT_SCALE)
        st.append(lax.dot_general(q, tail_rows(kt_ref, h), nt, preferred_element_type=F32) * ATT_SCALE)
        sn.append(lax.dot_general(q, kn, nt, preferred_element_type=F32) * ATT_SCALE)
    po, pt, pn, ls = [], [], [], []
    for h in heads:
        s_old = so[h] + bo_ref[h]
        sts = [st[h] + bt_ref[p, h] for p in range(n_pat)]
        sns = [sn[h] + bn_ref[p, h] for p in range(n_pat)]
        m = None
        for s in [s_old] + sts + sns:
            mx = jnp.max(s, axis=1, keepdims=True)
            m = mx if m is None else jnp.maximum(m, mx)
        poh = jnp.exp(s_old - m)
        pth = sum(jnp.exp(s - m) for s in sts)
        pnh = sum(jnp.exp(s - m) for s in sns)
        ls.append(jnp.sum(poh, axis=1, keepdims=True) + jnp.sum(pth, axis=1, keepdims=True)
                  + jnp.sum(pnh, axis=1, keepdims=True))
        po.append(poh.astype(BF16))
        pt.append(pth.astype(BF16))
        pn.append(pnh.astype(BF16))
    for h in heads:
        vn = jnp.concatenate([vn_ref[0, :, cols[h]], pad], axis=0).astype(BF16)
        acc = (jnp.dot(po[h], old_rows(vo_ref, h), preferred_element_type=F32)
               + jnp.dot(pt[h], tail_rows(vt_ref, h), preferred_element_type=F32)
               + jnp.dot(pn[h], vn, preferred_element_type=F32))
        o_ref[0, :, cols[h]] = acc / ls[h]


def _sample_cache_split(w, t):
    d_far = DILATED_PATTERNS[-1][1]
    tail = max(window for window, _ in DILATED_PATTERNS[:-1])
    assert t <= d_far and w % d_far == 0 and tail % d_far == 0 and tail <= w
    return d_far, tail, (w - tail) // d_far


def attn_sample(zs3, cache_k, cache_v, layer, bias_o, bias_t, bias_n):
    n, t, _ = zs3.shape
    h = ATT_HEADS
    e = ATT_HEAD_DIM
    depth, _, w = cache_k.shape[:3]
    d_far, tail, groups = _sample_cache_split(w, t)
    by_group = lambda c: c.reshape(depth, n, w // d_far, d_far * h, e)
    by_tail = lambda c: c.reshape(depth, n, w // tail, tail * h, e)
    old_spec = pl.BlockSpec((None, None, groups, t * h, e), lambda i: (layer, i, 0, 0, 0))
    tail_spec = pl.BlockSpec((None, None, None, tail * h, e), lambda i: (layer, i, w // tail - 1, 0, 0))
    return pl.pallas_call(
        _attn_sample_kernel,
        grid=(n,),
        in_specs=[pl.BlockSpec((1, t, ATT_WIDTH), lambda i: (i, 0, 0)),
                  pl.BlockSpec((1, t, ATT_WIDTH), lambda i: (i, 0, 1)),
                  pl.BlockSpec((1, t, ATT_WIDTH), lambda i: (i, 0, 2)),
                  old_spec, old_spec, tail_spec, tail_spec,
                  pl.BlockSpec(bias_o.shape, lambda i: (0, 0, 0)),
                  pl.BlockSpec(bias_t.shape, lambda i: (0, 0, 0, 0)),
                  pl.BlockSpec(bias_n.shape, lambda i: (0, 0, 0, 0))],
        out_specs=pl.BlockSpec((1, t, ATT_WIDTH), lambda i: (i, 0, 0)),
        out_shape=jax.ShapeDtypeStruct((n, t, ATT_WIDTH), F32),
        compiler_params=_params("parallel"),
        name="attn_sample",
    )(zs3, zs3, zs3, by_group(cache_k), by_group(cache_v), by_tail(cache_k), by_tail(cache_v),
      bias_o, bias_t, bias_n)


def _ssm_prompt_kernel(u_ref, bq_ref, a_ref, cq_ref, d_ref, wg_ref, bg_ref,
                       o_ref, hre_ref, him_ref, x_ref, h_ref, *, tb, pitch):
    step = pl.program_id(0)
    nq = SSM_QUARTERS
    qw = SSM_QW
    qc = SSM_QC

    @pl.when(step == 0)
    def _():
        h_ref[...] = jnp.zeros_like(h_ref)

    nl = 2 * qw // LANES
    hl = nl // 2

    for b in range(2):
        for k in range(nq):
            u = u_ref[b, :, k * qc:(k + 1) * qc].astype(BF16)
            x = jnp.dot(u, bq_ref[k], preferred_element_type=F32)
            for c in range(nl):
                x_ref[c, pl.ds((b * nq + k) * pitch, tb), :] = x[:, c * LANES:(c + 1) * LANES]

    a = [a_ref[:, c * LANES:(c + 1) * LANES] for c in range(nl)]

    def scan(t, h):
        rows = pl.ds(t, 2 * nq, stride=pitch)
        new = [None] * nl
        for c in range(hl):
            a_re, a_im, h_re, h_im = a[c], a[hl + c], h[c], h[hl + c]
            new[c] = a_re * h_re - a_im * h_im + x_ref[c, rows, :]
            new[hl + c] = a_re * h_im + a_im * h_re + x_ref[hl + c, rows, :]
        for c in range(nl):
            x_ref[c, rows, :] = new[c]
        return tuple(new)

    h0 = tuple(h_ref[:, c * LANES:(c + 1) * LANES] for c in range(nl))
    h_new = jnp.concatenate(lax.fori_loop(0, tb, scan, h0, unroll=8), axis=1)
    h_ref[...] = h_new

    for b in range(2):
        ys = []
        for k in range(nq):
            rows = pl.ds((b * nq + k) * pitch, tb)
            hs = jnp.concatenate([x_ref[c, rows, :] for c in range(nl)], axis=1).astype(BF16)
            ys.append(jnp.dot(hs, cq_ref[k], preferred_element_type=F32))
        y = jnp.concatenate(ys, axis=1) + d_ref[...] * u_ref[b]
        gate = jnp.dot(y.astype(BF16), wg_ref[...], preferred_element_type=F32) + bg_ref[...]
        o_ref[b] = y * jax.nn.sigmoid(gate)

    @pl.when(step == pl.num_programs(0) - 1)
    def _():
        hre_ref[...] = h_new[:, :qw]
        him_ref[...] = h_new[:, qw:]


def ssm_prompt(z3, layer, bq, a8, cq, d_skip, w_glu, b_glu, tb=512):
    n, seq, _ = z3.shape
    assert n == 2
    pitch = tb + 4
    ucol = 3 * ATT_WIDTH // SSM_WIDTH
    kern = functools.partial(_ssm_prompt_kernel, tb=tb, pitch=pitch)
    full = lambda *shape: pl.BlockSpec(shape, lambda s: (0,) * len(shape))
    return pl.pallas_call(
        kern,
        grid=(seq // tb,),
        in_specs=[pl.BlockSpec((n, tb, SSM_WIDTH), lambda s: (0, s, ucol))]
        + [_layer_spec(p, layer) for p in (bq, a8, cq, d_skip, w_glu, b_glu)],
        out_specs=[pl.BlockSpec((n, tb, SSM_WIDTH), lambda s: (0, s, 0)),
                   full(2 * SSM_QUARTERS, SSM_QW), full(2 * SSM_QUARTERS, SSM_QW)],
        out_shape=[jax.ShapeDtypeStruct((n, seq, SSM_WIDTH), F32),
                   jax.ShapeDtypeStruct((2 * SSM_QUARTERS, SSM_QW), F32),
                   jax.ShapeDtypeStruct((2 * SSM_QUARTERS, SSM_QW), F32)],
        scratch_shapes=[pltpu.VMEM((2 * SSM_QW // LANES, 2 * SSM_QUARTERS * pitch, LANES), F32),
                        pltpu.VMEM((2 * SSM_QUARTERS, 2 * SSM_QW), F32)],
        compiler_params=_params("arbitrary"),
        name="ssm_prompt",
    )(z3, bq, a8, cq, d_skip, w_glu, b_glu)


def _ssm_sample_kernel(u_ref, hre_ref, him_ref, bq_ref, a_ref, cq_ref, d_ref, wg_ref, bg_ref,
                       o_ref, ore_ref, oim_ref, x_ref):
    n, t, _ = u_ref.shape
    nq = SSM_QUARTERS
    qw = SSM_QW
    qc = SSM_QC
    nl = 2 * qw // LANES
    hl = nl // 2
    u_all = u_ref[...].reshape(n * t, SSM_WIDTH)
    ys = []
    for k in range(nq):
        u = u_all[:, k * qc:(k + 1) * qc]
        x = jnp.dot(u, bq_ref[k], preferred_element_type=F32, precision=lax.Precision.HIGHEST)
        for c in range(nl):
            x_ref[c] = x[:, c * LANES:(c + 1) * LANES]
        h = ([hre_ref[:, k * qw + c * LANES:k * qw + (c + 1) * LANES] for c in range(hl)]
             + [him_ref[:, k * qw + c * LANES:k * qw + (c + 1) * LANES] for c in range(hl)])
        for s in range(t):
            rows = pl.ds(s, n, stride=t)
            new = [None] * nl
            for c in range(hl):
                a_re = a_ref[k:k + 1, c * LANES:(c + 1) * LANES]
                a_im = a_ref[k:k + 1, (hl + c) * LANES:(hl + c + 1) * LANES]
                new[c] = a_re * h[c] - a_im * h[hl + c] + x_ref[c, rows, :]
                new[hl + c] = a_re * h[hl + c] + a_im * h[c] + x_ref[hl + c, rows, :]
            for c in range(nl):
                x_ref[c, rows, :] = new[c]
            h = new
        for c in range(hl):
            ore_ref[:, k * qw + c * LANES:k * qw + (c + 1) * LANES] = h[c]
            oim_ref[:, k * qw + c * LANES:k * qw + (c + 1) * LANES] = h[hl + c]
        hs = jnp.concatenate([x_ref[c] for c in range(nl)], axis=1).astype(BF16)
        ys.append(jnp.dot(hs, cq_ref[k].astype(BF16), preferred_element_type=F32))
    y = jnp.concatenate(ys, axis=1) + d_ref[...] * u_all
    gate = jnp.dot(y.astype(BF16), wg_ref[...], preferred_element_type=F32) + bg_ref[...]
    o_ref[...] = (y * jax.nn.sigmoid(gate)).reshape(n, t, SSM_WIDTH)


def ssm_sample(zs3, layer, h_re, h_im, bq32, a8, cq32, d_skip, w_glu, b_glu):
    n, t, _ = zs3.shape
    ucol = 3 * ATT_WIDTH // SSM_WIDTH
    ns = SSM_GROUPS * SSM_STATE
    full = lambda *shape: pl.BlockSpec(shape, lambda s: (0,) * len(shape))
    return pl.pallas_call(
        _ssm_sample_kernel,
        grid=(1,),
        in_specs=[pl.BlockSpec((n, t, SSM_WIDTH), lambda s: (0, 0, ucol))]
        + [_layer_spec(p, layer) for p in (h_re, h_im, bq32, a8, cq32, d_skip, w_glu, b_glu)],
        out_specs=[full(n, t, SSM_WIDTH), full(n, ns), full(n, ns)],
        out_shape=[jax.ShapeDtypeStruct((n, t, SSM_WIDTH), F32),
                   jax.ShapeDtypeStruct((n, ns), F32),
                   jax.ShapeDtypeStruct((n, ns), F32)],
        scratch_shapes=[pltpu.VMEM((2 * SSM_QW // LANES, n * t, LANES), F32)],
        compiler_params=_params("arbitrary"),
        name="ssm_sample",
    )(zs3, h_re, h_im, bq32, a8, cq32, d_skip, w_glu, b_glu)


def _sgu_norm_v(v, g):
    gv = jax.nn.gelu(v)
    vc = gv - jnp.mean(gv, axis=-1, keepdims=True)
    return vc * lax.rsqrt(jnp.mean(vc * vc, axis=-1, keepdims=True) + EPS) * g


def _sgu_causal(w_ref):
    c = SGU_CHUNK
    keep = lax.broadcasted_iota(jnp.int32, (c, c), 0) >= lax.broadcasted_iota(jnp.int32, (c, c), 1)
    return [jnp.where(keep, w_ref[h], 0.0).astype(BF16) for h in range(SGU_HEADS)]


def _sgu_prompt_kernel(u_ref, v_ref, g_ref, w_ref, b_ref, o_ref):
    c = SGU_CHUNK
    e = SGU_HEAD_DIM
    rows = u_ref.shape[1]
    w = _sgu_causal(w_ref)
    gu = jax.nn.gelu(u_ref[0])
    gv = _sgu_norm_v(v_ref[0], g_ref[...]).astype(BF16)
    for ci in range(rows // c):
        for h in range(SGU_HEADS):
            mixed = jnp.dot(w[h], gv[ci * c:(ci + 1) * c, h * e:(h + 1) * e],
                            preferred_element_type=F32) + b_ref[h]
            o_ref[0, ci * c:(ci + 1) * c, h * e:(h + 1) * e] = (
                gu[ci * c:(ci + 1) * c, h * e:(h + 1) * e] * mixed)


def sgu_prompt(z3, layer, g, w_sp, b_rows, rows=2048):
    n, seq, _ = z3.shape
    ucol = (3 * ATT_WIDTH + SSM_WIDTH) // SGU_WIDTH
    return pl.pallas_call(
        _sgu_prompt_kernel,
        grid=(n, seq // rows),
        in_specs=[pl.BlockSpec((1, rows, SGU_WIDTH), lambda i, j: (i, j, ucol)),
                  pl.BlockSpec((1, rows, SGU_WIDTH), lambda i, j: (i, j, ucol + 1)),
                  _layer_spec(g, layer), _layer_spec(w_sp, layer), _layer_spec(b_rows, layer)],
        out_specs=pl.BlockSpec((1, rows, SGU_WIDTH), lambda i, j: (i, j, 0)),
        out_shape=jax.ShapeDtypeStruct((n, seq, SGU_WIDTH), F32),
        compiler_params=_params("parallel", "parallel"),
        name="sgu_prompt",
    )(z3, z3, g, w_sp, b_rows)


def _sgu_sample_kernel(u_ref, v_ref, g_ref, w_ref, b_ref, o_ref, gv_ref):
    c = SGU_CHUNK
    e = SGU_HEAD_DIM
    n, t, _ = u_ref.shape
    w = _sgu_causal(w_ref)
    for i in range(n):
        gu = jax.nn.gelu(u_ref[i])
        gv = _sgu_norm_v(v_ref[i], g_ref[...])
        gv_ref[i] = gv
        gvp = jnp.concatenate([gv, jnp.zeros((c - t, SGU_WIDTH), F32)], axis=0).astype(BF16)
        for h in range(SGU_HEADS):
            mixed = jnp.dot(w[h], gvp[:, h * e:(h + 1) * e], preferred_element_type=F32) + b_ref[h]
            o_ref[i, :, h * e:(h + 1) * e] = gu[:, h * e:(h + 1) * e] * mixed[:t]


def sgu_sample(zs3, layer, g, w_sp, b_rows):
    n, t, _ = zs3.shape
    ucol = (3 * ATT_WIDTH + SSM_WIDTH) // SGU_WIDTH
    return pl.pallas_call(
        _sgu_sample_kernel,
        grid=(1,),
        in_specs=[pl.BlockSpec((n, t, SGU_WIDTH), lambda i: (0, 0, ucol)),
                  pl.BlockSpec((n, t, SGU_WIDTH), lambda i: (0, 0, ucol + 1)),
                  _layer_spec(g, layer), _layer_spec(w_sp, layer), _layer_spec(b_rows, layer)],
        out_specs=[pl.BlockSpec((n, t, SGU_WIDTH), lambda i: (0, 0, 0)),
                   pl.BlockSpec((n, t, SGU_WIDTH), lambda i: (0, 0, 0))],
        out_shape=[jax.ShapeDtypeStruct((n, t, SGU_WIDTH), F32),
                   jax.ShapeDtypeStruct((n, t, SGU_WIDTH), F32)],
        compiler_params=_params("arbitrary"),
        name="sgu_sample",
    )(zs3, zs3, g, w_sp, b_rows)


def _out_proj_kernel(att_ref, ssm_ref, sgu_ref, x_ref, w_ref, gm_ref, gp_ref, o_ref):
    a0 = ATT_WIDTH
    a1 = ATT_WIDTH + SSM_WIDTH
    a = _rms(att_ref[...], gm_ref[:, :a0]).astype(BF16)
    s = _rms(ssm_ref[...], gm_ref[:, a0:a1]).astype(BF16)
    c = _rms(sgu_ref[...], gm_ref[:, a1:]).astype(BF16)
    mixed = jnp.concatenate([a, s, c], axis=1)
    y = jnp.dot(mixed, w_ref[...], preferred_element_type=F32)
    o_ref[...] = x_ref[...] + _rms(y, gp_ref[...])


def out_proj(att, ssm, sgu, x, w, layer, gm, gp, tm):
    m, dm = x.shape
    row = lambda width: pl.BlockSpec((tm, width), lambda i: (i, 0))
    return pl.pallas_call(
        _out_proj_kernel,
        grid=(m // tm,),
        in_specs=[row(ATT_WIDTH), row(SSM_WIDTH), row(SGU_WIDTH), row(dm),
                  pl.BlockSpec(w.shape, lambda i: (0, 0)), _layer_spec(gm, layer), _layer_spec(gp, layer)],
        out_specs=row(dm),
        out_shape=jax.ShapeDtypeStruct((m, dm), F32),
        compiler_params=_params("parallel"),
        name="out_proj",
    )(att, ssm, sgu, x, w, gm, gp)


def _ffn_kernel(x_ref, gpre_ref, wg_ref, wu_ref, wd_ref, gpost_ref, *rest, n_cast, has_side):
    n_side = int(has_side)
    xs_ref = rest[0] if has_side else None
    w32_refs = rest[n_side:n_side + n_cast]
    o_ref = rest[n_side + n_cast]
    os_ref = rest[n_side + n_cast + 1] if has_side else None
    w16_refs = rest[2 * n_side + n_cast + 1:2 * n_side + 2 * n_cast + 1]
    h_ref = rest[2 * n_side + 2 * n_cast + 1]
    hs_ref, accs_ref = rest[-2:] if has_side else (None, None)
    i = pl.program_id(0)
    j = pl.program_id(1)
    last = pl.num_programs(1) - 1
    tm = x_ref.shape[0]
    chunk = min(tm, FFN_EDGE_ROWS)

    def partial_out(h):
        gate = jnp.dot(h, wg_ref[...], preferred_element_type=F32)
        up = jnp.dot(h, wu_ref[...], preferred_element_type=F32)
        act = (jax.nn.silu(gate) * up).astype(BF16)
        return jnp.dot(act, wd_ref[...], preferred_element_type=F32)

    def cast_side_stream():
        for src_ref, dst_ref in zip(w32_refs, w16_refs):
            dst_ref[...] = src_ref[...].astype(BF16)

    def side_rows(phase):
        if not has_side:
            return

        @pl.when(i == 0)
        def _():
            if phase == "first":
                hs_ref[...] = _rms(xs_ref[...], gpre_ref[...]).astype(BF16)
                accs_ref[...] = partial_out(hs_ref[...])
            elif phase == "middle":
                accs_ref[...] += partial_out(hs_ref[...])
            else:
                f = accs_ref[...] + partial_out(hs_ref[...])
                os_ref[...] = xs_ref[...] + _rms(f, gpost_ref[...])

    @pl.when(j == 0)
    def _():
        for r in range(0, tm, chunk):
            rows = pl.ds(r, chunk)
            h = _rms(x_ref[rows, :], gpre_ref[...]).astype(BF16)
            h_ref[rows, :] = h
            o_ref[rows, :] = partial_out(h)
        cast_side_stream()
        side_rows("first")

    @pl.when((j > 0) & (j < last))
    def _():
        o_ref[...] += partial_out(h_ref[...])
        cast_side_stream()
        side_rows("middle")

    @pl.when(j == last)
    def _():
        for r in range(0, tm, chunk):
            rows = pl.ds(r, chunk)
            f = o_ref[rows, :] + partial_out(h_ref[rows, :])
            o_ref[rows, :] = x_ref[rows, :] + _rms(f, gpost_ref[...])
        cast_side_stream()
        side_rows("last")


def ffn(x, gpre, wg, wu, wd, layer, gpost, tm, tf, cast=(), side=None):
    m, dm = x.shape
    f = wg.shape[1]
    gi, gj = m // tm, f // tf
    assert gj >= 2

    def spread(dim, steps):
        return next(b for b in range(CAST_BLOCK, dim + 1, CAST_BLOCK) if dim % b == 0 and dim // b <= steps)

    cast_in, cast_out, cast_shape = [], [], []
    for w32, lyr, rows_follow_i in cast:
        _, r, c = w32.shape
        if rows_follow_i:
            assert r % gi == 0
            blk = (r // gi, spread(c, gj))
            imap = lambda i, j, last=c // blk[1] - 1: (i, jnp.minimum(j, last))
        else:
            assert c % gi == 0
            blk = (spread(r, gj), c // gi)
            imap = lambda i, j, last=r // blk[0] - 1: (jnp.minimum(j, last), i)
        cast_in.append(pl.BlockSpec((None,) + blk, lambda i, j, imap=imap, lyr=lyr: (lyr,) + imap(i, j)))
        cast_out.append(pl.BlockSpec(blk, imap))
        cast_shape.append(jax.ShapeDtypeStruct((r, c), BF16))
    has_side = side is not None
    side_spec = [pl.BlockSpec(side.shape, lambda i, j: (0, 0))] if has_side else []
    side_shape = [jax.ShapeDtypeStruct(side.shape, F32)] if has_side else []
    side_scratch = [pltpu.VMEM(side.shape, BF16), pltpu.VMEM(side.shape, F32)] if has_side else []
    kern = functools.partial(_ffn_kernel, n_cast=len(cast), has_side=has_side)
    res = pl.pallas_call(
        kern,
        grid=(gi, gj),
        in_specs=[pl.BlockSpec((tm, dm), lambda i, j: (i, 0)),
                  _layer_spec(gpre, layer),
                  pl.BlockSpec((dm, tf), lambda i, j: (0, j)),
                  pl.BlockSpec((dm, tf), lambda i, j: (0, j)),
                  pl.BlockSpec((tf, dm), lambda i, j: (j, 0)),
                  _layer_spec(gpost, layer)] + side_spec + cast_in,
        out_specs=[pl.BlockSpec((tm, dm), lambda i, j: (i, 0))] + side_spec + cast_out,
        out_shape=[jax.ShapeDtypeStruct((m, dm), F32)] + side_shape + cast_shape,
        scratch_shapes=[pltpu.VMEM((tm, dm), BF16)] + side_scratch,
        compiler_params=_params("arbitrary", "arbitrary"),
        name="ffn_cast" if cast else "ffn",
    )(x, gpre, wg, wu, wd, gpost, *([side] if has_side else []), *[w for w, _, _ in cast])
    n_side = int(has_side)
    return res[0], res[1 + n_side:], (res[1] if has_side else None)


def _t5_bucket(dist):
    max_exact = N_REL_BUCKETS // 2
    df = jnp.maximum(dist, 1).astype(F32)
    large = max_exact + (jnp.log(df / max_exact) / math.log(REL_MAX_DIST / max_exact)
                         * (N_REL_BUCKETS - max_exact)).astype(jnp.int32)
    large = jnp.minimum(large, N_REL_BUCKETS - 1)
    return jnp.where(dist < max_exact, dist, large)


def _strided_bias(rel_bias, dilation, n_steps):
    dist = jnp.arange(n_steps + 1, dtype=jnp.int32) * dilation
    return rel_bias[_t5_bucket(dist)].astype(F32)


def _bias_lookup(sb, steps, ok):
    onehot = (steps[..., None] == jnp.arange(sb.shape[0])).astype(F32)
    vals = jnp.einsum('...j,jh->h...', onehot, sb, precision=lax.Precision.HIGHEST)
    return jnp.where(ok[None], vals, MASKED)


def _prompt_bias_tables(rel_bias):
    qb = ATT_BLOCK
    span = jnp.arange(3 * qb - 1)
    tables = []
    for window, d in DILATED_PATTERNS:
        nk = window // d
        sb = _strided_bias(rel_bias, d, nk)

        def band(delta):
            line = _bias_lookup(sb, jnp.clip(delta, 0, nk), (delta >= 0) & (delta <= nk))
            return _toeplitz(line, qb, 2 * qb)

        tables.append(jnp.stack([band(span - (qb - 1)), band(span - (2 * qb - 1))]))
    return jnp.stack(tables).astype(F32)


def _toeplitz(line, n_rows, n_cols):
    lh = n_rows + n_cols - 1
    w = jnp.concatenate([line[..., ::-1], jnp.zeros(line.shape[:-1] + (1,), line.dtype)], axis=-1)
    tiled = jnp.tile(w, (1,) * (w.ndim - 1) + (n_rows,))[..., :n_rows * lh]
    return tiled.reshape(line.shape[:-1] + (n_rows, lh))[..., n_rows - 1:n_rows - 1 + n_cols]


def _sample_bias_tables(rel_bias, w, t):
    d_far, tail, groups = _sample_cache_split(w, t)
    tq = jnp.arange(t)[:, None]
    old_slots = (jnp.arange(groups)[:, None] * d_far + jnp.arange(t)[None, :]).reshape(1, -1)
    old_delta = w + tq - old_slots
    tail_delta = w + tq - (w - tail + jnp.arange(tail))[None, :]
    new_delta = tq - jnp.arange(ATT_BLOCK)[None, :]
    out_t, out_n = [], []
    for window, d in DILATED_PATTERNS:
        nk = window // d
        sb = _strided_bias(rel_bias, d, nk)

        def table(delta, extra_ok):
            ok = (delta >= 0) & (delta % d == 0) & (delta <= nk * d) & extra_ok
            return _bias_lookup(sb, jnp.clip(delta // d, 0, nk), ok)

        out_t.append(table(tail_delta, True))
        out_n.append(table(new_delta, jnp.arange(ATT_BLOCK)[None, :] < t))
        if d == d_far:
            out_o = table(old_delta, True)
        else:
            assert window <= tail
    return out_o.astype(F32), jnp.stack(out_t).astype(F32), jnp.stack(out_n).astype(F32)


def _ssm_tables(lam_re, lam_im, log_step, b_re, b_im, c_re, c_im):
    g, p, cg = SSM_GROUPS, SSM_STATE, SSM_GROUP
    nq = SSM_QUARTERS
    gq = g // nq
    step = jnp.exp(log_step)[:, None]
    mag = jnp.exp(lam_re * step)
    a_re = mag * jnp.cos(lam_im * step)
    a_im = mag * jnp.sin(lam_im * step)
    den = lam_re * lam_re + lam_im * lam_im
    coef_re = ((a_re - 1.0) * lam_re + a_im * lam_im) / den
    coef_im = (a_im * lam_re - (a_re - 1.0) * lam_im) / den
    bb_re = coef_re[..., None] * b_re - coef_im[..., None] * b_im
    bb_im = coef_re[..., None] * b_im + coef_im[..., None] * b_re
    same_group = np.arange(gq * cg)[:, None] // cg == np.arange(gq * p)[None, :] // p

    def in_mat(bb):
        t = jnp.transpose(bb, (0, 2, 1)).reshape(nq, gq * cg, 1, p)
        t = jnp.broadcast_to(t, (nq, gq * cg, gq, p)).reshape(nq, gq * cg, gq * p)
        return jnp.where(same_group, t, 0.0)

    def out_mat(c):
        t = jnp.transpose(c, (0, 2, 1)).reshape(nq, gq * p, 1, cg)
        t = jnp.broadcast_to(t, (nq, gq * p, gq, cg)).reshape(nq, gq * p, gq * cg)
        return jnp.where(same_group.T, t, 0.0)

    bq = jnp.concatenate([in_mat(bb_re), in_mat(bb_im)], axis=2)
    cq = jnp.concatenate([out_mat(c_re), -out_mat(c_im)], axis=1)
    aq = jnp.concatenate([a_re.reshape(nq, gq * p), a_im.reshape(nq, gq * p)], axis=1)
    a8 = jnp.concatenate([aq, aq], axis=0)
    return bq, cq, a8


def _rows(v):
    return v.reshape(v.shape[0], 1, -1)


def kernel(x_prompt, x_sample, cache_attn_k, cache_attn_v, state_ssm_re, state_ssm_im, rel_bias, w_in, w_out, g_pre_mix, g_post_mix, g_mix_out, ssm_lam_re, ssm_lam_im, ssm_log_step, ssm_b_re, ssm_b_im, ssm_c_re, ssm_c_im, ssm_d, ssm_w_glu, ssm_b_glu, sgu_g, sgu_w, sgu_b, g_pre_ffn, g_post_ffn, w_gate, w_up, w_down):
    nb, seq, dm = x_prompt.shape
    ns, ts, _ = x_sample.shape
    wbuf = cache_attn_k.shape[2]
    n_keep = min(DILATED_PATTERNS[-1][0], seq)
    h, e = ATT_HEADS, ATT_HEAD_DIM

    bias_p = _prompt_bias_tables(rel_bias)
    bias_o, bias_t, bias_n = _sample_bias_tables(rel_bias, wbuf, ts)
    k_buf = lax.empty((DEPTH, nb, n_keep * h, e), F32)
    v_buf = lax.empty((DEPTH, nb, n_keep * h, e), F32)
    st_re = state_ssm_re.reshape(DEPTH, ns, SSM_GROUPS * SSM_STATE)
    st_im = state_ssm_im.reshape(DEPTH, ns, SSM_GROUPS * SSM_STATE)

    xp = x_prompt.reshape(nb * seq, dm)
    xs = x_sample.reshape(ns * ts, dm)
    outs = {k: [] for k in ('rp', 'ip', 'ks', 'vs', 'rs', 'is', 'us')}

    big = (w_in, w_out, w_gate, w_up, w_down)
    big16 = tuple(w[0].astype(BF16) for w in big)
    w_glu16 = ssm_w_glu.astype(BF16)
    bq, cq, a8 = jax.vmap(_ssm_tables)(ssm_lam_re, ssm_lam_im, ssm_log_step,
                                       ssm_b_re, ssm_b_im, ssm_c_re, ssm_c_im)
    bq16, cq16 = bq.astype(BF16), cq.astype(BF16)
    sgu_b_rows = jnp.broadcast_to(sgu_b[..., None], (DEPTH, SGU_HEADS, SGU_CHUNK, SGU_HEAD_DIM))
    g_pre, g_post, g_mix = _rows(g_pre_mix), _rows(g_post_mix), _rows(g_mix_out)
    g_pre_f, g_post_f = _rows(g_pre_ffn), _rows(g_post_ffn)
    d_skip, b_glu, g_sgu = _rows(ssm_d), _rows(ssm_b_glu), _rows(sgu_g)

    for l in range(DEPTH):
        w_in16, w_out16, w_gate16, w_up16, w_down16 = big16
        cast = [(w, l + 1, w is not w_down) for w in big] if l + 1 < DEPTH else []

        z, k_buf, v_buf = in_proj(xp, g_pre, w_in16, l, tm=1024, tn=768,
                                  kv=(seq, n_keep, k_buf, v_buf))
        z = z.reshape(nb, seq, IN_COLS)
        o_att = attn_prompt(z, bias_p)
        o_ssm, p_re, p_im = ssm_prompt(z, l, bq16, a8, cq16, d_skip, w_glu16, b_glu)
        o_sgu = sgu_prompt(z, l, g_sgu, sgu_w, sgu_b_rows)
        xp = out_proj(o_att.reshape(nb * seq, ATT_WIDTH), o_ssm.reshape(nb * seq, SSM_WIDTH),
                      o_sgu.reshape(nb * seq, SGU_WIDTH), xp, w_out16, l, g_mix, g_post, tm=512)
        outs['rp'].append(p_re.reshape(nb, SSM_GROUPS, SSM_STATE))
        outs['ip'].append(p_im.reshape(nb, SSM_GROUPS, SSM_STATE))

        zs = in_proj(xs, g_pre, w_in16, l, tm=ns * ts, tn=2304).reshape(ns, ts, IN_COLS)
        s_att = attn_sample(zs, cache_attn_k, cache_attn_v, l, bias_o, bias_t, bias_n)
        s_ssm, s_re, s_im = ssm_sample(zs, l, st_re, st_im, bq, a8, cq, d_skip, w_glu16, b_glu)
        s_sgu, s_gv = sgu_sample(zs, l, g_sgu, sgu_w, sgu_b_rows)
        xs = out_proj(s_att.reshape(ns * ts, ATT_WIDTH), s_ssm.reshape(ns * ts, SSM_WIDTH),
                      s_sgu.reshape(ns * ts, SGU_WIDTH), xs, w_out16, l, g_mix, g_post, tm=ns * ts)
        xp, next16, xs = ffn(xp, g_pre_f, w_gate16, w_up16, w_down16, l, g_post_f, tm=1024, tf=256,
                             cast=cast, side=xs)
        big16 = tuple(next16) or big16
        outs['ks'].append(zs[:, :, ATT_WIDTH:2 * ATT_WIDTH].reshape(ns, ts, h, e))
        outs['vs'].append(zs[:, :, 2 * ATT_WIDTH:3 * ATT_WIDTH].reshape(ns, ts, h, e))
        outs['rs'].append(s_re.reshape(ns, SSM_GROUPS, SSM_STATE))
        outs['is'].append(s_im.reshape(ns, SSM_GROUPS, SSM_STATE))
        outs['us'].append(s_gv)

    st = lambda key: jnp.stack(outs[key])
    kv_shape = (DEPTH, nb, n_keep, h, e)
    return (xp.reshape(nb, seq, dm), xs.reshape(ns, ts, dm),
            k_buf.reshape(kv_shape), v_buf.reshape(kv_shape), st('rp'), st('ip'),
            st('ks'), st('vs'), st('rs'), st('is'), st('us'))
```
